```python
import math
import jax, jax.numpy as jnp
from jax import lax
import numpy as np

D_MODEL = 1024
BATCH = 16
SEQ = 2048
DEPTH = 1

N_MEM = 256
MEM_HEADS = 4
MEM_HEAD_DIM = D_MODEL // MEM_HEADS
D_MIX = D_MODEL
D_LRU = D_MIX // 2
LRU_BLOCKS = 8
LRU_BLOCK_DIM = D_LRU // LRU_BLOCKS
CONV_WIDTH = 4
LRU_C = 8.0
MLA_HEADS = 4
QK_NOPE_DIM = 128
QK_ROPE_DIM = 64
V_HEAD_DIM = 128
Q_LORA_RANK = 384
KV_LORA_RANK = 256
D_MLA_OUT = MLA_HEADS * V_HEAD_DIM
ROPE_THETA = 10000.0
Q_BLOCK = 128
OFF_LRU_X = 0
OFF_LRU_GATE = OFF_LRU_X + D_LRU
OFF_CQ = OFF_LRU_GATE + D_LRU
OFF_CKV = OFF_CQ + Q_LORA_RANK
OFF_KPE = OFF_CKV + KV_LORA_RANK
D_IN = OFF_KPE + QK_ROPE_DIM
D_FF = -(-8 * D_MODEL // (3 * 256)) * 256
RMS_EPS = 1e-6
NEG_INF = -1e30

kernel_name = "hymba_rglru_mla_memory_layer"


def rmsnorm(x, g):
    xf = x.astype(jnp.float32)
    y = xf * lax.rsqrt(jnp.mean(xf * xf, axis=-1, keepdims=True) + RMS_EPS)
    return (y * g.astype(jnp.float32)).astype(x.dtype)


def causal_depthwise_conv(x, w, b):
    s = x.shape[1]
    xp = jnp.pad(x, ((0, 0), (CONV_WIDTH - 1, 0), (0, 0)))
    y = b
    for k in range(CONV_WIDTH):
        y = y + w[k] * xp[:, k:k + s]
    return y


def rg_lru(x, w_a, b_a, w_x, b_x, lam):
    bsz, s, _ = x.shape
    xb = x.reshape(bsz, s, LRU_BLOCKS, LRU_BLOCK_DIM)
    r = jax.nn.sigmoid(jnp.einsum('bshi,hij->bshj', xb, w_a).reshape(bsz, s, D_LRU) + b_a)
    i = jax.nn.sigmoid(jnp.einsum('bshi,hij->bshj', xb, w_x).reshape(bsz, s, D_LRU) + b_x)
    log_a = -LRU_C * r.astype(jnp.float32) * jax.nn.softplus(-lam.astype(jnp.float32))
    a = jnp.exp(log_a)
    u = jnp.sqrt(-jnp.expm1(2.0 * log_a)) * (i * x).astype(jnp.float32)

    def combine(c1, c2):
        a1, b1 = c1
        a2, b2 = c2
        return a1 * a2, a2 * b1 + b2

    _, h = lax.associative_scan(combine, (a, u), axis=1)
    return h.astype(x.dtype)


def apply_rope(x, cos, sin):
    half = x.shape[-1] // 2
    x1, x2 = x[..., :half], x[..., half:]
    return jnp.concatenate([x1 * cos - x2 * sin, x2 * cos + x1 * sin], axis=-1)


def causal_block_attention(q, k, v, scale):
    s = q.shape[1]
    outs = []
    for blk in range(s // Q_BLOCK):
        q0 = blk * Q_BLOCK
        kv_len = q0 + Q_BLOCK
        qb = q[:, q0:kv_len]
        kb = k[:, :kv_len]
        vb = v[:, :kv_len]
        sc = jnp.einsum('bqhd,bkhd->bhqk', qb, kb).astype(jnp.float32) * scale
        mask = (q0 + jnp.arange(Q_BLOCK))[:, None] >= jnp.arange(kv_len)[None, :]
        sc = jnp.where(mask[None, None], sc, NEG_INF)
        p = jax.nn.softmax(sc, axis=-1).astype(vb.dtype)
        outs.append(jnp.einsum('bhqk,bkhd->bqhd', p, vb))
    return jnp.concatenate(outs, axis=1)


def setup_inputs(seed: int = 0) -> dict:
    key = jax.random.key(seed)
    ks = iter(jax.random.split(key, 48))

    def w(shape, fan_in):
        return jax.random.normal(next(ks), shape, jnp.float32) * fan_in ** -0.5

    def gain(n):
        return 1.0 + 0.05 * jax.random.normal(next(ks), (DEPTH, n), jnp.float32)

    def bias(n, s=0.02):
        return s * jax.random.normal(next(ks), (DEPTH, n), jnp.float32)

    x = jax.random.normal(next(ks), (BATCH, SEQ, D_MODEL), jnp.float32)
    mem = jax.random.normal(next(ks), (BATCH, N_MEM, D_MODEL), jnp.float32)
    start = jax.random.randint(next(ks), (BATCH, 1), 0, 4096, dtype=jnp.int32)
    positions = (start + jnp.arange(SEQ, dtype=jnp.int32)[None, :]).astype(jnp.int32)
    a0 = jax.random.uniform(next(ks), (DEPTH, D_LRU), jnp.float32, minval=0.9, maxval=0.999)
    lru_lambda = jnp.log(a0) - jnp.log1p(-a0)

    return {
        "x": x,
        "mem": mem,
        "positions": positions,
        "g_pre_mix": gain(D_MODEL),
        "w_in": w((DEPTH, D_MODEL, D_IN), D_MODEL),
        "conv_w": w((DEPTH, CONV_WIDTH, D_LRU), CONV_WIDTH),
        "conv_b": bias(D_LRU),
        "lru_wa": w((DEPTH, LRU_BLOCKS, LRU_BLOCK_DIM, LRU_BLOCK_DIM), LRU_BLOCK_DIM),
        "lru_ba": bias(D_LRU, 0.1),
        "lru_wx": w((DEPTH, LRU_BLOCKS, LRU_BLOCK_DIM, LRU_BLOCK_DIM), LRU_BLOCK_DIM),
        "lru_bx": bias(D_LRU, 0.1),
        "lru_lambda": lru_lambda,
        "g_q_lat": gain(Q_LORA_RANK),
        "w_uq": w((DEPTH, Q_LORA_RANK, MLA_HEADS * (QK_NOPE_DIM + QK_ROPE_DIM)), Q_LORA_RANK),
        "g_kv_lat": gain(KV_LORA_RANK),
        "w_ukv": w((DEPTH, KV_LORA_RANK, MLA_HEADS * (QK_NOPE_DIM + V_HEAD_DIM)), KV_LORA_RANK),
        "g_lru_out": gain(D_LRU),
        "g_mla_out": gain(D_MLA_OUT),
        "w_out": w((DEPTH, D_MIX, D_MODEL), D_MIX),
        "g_post_mix": gain(D_MODEL),
        "g_pre_mem": gain(D_MODEL),
        "g_mem_kv": gain(D_MODEL),
        "w_mq": w((DEPTH, D_MODEL, D_MODEL), D_MODEL),
        "w_mk": w((DEPTH, D_MODEL, D_MODEL), D_MODEL),
        "w_mv": w((DEPTH, D_MODEL, D_MODEL), D_MODEL),
        "w_mo": w((DEPTH, D_MODEL, D_MODEL), D_MODEL),
        "g_post_mem": gain(D_MODEL),
        "g_pre_ffn": gain(D_MODEL),
        "w_gate": w((DEPTH, D_MODEL, D_FF), D_MODEL),
        "w_up": w((DEPTH, D_MODEL, D_FF), D_MODEL),
        "w_down": w((DEPTH, D_FF, D_MODEL), D_FF),
        "g_post_ffn": gain(D_MODEL),
    }


def reference(x, mem, positions, g_pre_mix, w_in, conv_w, conv_b, lru_wa, lru_ba, lru_wx,
              lru_bx, lru_lambda, g_q_lat, w_uq, g_kv_lat, w_ukv, g_lru_out, g_mla_out, w_out,
              g_post_mix, g_pre_mem, g_mem_kv, w_mq, w_mk, w_mv, w_mo, g_post_mem, g_pre_ffn,
              w_gate, w_up, w_down, g_post_ffn):
    bsz, s, _ = x.shape
    n_mem = mem.shape[1]
    mla_scale = 1.0 / math.sqrt(QK_NOPE_DIM + QK_ROPE_DIM)
    mem_scale = 1.0 / math.sqrt(MEM_HEAD_DIM)

    inv_freq = ROPE_THETA ** (-jnp.arange(0, QK_ROPE_DIM, 2, dtype=jnp.float32) / QK_ROPE_DIM)
    ang = positions.astype(jnp.float32)[..., None] * inv_freq
    cos = jnp.cos(ang)[:, :, None, :].astype(x.dtype)
    sin = jnp.sin(ang)[:, :, None, :].astype(x.dtype)

    h = x
    for l in range(DEPTH):
        xn = rmsnorm(h, g_pre_mix[l])
        z = xn @ w_in[l]
        lru_x = z[..., OFF_LRU_X:OFF_LRU_GATE]
        lru_gate = z[..., OFF_LRU_GATE:OFF_CQ]
        c_q = z[..., OFF_CQ:OFF_CKV]
        c_kv = z[..., OFF_CKV:OFF_KPE]
        k_pe = z[..., OFF_KPE:D_IN]

        u = causal_depthwise_conv(lru_x, conv_w[l], conv_b[l])
        hl = rg_lru(u, lru_wa[l], lru_ba[l], lru_wx[l], lru_bx[l], lru_lambda[l])
        y_lru = hl * jax.nn.gelu(lru_gate, approximate=True)

        q = (rmsnorm(c_q, g_q_lat[l]) @ w_uq[l]).reshape(bsz, s, MLA_HEADS, QK_NOPE_DIM + QK_ROPE_DIM)
        kv = (rmsnorm(c_kv, g_kv_lat[l]) @ w_ukv[l]).reshape(bsz, s, MLA_HEADS, QK_NOPE_DIM + V_HEAD_DIM)
        q_nope, q_pe = q[..., :QK_NOPE_DIM], q[..., QK_NOPE_DIM:]
        k_nope, v = kv[..., :QK_NOPE_DIM], kv[..., QK_NOPE_DIM:]
        q_pe = apply_rope(q_pe, cos, sin)
        k_pe_r = apply_rope(k_pe[:, :, None, :], cos, sin)
        q_full = jnp.concatenate([q_nope, q_pe], axis=-1)
        k_full = jnp.concatenate(
            [k_nope, jnp.broadcast_to(k_pe_r, (bsz, s, MLA_HEADS, QK_ROPE_DIM))], axis=-1)
        y_mla = causal_block_attention(q_full, k_full, v, mla_scale).reshape(bsz, s, D_MLA_OUT)

        y = jnp.concatenate([rmsnorm(y_lru, g_lru_out[l]), rmsnorm(y_mla, g_mla_out[l])], axis=-1)
        h = h + rmsnorm(y @ w_out[l], g_post_mix[l])

        xn = rmsnorm(h, g_pre_mem[l])
        mn = rmsnorm(mem, g_mem_kv[l])
        mq = (xn @ w_mq[l]).reshape(bsz, s, MEM_HEADS, MEM_HEAD_DIM)
        mk = (mn @ w_mk[l]).reshape(bsz, n_mem, MEM_HEADS, MEM_HEAD_DIM)
        mv = (mn @ w_mv[l]).reshape(bsz, n_mem, MEM_HEADS, MEM_HEAD_DIM)
        sc = jnp.einsum('bshd,bnhd->bhsn', mq, mk).astype(jnp.float32) * mem_scale
        p = jax.nn.softmax(sc, axis=-1).astype(mv.dtype)
        o = jnp.einsum('bhsn,bnhd->bshd', p, mv).reshape(bsz, s, D_MODEL)
        h = h + rmsnorm(o @ w_mo[l], g_post_mem[l])

        xn = rmsnorm(h, g_pre_ffn[l])
        f = (jax.nn.silu(xn @ w_gate[l]) * (xn @ w_up[l])) @ w_down[l]
        h = h + rmsnorm(f, g_post_ffn[l])
    return h
```

```python
import functools
import math

import jax
import jax.numpy as jnp
from jax import lax
from jax.experimental import pallas as pl
from jax.experimental.pallas import tpu as pltpu

F32 = jnp.float32
BF16 = jnp.bfloat16

D_MODEL = 1024
N_MEM = 256
MEM_HEADS = 4
MEM_HEAD_DIM = D_MODEL // MEM_HEADS
D_LRU = 512
LRU_BLOCKS = 8
LRU_BLOCK_DIM = D_LRU // LRU_BLOCKS
CONV_WIDTH = 4
LRU_C = 8.0
MLA_HEADS = 4
QK_NOPE_DIM = 128
QK_ROPE_DIM = 64
V_HEAD_DIM = 128
Q_LORA_RANK = 384
KV_LORA_RANK = 256
D_MLA_OUT = MLA_HEADS * V_HEAD_DIM
ROPE_THETA = 10000.0
D_FF = 2816
RMS_EPS = 1e-6
NEG_INF = -1e30

LANES = 128
SUBLANES = 8
HEAD_SLAB = 2 * LANES
Z_LRU_X = 0
Z_LRU_GATE = Z_LRU_X + D_LRU
Z_CQ = Z_LRU_GATE + D_LRU
Z_CKV = Z_CQ + Q_LORA_RANK
Z_KPE = Z_CKV + KV_LORA_RANK
Z_WIDTH = Z_KPE + LANES
LRU_GROUP = 256

TS_IN = 512
TQ_ATT = 256
TM_MIX = 512
TM_FFN = 512
FF_CHUNKS = ((0, 1024), (1024, 2048), (2048, D_FF))
VMEM_LIMIT = 56 * 1024 * 1024


def _rms(x, g):
    ms = jnp.mean(x * x, axis=-1, keepdims=True)
    return x * lax.rsqrt(ms + RMS_EPS) * g


def _dot(a, b):
    return jnp.dot(a, b, preferred_element_type=F32)


def _dot_nt(a, b):
    return lax.dot_general(a, b, (((1,), (1,)), ((), ())), preferred_element_type=F32)


def _const_spec(shape):
    return pl.BlockSpec(shape, lambda *_: (0,) * len(shape))


def _mem_kv_kernel(mem_ref, g_ref, wk_ref, wv_ref, mk_ref, mv_ref):
    mn = _rms(mem_ref[...], g_ref[...]).astype(BF16)
    mk_ref[...] = _dot(mn, wk_ref[...]).astype(BF16)
    mv_ref[...] = _dot(mn, wv_ref[...]).astype(BF16)


def _mem_kv(mem, g, wk, wv):
    bsz = mem.shape[0]
    blk = pl.BlockSpec((None, N_MEM, D_MODEL), lambda b: (b, 0, 0))
    return pl.pallas_call(
        _mem_kv_kernel,
        out_shape=(jax.ShapeDtypeStruct((bsz, N_MEM, D_MODEL), BF16),) * 2,
        grid=(bsz,),
        in_specs=[blk, _const_spec((1, D_MODEL)), _const_spec((D_MODEL, D_MODEL)),
                  _const_spec((D_MODEL, D_MODEL))],
        out_specs=(blk, blk),
        compiler_params=pltpu.CompilerParams(dimension_semantics=("parallel",),
                                             vmem_limit_bytes=VMEM_LIMIT),
        name="mem_kv",
    )(mem, g, wk, wv)


def _shift_rows(x, prev, d):
    r = pltpu.roll(x, d, 0)
    row = lax.broadcasted_iota(jnp.int32, prev.shape, 0)
    head = jnp.where(row < d, pltpu.roll(prev, d, 0), r[0:SUBLANES])
    return jnp.concatenate([head, r[SUBLANES:]], axis=0)


def _linear_scan(a, u, h0):
    t = a.shape[0]
    row = lax.broadcasted_iota(jnp.int32, a.shape, 0) % SUBLANES
    for s in (1, 2, 4):
        a_s = pltpu.roll(a, s, 0)
        u_s = pltpu.roll(u, s, 0)
        m = row >= s
        u = jnp.where(m, a * u_s + u, u)
        a = jnp.where(m, a * a_s, a)
    hs = []
    h = h0
    for g in range(t // SUBLANES):
        sl = slice(g * SUBLANES, (g + 1) * SUBLANES)
        hg = a[sl] * h + u[sl]
        hs.append(hg)
        h = hg[SUBLANES - 1:SUBLANES]
    return jnp.concatenate(hs, axis=0)


def _rope(t, cos, sin):
    half = QK_ROPE_DIM // 2
    return t * cos + pltpu.roll(t, half, 1) * sin - pltpu.roll(t, LANES - half, 1) * sin


def _in_proj_kernel(x_ref, pos_ref, invf_ref, gpre_ref, win_ref, convw_ref, convb_ref, wg_ref,
                    ba_ref, bx_ref, lam_ref, gq_ref, wuq_ref, gkv_ref, wukv_ref, glru_ref,
                    q_ref, k_ref, v_ref, ylru_ref, xprev_ref, hprev_ref):
    ts = x_ref.shape[0]

    @pl.when(pl.program_id(1) == 0)
    def _():
        xprev_ref[...] = jnp.zeros_like(xprev_ref)
        hprev_ref[...] = jnp.zeros_like(hprev_ref)

    xn = _rms(x_ref[...], gpre_ref[...]).astype(BF16)
    z = _dot(xn, win_ref[...])

    xl = z[:, Z_LRU_X:Z_LRU_GATE]
    prev = xprev_ref[...]
    u = convb_ref[...] + convw_ref[CONV_WIDTH - 1:CONV_WIDTH, :] * xl
    for d in range(1, CONV_WIDTH):
        k = CONV_WIDTH - 1 - d
        u = u + convw_ref[k:k + 1, :] * _shift_rows(xl, prev, d)
    xprev_ref[...] = xl[ts - SUBLANES:ts]

    ub = u.astype(BF16)
    pre = [_dot(ub[:, j * LRU_GROUP:(j + 1) * LRU_GROUP], wg_ref[j])
           for j in range(D_LRU // LRU_GROUP)]
    r = jax.nn.sigmoid(jnp.concatenate([p[:, :LRU_GROUP] for p in pre], axis=-1) + ba_ref[...])
    gi = jax.nn.sigmoid(jnp.concatenate([p[:, LRU_GROUP:] for p in pre], axis=-1) + bx_ref[...])
    nlam = -lam_ref[...]
    softplus = jnp.maximum(nlam, 0.0) + jnp.log1p(jnp.exp(-jnp.abs(nlam)))
    log_a = -LRU_C * r * softplus
    a = jnp.exp(log_a)
    v_in = jnp.sqrt(-jnp.tanh(log_a) * (a * a + 1.0)) * (gi * u)
    h = _linear_scan(a, v_in, hprev_ref[SUBLANES - 1:SUBLANES, :])
    hprev_ref[...] = h[ts - SUBLANES:ts]
    y = h * jax.nn.gelu(z[:, Z_LRU_GATE:Z_CQ], approximate=True)
    ylru_ref[...] = _rms(y, glru_ref[...]).astype(BF16)

    scale = 1.0 / math.sqrt(QK_NOPE_DIM + QK_ROPE_DIM)
    q = _dot(_rms(z[:, Z_CQ:Z_CKV], gq_ref[...]).astype(BF16), wuq_ref[...]) * scale
    kv = _dot(_rms(z[:, Z_CKV:Z_KPE], gkv_ref[...]).astype(BF16), wukv_ref[...])
    ang = pos_ref[...].astype(F32) * invf_ref[...]
    cos = jnp.cos(ang)
    sin = jnp.sin(ang)
    kpe = _rope(z[:, Z_KPE:Z_WIDTH], cos, sin).astype(BF16)
    for hd in range(MLA_HEADS):
        c0 = hd * HEAD_SLAB
        q_ref[:, c0:c0 + LANES] = q[:, c0:c0 + LANES].astype(BF16)
        q_ref[:, c0 + LANES:c0 + HEAD_SLAB] = _rope(q[:, c0 + LANES:c0 + HEAD_SLAB], cos, sin).astype(BF16)
        k_ref[:, c0:c0 + LANES] = kv[:, hd * LANES:(hd + 1) * LANES].astype(BF16)
        k_ref[:, c0 + LANES:c0 + HEAD_SLAB] = kpe
    v_ref[...] = kv[:, MLA_HEADS * QK_NOPE_DIM:].astype(BF16)


def _in_proj(x, pos3, invf, gpre, win, convw, convb, wg, ba, bx, lam, gq, wuq, gkv, wukv, glru):
    bsz, s, _ = x.shape
    ts = TS_IN
    tok = lambda w: pl.BlockSpec((None, ts, w), lambda b, i: (b, i, 0))
    in_specs = [
        tok(D_MODEL), tok(1), _const_spec((1, LANES)), _const_spec((1, D_MODEL)),
        _const_spec((D_MODEL, Z_WIDTH)), _const_spec((CONV_WIDTH, D_LRU)), _const_spec((1, D_LRU)),
        _const_spec((D_LRU // LRU_GROUP, LRU_GROUP, 2 * LRU_GROUP)),
        _const_spec((1, D_LRU)), _const_spec((1, D_LRU)), _const_spec((1, D_LRU)),
        _const_spec((1, Q_LORA_RANK)), _const_spec((Q_LORA_RANK, MLA_HEADS * HEAD_SLAB)),
        _const_spec((1, KV_LORA_RANK)), _const_spec((KV_LORA_RANK, 2 * D_MLA_OUT)),
        _const_spec((1, D_LRU)),
    ]
    out_shape = (
        jax.ShapeDtypeStruct((bsz, s, MLA_HEADS * HEAD_SLAB), BF16),
        jax.ShapeDtypeStruct((bsz, s, MLA_HEADS * HEAD_SLAB), BF16),
        jax.ShapeDtypeStruct((bsz, s, D_MLA_OUT), BF16),
        jax.ShapeDtypeStruct((bsz, s, D_LRU), BF16),
    )
    out_specs = (tok(MLA_HEADS * HEAD_SLAB), tok(MLA_HEADS * HEAD_SLAB), tok(D_MLA_OUT), tok(D_LRU))
    return pl.pallas_call(
        _in_proj_kernel,
        out_shape=out_shape,
        grid=(bsz, s // ts),
        in_specs=in_specs,
        out_specs=out_specs,
        scratch_shapes=[pltpu.VMEM((SUBLANES, D_LRU), F32), pltpu.VMEM((SUBLANES, D_LRU), F32)],
        compiler_params=pltpu.CompilerParams(dimension_semantics=("parallel", "arbitrary"),
                                             vmem_limit_bytes=VMEM_LIMIT),
        name="in_proj",
    )(x, pos3, invf, gpre, win, convw, convb, wg, ba, bx, lam, gq, wuq, gkv, wukv, glru)


def _mla_attn_kernel(q_ref, k_ref, v_ref, g_ref, o_ref, acc_ref):
    tq = q_ref.shape[0]
    i = pl.program_id(1)
    rowid = lax.broadcasted_iota(jnp.int32, (tq, tq), 0)
    colid = lax.broadcasted_iota(jnp.int32, (tq, tq), 1)
    causal = rowid >= colid

    for hd in range(MLA_HEADS):
        qh = q_ref[:, hd * HEAD_SLAB:(hd + 1) * HEAD_SLAB]

        def step(j, carry, masked, hd=hd, qh=qh):
            m, l, acc = carry
            start = pl.multiple_of(j * tq, tq)
            kj = k_ref[pl.ds(start, tq), hd * HEAD_SLAB:(hd + 1) * HEAD_SLAB]
            vj = v_ref[pl.ds(start, tq), hd * V_HEAD_DIM:(hd + 1) * V_HEAD_DIM]
            s = _dot_nt(qh, kj)
            if masked:
                s = jnp.where(causal, s, NEG_INF)
            m_new = jnp.maximum(m, jnp.max(s, axis=-1, keepdims=True))
            alpha = jnp.exp(m - m_new)
            p = jnp.exp(s - m_new)
            l = alpha * l + jnp.sum(p, axis=-1, keepdims=True)
            acc = alpha * acc + _dot(p.astype(BF16), vj)
            return m_new, l, acc

        init = (jnp.full((tq, 1), NEG_INF, F32), jnp.zeros((tq, 1), F32),
                jnp.zeros((tq, V_HEAD_DIM), F32))
        carry = lax.fori_loop(0, i, functools.partial(step, masked=False), init)
        _, l, acc = step(i, carry, masked=True)
        acc_ref[:, hd * V_HEAD_DIM:(hd + 1) * V_HEAD_DIM] = acc / l

    o_ref[...] = _rms(acc_ref[...], g_ref[...]).astype(BF16)


def _mla_attn(q, k, v, g):
    bsz, s, _ = q.shape
    tq = TQ_ATT
    return pl.pallas_call(
        _mla_attn_kernel,
        out_shape=jax.ShapeDtypeStruct((bsz, s, D_MLA_OUT), BF16),
        grid=(bsz, s // tq),
        in_specs=[
            pl.BlockSpec((None, tq, MLA_HEADS * HEAD_SLAB), lambda b, i: (b, i, 0)),
            pl.BlockSpec((None, s, MLA_HEADS * HEAD_SLAB), lambda b, i: (b, 0, 0)),
            pl.BlockSpec((None, s, D_MLA_OUT), lambda b, i: (b, 0, 0)),
            _const_spec((1, D_MLA_OUT)),
        ],
        out_specs=pl.BlockSpec((None, tq, D_MLA_OUT), lambda b, i: (b, i, 0)),
        scratch_shapes=[pltpu.VMEM((tq, D_MLA_OUT), F32)],
        compiler_params=pltpu.CompilerParams(dimension_semantics=("parallel", "arbitrary"),
                                             vmem_limit_bytes=VMEM_LIMIT),
        name="mla_attn",
    )(q, k, v, g)


def _mix_mem_kernel(ylru_ref, ymla_ref, x_ref, wout_ref, gpost_ref, gpre_ref, wmq_ref, mk_ref,
                    mv_ref, wmo_ref, gpm_ref, h_ref, o_scr):
    half = D_LRU
    yo = _dot(ylru_ref[...], wout_ref[0:half, :]) + _dot(ymla_ref[...], wout_ref[half:, :])
    h1 = x_ref[...] + _rms(yo, gpost_ref[...])

    xn = _rms(h1, gpre_ref[...]).astype(BF16)
    mq = (_dot(xn, wmq_ref[...]) * (1.0 / math.sqrt(MEM_HEAD_DIM))).astype(BF16)
    for hd in range(MEM_HEADS):
        sl = slice(hd * MEM_HEAD_DIM, (hd + 1) * MEM_HEAD_DIM)
        s = _dot_nt(mq[:, sl], mk_ref[:, sl])
        m = jnp.max(s, axis=-1, keepdims=True)
        p = jnp.exp(s - m)
        l = jnp.sum(p, axis=-1, keepdims=True)
        o_scr[:, sl] = (_dot(p.astype(BF16), mv_ref[:, sl]) / l).astype(BF16)
    h_ref[...] = h1 + _rms(_dot(o_scr[...], wmo_ref[...]), gpm_ref[...])


def _mix_mem(ylru, ymla, x, wout, gpost, gpre, wmq, mk, mv, wmo, gpm):
    bsz, s, _ = x.shape
    tm = TM_MIX
    tok = lambda w: pl.BlockSpec((None, tm, w), lambda b, i: (b, i, 0))
    memspec = pl.BlockSpec((None, N_MEM, D_MODEL), lambda b, i: (b, 0, 0))
    sq = _const_spec((D_MODEL, D_MODEL))
    vec = _const_spec((1, D_MODEL))
    return pl.pallas_call(
        _mix_mem_kernel,
        out_shape=jax.ShapeDtypeStruct((bsz, s, D_MODEL), F32),
        grid=(bsz, s // tm),
        in_specs=[tok(D_LRU), tok(D_MLA_OUT), tok(D_MODEL), sq, vec, vec, sq, memspec, memspec,
                  sq, vec],
        out_specs=tok(D_MODEL),
        scratch_shapes=[pltpu.VMEM((tm, D_MODEL), BF16)],
        compiler_params=pltpu.CompilerParams(dimension_semantics=("parallel", "parallel"),
                                             vmem_limit_bytes=VMEM_LIMIT),
        name="mix_mem",
    )(ylru, ymla, x, wout, gpost, gpre, wmq, mk, mv, wmo, gpm)


def _ffn_kernel(h_ref, gpre_ref, wg_ref, wu_ref, wd_ref, gpost_ref, o_ref):
    h = h_ref[...]
    xn = _rms(h, gpre_ref[...]).astype(BF16)
    f = None
    for lo, hi in FF_CHUNKS:
        act = (jax.nn.silu(_dot(xn, wg_ref[:, lo:hi])) * _dot(xn, wu_ref[:, lo:hi])).astype(BF16)
        part = _dot(act, wd_ref[lo:hi, :])
        f = part if f is None else f + part
    o_ref[...] = h + _rms(f, gpost_ref[...])


def _ffn(h, gpre, wg, wu, wd, gpost):
    m, _ = h.shape
    tm = TM_FFN
    single = pl.Buffered(1)
    tok = pl.BlockSpec((tm, D_MODEL), lambda i: (i, 0))
    vec = _const_spec((1, D_MODEL))
    return pl.pallas_call(
        _ffn_kernel,
        out_shape=jax.ShapeDtypeStruct((m, D_MODEL), F32),
        grid=(m // tm,),
        in_specs=[
            tok, vec,
            pl.BlockSpec((D_MODEL, D_FF), lambda i: (0, 0), pipeline_mode=single),
            pl.BlockSpec((D_MODEL, D_FF), lambda i: (0, 0), pipeline_mode=single),
            pl.BlockSpec((D_FF, D_MODEL), lambda i: (0, 0), pipeline_mode=single),
            vec,
        ],
        out_specs=tok,
        compiler_params=pltpu.CompilerParams(dimension_semantics=("parallel",),
                                             vmem_limit_bytes=VMEM_LIMIT),
        name="ffn",
    )(h, gpre, wg, wu, wd, gpost)


def _block_diag(w):
    eye = jnp.eye(LRU_BLOCKS, dtype=w.dtype)
    return jnp.einsum('hij,hg->higj', w, eye).reshape(D_LRU, D_LRU)


def _gate_weights(wa, wx):
    da, dx = _block_diag(wa), _block_diag(wx)
    groups = []
    for j in range(D_LRU // LRU_GROUP):
        sl = slice(j * LRU_GROUP, (j + 1) * LRU_GROUP)
        groups.append(jnp.concatenate([da[sl, sl], dx[sl, sl]], axis=1))
    return jnp.stack(groups).astype(BF16)


def _pad_in_proj(w_in):
    pad = jnp.zeros((D_MODEL, Z_WIDTH - w_in.shape[1]), w_in.dtype)
    return jnp.concatenate([w_in, pad], axis=1).astype(BF16)


def _q_weights(w_uq):
    dh = QK_NOPE_DIM + QK_ROPE_DIM
    w = w_uq.reshape(Q_LORA_RANK, MLA_HEADS, dh)
    pad = jnp.zeros((Q_LORA_RANK, MLA_HEADS, HEAD_SLAB - dh), w_uq.dtype)
    return jnp.concatenate([w, pad], axis=-1).reshape(Q_LORA_RANK, MLA_HEADS * HEAD_SLAB).astype(BF16)


def _kv_weights(w_ukv):
    w = w_ukv.reshape(KV_LORA_RANK, MLA_HEADS, QK_NOPE_DIM + V_HEAD_DIM)
    kn = w[:, :, :QK_NOPE_DIM].reshape(KV_LORA_RANK, MLA_HEADS * QK_NOPE_DIM)
    vv = w[:, :, QK_NOPE_DIM:].reshape(KV_LORA_RANK, MLA_HEADS * V_HEAD_DIM)
    return jnp.concatenate([kn, vv], axis=1).astype(BF16)


def _rope_freq_tile():
    half = QK_ROPE_DIM // 2
    inv_freq = ROPE_THETA ** (-jnp.arange(0, QK_ROPE_DIM, 2, dtype=F32) / QK_ROPE_DIM)
    return jnp.concatenate([inv_freq, inv_freq, jnp.zeros((LANES - 2 * half,), F32)]).reshape(1, LANES)


def kernel(x, mem, positions, g_pre_mix, w_in, conv_w, conv_b, lru_wa, lru_ba, lru_wx, lru_bx, lru_lambda, g_q_lat, w_uq, g_kv_lat, w_ukv, g_lru_out, g_mla_out, w_out, g_post_mix, g_pre_mem, g_mem_kv, w_mq, w_mk, w_mv, w_mo, g_post_mem, g_pre_ffn, w_gate, w_up, w_down, g_post_ffn):
    bsz, s, d = x.shape
    depth = w_in.shape[0]
    assert d == D_MODEL and s % TS_IN == 0 and s % TQ_ATT == 0 and s % TM_MIX == 0
    assert (bsz * s) % TM_FFN == 0 and mem.shape == (bsz, N_MEM, D_MODEL)
    pos3 = positions.reshape(bsz, s, 1)
    invf = _rope_freq_tile()
    row = lambda a: a.reshape(1, -1)

    h = x
    for l in range(depth):
        mk, mv = _mem_kv(mem, row(g_mem_kv[l]), w_mk[l].astype(BF16), w_mv[l].astype(BF16))
        q, k, v, ylru = _in_proj(
            h, pos3, invf, row(g_pre_mix[l]), _pad_in_proj(w_in[l]), conv_w[l], row(conv_b[l]),
            _gate_weights(lru_wa[l], lru_wx[l]), row(lru_ba[l]), row(lru_bx[l]), row(lru_lambda[l]),
            row(g_q_lat[l]), _q_weights(w_uq[l]), row(g_kv_lat[l]), _kv_weights(w_ukv[l]),
            row(g_lru_out[l]))
        ymla = _mla_attn(q, k, v, row(g_mla_out[l]))
        h = _mix_mem(ylru, ymla, h, w_out[l].astype(BF16), row(g_post_mix[l]), row(g_pre_mem[l]),
                     w_mq[l].astype(BF16), mk, mv, w_mo[l].astype(BF16), row(g_post_mem[l]))
        h = _ffn(h.reshape(bsz * s, d), row(g_pre_ffn[l]), w_gate[l].astype(BF16),
                 w_up[l].astype(BF16), w_down[l].astype(BF16), row(g_post_ffn[l])).reshape(bsz, s, d)
    return h
```

```python
import functools
import math

import jax
import jax.numpy as jnp
from jax import lax
from jax.experimental import pallas as pl
from jax.experimental.pallas import tpu as pltpu

F32 = jnp.float32
BF16 = jnp.bfloat16

D_MODEL = 1024
N_MEM = 256
MEM_HEADS = 4
MEM_HEAD_DIM = D_MODEL // MEM_HEADS
D_LRU = 512
LRU_BLOCKS = 8
LRU_BLOCK_DIM = D_LRU // LRU_BLOCKS
CONV_WIDTH = 4
LRU_C = 8.0
MLA_HEADS = 4
QK_NOPE_DIM = 128
QK_ROPE_DIM = 64
V_HEAD_DIM = 128
Q_LORA_RANK = 384
KV_LORA_RANK = 256
D_MLA_OUT = MLA_HEADS * V_HEAD_DIM
ROPE_THETA = 10000.0
D_FF = 2816
RMS_EPS = 1e-6
NEG_INF = -1e30

LANES = 128
SUBLANES = 8
HEAD_SLAB = 2 * LANES
BF16_ROWS = 16
VT_ROWS = V_HEAD_DIM + BF16_ROWS
Z_LRU_X = 0
Z_LRU_GATE = Z_LRU_X + D_LRU
Z_CQ = Z_LRU_GATE + D_LRU
Z_CKV = Z_CQ + Q_LORA_RANK
Z_KPE = Z_CKV + KV_LORA_RANK
Z_WIDTH = Z_KPE + LANES
LRU_GROUP = 256

TS_IN = 512
TM_MIX = 512
TM_FFN = 512
FF_CHUNKS = ((0, 1024), (1024, 2048), (2048, D_FF))
VMEM_LIMIT = 56 * 1024 * 1024


def _rms(x, g):
    ms = jnp.mean(x * x, axis=-1, keepdims=True)
    return x * lax.rsqrt(ms + RMS_EPS) * g


def _dot(a, b):
    return jnp.dot(a, b, preferred_element_type=F32)


def _dot_nt(a, b):
    return lax.dot_general(a, b, (((1,), (1,)), ((), ())), preferred_element_type=F32)


def _const_spec(shape):
    return pl.BlockSpec(shape, lambda *_: (0,) * len(shape))


def _mem_kv_kernel(mem_ref, g_ref, wk_ref, wv_ref, mk_ref, mv_ref):
    mn = _rms(mem_ref[...], g_ref[...]).astype(BF16)
    mk_ref[...] = _dot(mn, wk_ref[...]).astype(BF16)
    mv_ref[...] = _dot(mn, wv_ref[...]).astype(BF16)


def _mem_kv(mem, g, wk, wv):
    bsz = mem.shape[0]
    blk = pl.BlockSpec((None, N_MEM, D_MODEL), lambda b: (b, 0, 0))
    return pl.pallas_call(
        _mem_kv_kernel,
        out_shape=(jax.ShapeDtypeStruct((bsz, N_MEM, D_MODEL), BF16),) * 2,
        grid=(bsz,),
        in_specs=[blk, _const_spec((1, D_MODEL)), _const_spec((D_MODEL, D_MODEL)),
                  _const_spec((D_MODEL, D_MODEL))],
        out_specs=(blk, blk),
        compiler_params=pltpu.CompilerParams(dimension_semantics=("parallel",),
                                             vmem_limit_bytes=VMEM_LIMIT),
        name="mem_kv",
    )(mem, g, wk, wv)


def _shift_rows(x, prev, d):
    r = pltpu.roll(x, d, 0)
    row = lax.broadcasted_iota(jnp.int32, prev.shape, 0)
    head = jnp.where(row < d, pltpu.roll(prev, d, 0), r[0:SUBLANES])
    return jnp.concatenate([head, r[SUBLANES:]], axis=0)


def _linear_scan(a, u, h0):
    t = a.shape[0]
    row = lax.broadcasted_iota(jnp.int32, a.shape, 0) % SUBLANES
    for s in (1, 2, 4):
        a_s = pltpu.roll(a, s, 0)
        u_s = pltpu.roll(u, s, 0)
        m = row >= s
        u = jnp.where(m, a * u_s + u, u)
        a = jnp.where(m, a * a_s, a)
    hs = []
    h = h0
    for g in range(t // SUBLANES):
        sl = slice(g * SUBLANES, (g + 1) * SUBLANES)
        hg = a[sl] * h + u[sl]
        hs.append(hg)
        h = hg[SUBLANES - 1:SUBLANES]
    return jnp.concatenate(hs, axis=0)


def _rope(t, cos, sin):
    half = QK_ROPE_DIM // 2
    return t * cos + pltpu.roll(t, half, 1) * sin - pltpu.roll(t, LANES - half, 1) * sin


def _in_proj_kernel(x_ref, pos_ref, invf_ref, gpre_ref, win_ref, convw_ref, convb_ref, wg_ref,
                    ba_ref, bx_ref, lam_ref, gq_ref, wuq_ref, gkv_ref, wukv_ref, glru_ref,
                    qt_ref, k_ref, vt_ref, ylru_ref, xprev_ref, hprev_ref):
    ts = x_ref.shape[0]

    @pl.when(pl.program_id(1) == 0)
    def _():
        xprev_ref[...] = jnp.zeros_like(xprev_ref)
        hprev_ref[...] = jnp.zeros_like(hprev_ref)

    xn = _rms(x_ref[...], gpre_ref[...]).astype(BF16)
    z = _dot(xn, win_ref[...])

    xl = z[:, Z_LRU_X:Z_LRU_GATE]
    prev = xprev_ref[...]
    u = convb_ref[...] + convw_ref[CONV_WIDTH - 1:CONV_WIDTH, :] * xl
    for d in range(1, CONV_WIDTH):
        k = CONV_WIDTH - 1 - d
        u = u + convw_ref[k:k + 1, :] * _shift_rows(xl, prev, d)
    xprev_ref[...] = xl[ts - SUBLANES:ts]

    ub = u.astype(BF16)
    pre = [_dot(ub[:, j * LRU_GROUP:(j + 1) * LRU_GROUP], wg_ref[j])
           for j in range(D_LRU // LRU_GROUP)]
    r = jax.nn.sigmoid(jnp.concatenate([p[:, :LRU_GROUP] for p in pre], axis=-1) + ba_ref[...])
    gi = jax.nn.sigmoid(jnp.concatenate([p[:, LRU_GROUP:] for p in pre], axis=-1) + bx_ref[...])
    nlam = -lam_ref[...]
    softplus = jnp.maximum(nlam, 0.0) + jnp.log1p(jnp.exp(-jnp.abs(nlam)))
    log_a = -LRU_C * r * softplus
    a = jnp.exp(log_a)
    v_in = jnp.sqrt(-jnp.tanh(log_a) * (a * a + 1.0)) * (gi * u)
    h = _linear_scan(a, v_in, hprev_ref[SUBLANES - 1:SUBLANES, :])
    hprev_ref[...] = h[ts - SUBLANES:ts]
    y = h * jax.nn.gelu(z[:, Z_LRU_GATE:Z_CQ], approximate=True)
    ylru_ref[...] = _rms(y, glru_ref[...]).astype(BF16)

    scale = math.log2(math.e) / math.sqrt(QK_NOPE_DIM + QK_ROPE_DIM)
    q = _dot(_rms(z[:, Z_CQ:Z_CKV], gq_ref[...]).astype(BF16), wuq_ref[...]) * scale
    kv = _dot(_rms(z[:, Z_CKV:Z_KPE], gkv_ref[...]).astype(BF16), wukv_ref[...])
    ang = pos_ref[...].astype(F32) * invf_ref[...]
    cos = jnp.cos(ang)
    sin = jnp.sin(ang)
    kpe = _rope(z[:, Z_KPE:Z_WIDTH], cos, sin).astype(BF16)
    q_parts = []
    for hd in range(MLA_HEADS):
        c0 = hd * HEAD_SLAB
        q_parts += [q[:, c0:c0 + LANES], _rope(q[:, c0 + LANES:c0 + HEAD_SLAB], cos, sin)]
        k_ref[:, c0:c0 + LANES] = kv[:, hd * LANES:(hd + 1) * LANES].astype(BF16)
        k_ref[:, c0 + LANES:c0 + HEAD_SLAB] = kpe
    qt_ref[...] = jnp.concatenate(q_parts, axis=-1).T.astype(BF16)
    vt = kv[:, MLA_HEADS * QK_NOPE_DIM:].T.astype(BF16)
    for hd in range(MLA_HEADS):
        r0 = hd * VT_ROWS
        vt_ref[r0:r0 + V_HEAD_DIM, :] = vt[hd * V_HEAD_DIM:(hd + 1) * V_HEAD_DIM]
        vt_ref[r0 + V_HEAD_DIM:r0 + VT_ROWS, :] = jnp.ones((BF16_ROWS, ts), BF16)


def _in_proj(x, pos3, invf, gpre, win, convw, convb, wg, ba, bx, lam, gq, wuq, gkv, wukv, glru):
    bsz, s, _ = x.shape
    ts = TS_IN
    tok = lambda w: pl.BlockSpec((None, ts, w), lambda b, i: (b, i, 0))
    in_specs = [
        tok(D_MODEL), tok(1), _const_spec((1, LANES)), _const_spec((1, D_MODEL)),
        _const_spec((D_MODEL, Z_WIDTH)), _const_spec((CONV_WIDTH, D_LRU)), _const_spec((1, D_LRU)),
        _const_spec((D_LRU // LRU_GROUP, LRU_GROUP, 2 * LRU_GROUP)),
        _const_spec((1, D_LRU)), _const_spec((1, D_LRU)), _const_spec((1, D_LRU)),
        _const_spec((1, Q_LORA_RANK)), _const_spec((Q_LORA_RANK, MLA_HEADS * HEAD_SLAB)),
        _const_spec((1, KV_LORA_RANK)), _const_spec((KV_LORA_RANK, 2 * D_MLA_OUT)),
        _const_spec((1, D_LRU)),
    ]
    tile_t = lambda w: pl.BlockSpec((None, None, w, ts), lambda b, i: (b, i, 0, 0))
    out_shape = (
        jax.ShapeDtypeStruct((bsz, s // ts, MLA_HEADS * HEAD_SLAB, ts), BF16),
        jax.ShapeDtypeStruct((bsz, s, MLA_HEADS * HEAD_SLAB), BF16),
        jax.ShapeDtypeStruct((bsz, s // ts, MLA_HEADS * VT_ROWS, ts), BF16),
        jax.ShapeDtypeStruct((bsz, s, D_LRU), BF16),
    )
    out_specs = (tile_t(MLA_HEADS * HEAD_SLAB), tok(MLA_HEADS * HEAD_SLAB),
                 tile_t(MLA_HEADS * VT_ROWS), tok(D_LRU))
    return pl.pallas_call(
        _in_proj_kernel,
        out_shape=out_shape,
        grid=(bsz, s // ts),
        in_specs=in_specs,
        out_specs=out_specs,
        scratch_shapes=[pltpu.VMEM((SUBLANES, D_LRU), F32), pltpu.VMEM((SUBLANES, D_LRU), F32)],
        compiler_params=pltpu.CompilerParams(dimension_semantics=("parallel", "arbitrary"),
                                             vmem_limit_bytes=VMEM_LIMIT),
        name="in_proj",
    )(x, pos3, invf, gpre, win, convw, convb, wg, ba, bx, lam, gq, wuq, gkv, wukv, glru)


def _mla_attn_kernel(qt_ref, k_ref, vt_ref, g_ref, o_ref, m_ref, acc_ref):
    tq = qt_ref.shape[1]
    i = pl.program_id(1)
    m_ref[...] = jnp.full(m_ref.shape, NEG_INF, F32)
    acc_ref[...] = jnp.zeros(acc_ref.shape, F32)

    def kv_tile(j, masked):
        start = pl.multiple_of(j * tq, tq)

        def scores(hd):
            slab = slice(hd * HEAD_SLAB, (hd + 1) * HEAD_SLAB)
            return _dot(k_ref[pl.ds(start, tq), slab], qt_ref[slab, :])

        st_next = scores(0)
        for hd in range(MLA_HEADS):
            st = st_next
            if hd + 1 < MLA_HEADS:
                st_next = scores(hd + 1)
            vrows = slice(hd * VT_ROWS, (hd + 1) * VT_ROWS)
            if masked:
                kid = lax.broadcasted_iota(jnp.int32, (tq, tq), 0)
                qid = lax.broadcasted_iota(jnp.int32, (tq, tq), 1)
                st = jnp.where(qid >= kid, st, NEG_INF)
            m_prev = m_ref[hd]
            m_new = jnp.maximum(m_prev, jnp.max(st, axis=0, keepdims=True))
            alpha = jnp.exp2(m_prev - m_new)
            p = jnp.exp2(st - m_new).astype(BF16)
            m_ref[hd] = m_new
            acc_ref[vrows, :] = alpha * acc_ref[vrows, :] + _dot(vt_ref[j, vrows, :], p)

    def body(j, carry):
        kv_tile(j, masked=False)
        return carry

    lax.fori_loop(0, i, body, 0)
    kv_tile(i, masked=True)

    outs = []
    for hd in range(MLA_HEADS):
        r0 = hd * VT_ROWS
        inv_l = 1.0 / acc_ref[r0 + V_HEAD_DIM:r0 + V_HEAD_DIM + 1, :]
        outs.append(acc_ref[r0:r0 + V_HEAD_DIM, :] * inv_l)
    out = jnp.concatenate(outs, axis=0).T
    o_ref[...] = _rms(out, g_ref[...]).astype(BF16)


def _mla_attn(qt, k, vt, g):
    bsz, nq, _, tq = qt.shape
    s = k.shape[1]
    return pl.pallas_call(
        _mla_attn_kernel,
        out_shape=jax.ShapeDtypeStruct((bsz, s, D_MLA_OUT), BF16),
        grid=(bsz, nq),
        in_specs=[
            pl.BlockSpec((None, None, MLA_HEADS * HEAD_SLAB, tq), lambda b, i: (b, i, 0, 0)),
            pl.BlockSpec((None, s, MLA_HEADS * HEAD_SLAB), lambda b, i: (b, 0, 0)),
            pl.BlockSpec((None, nq, MLA_HEADS * VT_ROWS, tq), lambda b, i: (b, 0, 0, 0)),
            _const_spec((1, D_MLA_OUT)),
        ],
        out_specs=pl.BlockSpec((None, tq, D_MLA_OUT), lambda b, i: (b, i, 0)),
        scratch_shapes=[pltpu.VMEM((MLA_HEADS, 1, tq), F32),
                        pltpu.VMEM((MLA_HEADS * VT_ROWS, tq), F32)],
        compiler_params=pltpu.CompilerParams(dimension_semantics=("parallel", "arbitrary"),
                                             vmem_limit_bytes=VMEM_LIMIT),
        name="mla_attn",
    )(qt, k, vt, g)


def _mix_mem_kernel(ylru_ref, ymla_ref, x_ref, wout_ref, gpost_ref, gpre_ref, wmq_ref, mk_ref,
                    mv_ref, wmo_ref, gpm_ref, h_ref, o_scr):
    half = D_LRU
    yo = _dot(ylru_ref[...], wout_ref[0:half, :]) + _dot(ymla_ref[...], wout_ref[half:, :])
    h1 = x_ref[...] + _rms(yo, gpost_ref[...])

    xn = _rms(h1, gpre_ref[...]).astype(BF16)
    mq = (_dot(xn, wmq_ref[...]) * (1.0 / math.sqrt(MEM_HEAD_DIM))).astype(BF16)
    for hd in range(MEM_HEADS):
        sl = slice(hd * MEM_HEAD_DIM, (hd + 1) * MEM_HEAD_DIM)
        s = _dot_nt(mq[:, sl], mk_ref[:, sl])
        m = jnp.max(s, axis=-1, keepdims=True)
        p = jnp.exp(s - m)
        l = jnp.sum(p, axis=-1, keepdims=True)
        o_scr[:, sl] = (_dot(p.astype(BF16), mv_ref[:, sl]) / l).astype(BF16)
    h_ref[...] = h1 + _rms(_dot(o_scr[...], wmo_ref[...]), gpm_ref[...])


def _mix_mem(ylru, ymla, x, wout, gpost, gpre, wmq, mk, mv, wmo, gpm):
    bsz, s, _ = x.shape
    tm = TM_MIX
    tok = lambda w: pl.BlockSpec((None, tm, w), lambda b, i: (b, i, 0))
    memspec = pl.BlockSpec((None, N_MEM, D_MODEL), lambda b, i: (b, 0, 0))
    sq = _const_spec((D_MODEL, D_MODEL))
    vec = _const_spec((1, D_MODEL))
    return pl.pallas_call(
        _mix_mem_kernel,
        out_shape=jax.ShapeDtypeStruct((bsz, s, D_MODEL), F32),
        grid=(bsz, s // tm),
        in_specs=[tok(D_LRU), tok(D_MLA_OUT), tok(D_MODEL), sq, vec, vec, sq, memspec, memspec,
                  sq, vec],
        out_specs=tok(D_MODEL),
        scratch_shapes=[pltpu.VMEM((tm, D_MODEL), BF16)],
        compiler_params=pltpu.CompilerParams(dimension_semantics=("parallel", "parallel"),
                                             vmem_limit_bytes=VMEM_LIMIT),
        name="mix_mem",
    )(ylru, ymla, x, wout, gpost, gpre, wmq, mk, mv, wmo, gpm)


def _ffn_kernel(h_ref, gpre_ref, wg_ref, wu_ref, wd_ref, gpost_ref, o_ref):
    h = h_ref[...]
    xn = _rms(h, gpre_ref[...]).astype(BF16)
    f = None
    for lo, hi in FF_CHUNKS:
        act = (jax.nn.silu(_dot(xn, wg_ref[:, lo:hi])) * _dot(xn, wu_ref[:, lo:hi])).astype(BF16)
        part = _dot(act, wd_ref[lo:hi, :])
        f = part if f is None else f + part
    o_ref[...] = h + _rms(f, gpost_ref[...])


def _ffn(h, gpre, wg, wu, wd, gpost):
    m, _ = h.shape
    tm = TM_FFN
    single = pl.Buffered(1)
    tok = pl.BlockSpec((tm, D_MODEL), lambda i: (i, 0))
    vec = _const_spec((1, D_MODEL))
    return pl.pallas_call(
        _ffn_kernel,
        out_shape=jax.ShapeDtypeStruct((m, D_MODEL), F32),
        grid=(m // tm,),
        in_specs=[
            tok, vec,
            pl.BlockSpec((D_MODEL, D_FF), lambda i: (0, 0), pipeline_mode=single),
            pl.BlockSpec((D_MODEL, D_FF), lambda i: (0, 0), pipeline_mode=single),
            pl.BlockSpec((D_FF, D_MODEL), lambda i: (0, 0), pipeline_mode=single),
            vec,
        ],
        out_specs=tok,
        compiler_params=pltpu.CompilerParams(dimension_semantics=("parallel",),
                                             vmem_limit_bytes=VMEM_LIMIT),
        name="ffn",
    )(h, gpre, wg, wu, wd, gpost)


def _block_diag(w):
    eye = jnp.eye(LRU_BLOCKS, dtype=w.dtype)
    return jnp.einsum('hij,hg->higj', w, eye).reshape(D_LRU, D_LRU)


def _gate_weights(wa, wx):
    da, dx = _block_diag(wa), _block_diag(wx)
    groups = []
    for j in range(D_LRU // LRU_GROUP):
        sl = slice(j * LRU_GROUP, (j + 1) * LRU_GROUP)
        groups.append(jnp.concatenate([da[sl, sl], dx[sl, sl]], axis=1))
    return jnp.stack(groups).astype(BF16)


def _pad_in_proj(w_in):
    pad = jnp.zeros((D_MODEL, Z_WIDTH - w_in.shape[1]), w_in.dtype)
    return jnp.concatenate([w_in, pad], axis=1).astype(BF16)


def _q_weights(w_uq):
    dh = QK_NOPE_DIM + QK_ROPE_DIM
    w = w_uq.reshape(Q_LORA_RANK, MLA_HEADS, dh)
    pad = jnp.zeros((Q_LORA_RANK, MLA_HEADS, HEAD_SLAB - dh), w_uq.dtype)
    return jnp.concatenate([w, pad], axis=-1).reshape(Q_LORA_RANK, MLA_HEADS * HEAD_SLAB).astype(BF16)


def _kv_weights(w_ukv):
    w = w_ukv.reshape(KV_LORA_RANK, MLA_HEADS, QK_NOPE_DIM + V_HEAD_DIM)
    kn = w[:, :, :QK_NOPE_DIM].reshape(KV_LORA_RANK, MLA_HEADS * QK_NOPE_DIM)
    vv = w[:, :, QK_NOPE_DIM:].reshape(KV_LORA_RANK, MLA_HEADS * V_HEAD_DIM)
    return jnp.concatenate([kn, vv], axis=1).astype(BF16)


def _rope_freq_tile():
    half = QK_ROPE_DIM // 2
    inv_freq = ROPE_THETA ** (-jnp.arange(0, QK_ROPE_DIM, 2, dtype=F32) / QK_ROPE_DIM)
    return jnp.concatenate([inv_freq, inv_freq, jnp.zeros((LANES - 2 * half,), F32)]).reshape(1, LANES)


def kernel(x, mem, positions, g_pre_mix, w_in, conv_w, conv_b, lru_wa, lru_ba, lru_wx, lru_bx, lru_lambda, g_q_lat, w_uq, g_kv_lat, w_ukv, g_lru_out, g_mla_out, w_out, g_post_mix, g_pre_mem, g_mem_kv, w_mq, w_mk, w_mv, w_mo, g_post_mem, g_pre_ffn, w_gate, w_up, w_down, g_post_ffn):
    bsz, s, d = x.shape
    depth = w_in.shape[0]
    assert d == D_MODEL and s % TS_IN == 0 and s % TM_MIX == 0
    assert (bsz * s) % TM_FFN == 0 and mem.shape == (bsz, N_MEM, D_MODEL)
    pos3 = positions.reshape(bsz, s, 1)
    invf = _rope_freq_tile()
    row = lambda a: a.reshape(1, -1)

    h = x
    for l in range(depth):
        mk, mv = _mem_kv(mem, row(g_mem_kv[l]), w_mk[l].astype(BF16), w_mv[l].astype(BF16))
        qt, k, vt, ylru = _in_proj(
            h, pos3, invf, row(g_pre_mix[l]), _pad_in_proj(w_in[l]), conv_w[l], row(conv_b[l]),
            _gate_weights(lru_wa[l], lru_wx[l]), row(lru_ba[l]), row(lru_bx[l]), row(lru_lambda[l]),
            row(g_q_lat[l]), _q_weights(w_uq[l]), row(g_kv_lat[l]), _kv_weights(w_ukv[l]),
            row(g_lru_out[l]))
        ymla = _mla_attn(qt, k, vt, row(g_mla_out[l]))
        h = _mix_mem(ylru, ymla, h, w_out[l].astype(BF16), row(g_post_mix[l]), row(g_pre_mem[l]),
                     w_mq[l].astype(BF16), mk, mv, w_mo[l].astype(BF16), row(g_post_mem[l]))
        h = _ffn(h.reshape(bsz * s, d), row(g_pre_ffn[l]), w_gate[l].astype(BF16),
                 w_up[l].astype(BF16), w_down[l].astype(BF16), row(g_post_ffn[l])).reshape(bsz, s, d)
    return h
```

```python
import functools
import math

import jax
import jax.numpy as jnp
from jax import lax
from jax.experimental import pallas as pl
from jax.experimental.pallas import tpu as pltpu

F32 = jnp.float32
BF16 = jnp.bfloat16

D_MODEL = 1024
N_MEM = 256
MEM_HEADS = 4
MEM_HEAD_DIM = D_MODEL // MEM_HEADS
D_LRU = 512
LRU_BLOCKS = 8
LRU_BLOCK_DIM = D_LRU // LRU_BLOCKS
CONV_WIDTH = 4
LRU_C = 8.0
MLA_HEADS = 4
QK_NOPE_DIM = 128
QK_ROPE_DIM = 64
V_HEAD_DIM = 128
Q_LORA_RANK = 384
KV_LORA_RANK = 256
D_MLA_OUT = MLA_HEADS * V_HEAD_DIM
ROPE_THETA = 10000.0
D_FF = 2816
RMS_EPS = 1e-6
NEG_INF = -1e30

LANES = 128
SUBLANES = 8
HEAD_SLAB = 2 * LANES
BF16_ROWS = 16
VT_ROWS = V_HEAD_DIM + BF16_ROWS
Z_LRU_X = 0
Z_LRU_GATE = Z_LRU_X + D_LRU
Z_CQ = Z_LRU_GATE + D_LRU
Z_CKV = Z_CQ + Q_LORA_RANK
Z_KPE = Z_CKV + KV_LORA_RANK
Z_WIDTH = Z_KPE + LANES
LRU_GROUP = 256

TS_IN = 512
TM_MIX = 512
TM_FFN = 512
FF_CHUNKS = ((0, 1024), (1024, 2048), (2048, D_FF))
VMEM_LIMIT = 56 * 1024 * 1024


def _rms(x, g):
    ms = jnp.mean(x * x, axis=-1, keepdims=True)
    return x * lax.rsqrt(ms + RMS_EPS) * g


def _dot(a, b):
    return jnp.dot(a, b, preferred_element_type=F32)


def _dot_nt(a, b):
    return lax.dot_general(a, b, (((1,), (1,)), ((), ())), preferred_element_type=F32)


def _const_spec(shape):
    return pl.BlockSpec(shape, lambda *_: (0,) * len(shape))


def _mem_kv_kernel(mem_ref, g_ref, wk_ref, wv_ref, mk_ref, mv_ref):
    mn = _rms(mem_ref[...], g_ref[...]).astype(BF16)
    mk_ref[...] = _dot(mn, wk_ref[...]).astype(BF16)
    mv_ref[...] = _dot(mn, wv_ref[...]).astype(BF16)


def _mem_kv(mem, g, wk, wv):
    bsz = mem.shape[0]
    blk = pl.BlockSpec((None, N_MEM, D_MODEL), lambda b: (b, 0, 0))
    return pl.pallas_call(
        _mem_kv_kernel,
        out_shape=(jax.ShapeDtypeStruct((bsz, N_MEM, D_MODEL), BF16),) * 2,
        grid=(bsz,),
        in_specs=[blk, _const_spec((1, D_MODEL)), _const_spec((D_MODEL, D_MODEL)),
                  _const_spec((D_MODEL, D_MODEL))],
        out_specs=(blk, blk),
        compiler_params=pltpu.CompilerParams(dimension_semantics=("parallel",),
                                             vmem_limit_bytes=VMEM_LIMIT),
        name="mem_kv",
    )(mem, g, wk, wv)


def _shift_rows(x, prev, d):
    r = pltpu.roll(x, d, 0)
    row = lax.broadcasted_iota(jnp.int32, prev.shape, 0)
    head = jnp.where(row < d, pltpu.roll(prev, d, 0), r[0:SUBLANES])
    return jnp.concatenate([head, r[SUBLANES:]], axis=0)


def _linear_scan(a, u, h0):
    t, c = a.shape
    groups = t // SUBLANES
    a = a.reshape(groups, SUBLANES, c)
    u = u.reshape(groups, SUBLANES, c)
    row = lax.broadcasted_iota(jnp.int32, a.shape, 1)
    for s in (1, 2, 4):
        a_s = pltpu.roll(a, s, 1)
        u_s = pltpu.roll(u, s, 1)
        m = row >= s
        u = jnp.where(m, a * u_s + u, u)
        a = jnp.where(m, a * a_s, a)
    hs = []
    h = h0
    for g in range(groups):
        hg = a[g] * h + u[g]
        hs.append(hg)
        h = hg[SUBLANES - 1:SUBLANES]
    return jnp.concatenate(hs, axis=0)


def _lane_roll(x, shift):
    shift %= LANES
    return x if shift == 0 else pltpu.roll(x, shift, 1)


def _rope_tables(posf, invf4):
    ts = posf.shape[0]
    half = QK_ROPE_DIM // 2
    nblk = LANES // half
    rows = ts // nblk
    lane = lax.broadcasted_iota(jnp.int32, (rows, LANES), 1)
    pos_c = posf[(nblk - 1) * rows:]
    for j in range(nblk - 2, -1, -1):
        pos_c = jnp.where(lane < (j + 1) * half, posf[j * rows:(j + 1) * rows], pos_c)
    ang = pos_c * invf4
    tables = []
    for dense in (jnp.cos(ang), jnp.sin(ang)):
        parts = []
        for j in range(nblk):
            lo = _lane_roll(dense, -j * half)
            hi = _lane_roll(dense, half - j * half)
            parts.append(jnp.where(lane < half, lo, jnp.where(lane < 2 * half, hi, 0.0)))
        tables.append(jnp.concatenate(parts, axis=0))
    return tables


def _rope(t, cos, sin):
    half = QK_ROPE_DIM // 2
    return t * cos + pltpu.roll(t, half, 1) * sin - pltpu.roll(t, LANES - half, 1) * sin


def _in_proj_kernel(x_ref, pos_ref, invf_ref, gpre_ref, win_ref, convw_ref, convb_ref, wg_ref,
                    ba_ref, bx_ref, lam_ref, gq_ref, wuq_ref, gkv_ref, wukv_ref, glru_ref,
                    qt_ref, k_ref, vt_ref, ylru_ref, xprev_ref, hprev_ref):
    ts = x_ref.shape[0]

    @pl.when(pl.program_id(1) == 0)
    def _():
        xprev_ref[...] = jnp.zeros_like(xprev_ref)
        hprev_ref[...] = jnp.zeros_like(hprev_ref)

    xn = _rms(x_ref[...], gpre_ref[...]).astype(BF16)
    z = _dot(xn, win_ref[...])

    xl = z[:, Z_LRU_X:Z_LRU_GATE]
    prev = xprev_ref[...]
    u = convb_ref[...] + convw_ref[CONV_WIDTH - 1:CONV_WIDTH, :] * xl
    for d in range(1, CONV_WIDTH):
        k = CONV_WIDTH - 1 - d
        u = u + convw_ref[k:k + 1, :] * _shift_rows(xl, prev, d)
    xprev_ref[...] = xl[ts - SUBLANES:ts]

    ub = u.astype(BF16)
    pre = [_dot(ub[:, j * LRU_GROUP:(j + 1) * LRU_GROUP], wg_ref[j])
           for j in range(D_LRU // LRU_GROUP)]
    r = jax.nn.sigmoid(jnp.concatenate([p[:, :LRU_GROUP] for p in pre], axis=-1) + ba_ref[...])
    gi = jax.nn.sigmoid(jnp.concatenate([p[:, LRU_GROUP:] for p in pre], axis=-1) + bx_ref[...])
    nlam = -lam_ref[...]
    softplus = jnp.maximum(nlam, 0.0) + jnp.log1p(jnp.exp(-jnp.abs(nlam)))
    log_a = -LRU_C * r * softplus
    a = jnp.exp(log_a)
    one_m_a2 = 1.0 - a * a
    root = jnp.where(one_m_a2 > 0.0, one_m_a2 * lax.rsqrt(one_m_a2), 0.0)
    v_in = root * (gi * u)
    h = _linear_scan(a, v_in, hprev_ref[SUBLANES - 1:SUBLANES, :])
    hprev_ref[...] = h[ts - SUBLANES:ts]
    y = h * jax.nn.gelu(z[:, Z_LRU_GATE:Z_CQ], approximate=True)
    ylru_ref[...] = _rms(y, glru_ref[...]).astype(BF16)

    scale = math.log2(math.e) / math.sqrt(QK_NOPE_DIM + QK_ROPE_DIM)
    q = _dot(_rms(z[:, Z_CQ:Z_CKV], gq_ref[...]).astype(BF16), wuq_ref[...]) * scale
    kv = _dot(_rms(z[:, Z_CKV:Z_KPE], gkv_ref[...]).astype(BF16), wukv_ref[...])
    cos, sin = _rope_tables(pos_ref[...].astype(F32), invf_ref[...])
    kpe = _rope(z[:, Z_KPE:Z_WIDTH], cos, sin).astype(BF16)
    q_parts = []
    for hd in range(MLA_HEADS):
        c0 = hd * HEAD_SLAB
        q_parts += [q[:, c0:c0 + LANES], _rope(q[:, c0 + LANES:c0 + HEAD_SLAB], cos, sin)]
        k_ref[:, c0:c0 + LANES] = kv[:, hd * LANES:(hd + 1) * LANES].astype(BF16)
        k_ref[:, c0 + LANES:c0 + HEAD_SLAB] = kpe
    qt_ref[...] = jnp.concatenate(q_parts, axis=-1).T.astype(BF16)
    vt = kv[:, MLA_HEADS * QK_NOPE_DIM:].T.astype(BF16)
    for hd in range(MLA_HEADS):
        r0 = hd * VT_ROWS
        vt_ref[r0:r0 + V_HEAD_DIM, :] = vt[hd * V_HEAD_DIM:(hd + 1) * V_HEAD_DIM]
        vt_ref[r0 + V_HEAD_DIM:r0 + VT_ROWS, :] = jnp.ones((BF16_ROWS, ts), BF16)


def _in_proj(x, pos3, invf, gpre, win, convw, convb, wg, ba, bx, lam, gq, wuq, gkv, wukv, glru):
    bsz, s, _ = x.shape
    ts = TS_IN
    tok = lambda w: pl.BlockSpec((None, ts, w), lambda b, i: (b, i, 0))
    in_specs = [
        tok(D_MODEL), tok(1), _const_spec((1, LANES)), _const_spec((1, D_MODEL)),
        _const_spec((D_MODEL, Z_WIDTH)), _const_spec((CONV_WIDTH, D_LRU)), _const_spec((1, D_LRU)),
        _const_spec((D_LRU // LRU_GROUP, LRU_GROUP, 2 * LRU_GROUP)),
        _const_spec((1, D_LRU)), _const_spec((1, D_LRU)), _const_spec((1, D_LRU)),
        _const_spec((1, Q_LORA_RANK)), _const_spec((Q_LORA_RANK, MLA_HEADS * HEAD_SLAB)),
        _const_spec((1, KV_LORA_RANK)), _const_spec((KV_LORA_RANK, 2 * D_MLA_OUT)),
        _const_spec((1, D_LRU)),
    ]
    tile_t = lambda w: pl.BlockSpec((None, None, w, ts), lambda b, i: (b, i, 0, 0))
    out_shape = (
        jax.ShapeDtypeStruct((bsz, s // ts, MLA_HEADS * HEAD_SLAB, ts), BF16),
        jax.ShapeDtypeStruct((bsz, s, MLA_HEADS * HEAD_SLAB), BF16),
        jax.ShapeDtypeStruct((bsz, s // ts, MLA_HEADS * VT_ROWS, ts), BF16),
        jax.ShapeDtypeStruct((bsz, s, D_LRU), BF16),
    )
    out_specs = (tile_t(MLA_HEADS * HEAD_SLAB), tok(MLA_HEADS * HEAD_SLAB),
                 tile_t(MLA_HEADS * VT_ROWS), tok(D_LRU))
    return pl.pallas_call(
        _in_proj_kernel,
        out_shape=out_shape,
        grid=(bsz, s // ts),
        in_specs=in_specs,
        out_specs=out_specs,
        scratch_shapes=[pltpu.VMEM((SUBLANES, D_LRU), F32), pltpu.VMEM((SUBLANES, D_LRU), F32)],
        compiler_params=pltpu.CompilerParams(dimension_semantics=("parallel", "arbitrary"),
                                             vmem_limit_bytes=VMEM_LIMIT),
        name="in_proj",
    )(x, pos3, invf, gpre, win, convw, convb, wg, ba, bx, lam, gq, wuq, gkv, wukv, glru)


def _mla_attn_kernel(qt_ref, k_ref, vt_ref, g_ref, o_ref, m_ref, acc_ref):
    tq = qt_ref.shape[1]
    i = pl.program_id(1)
    m_ref[...] = jnp.full(m_ref.shape, NEG_INF, F32)
    acc_ref[...] = jnp.zeros(acc_ref.shape, F32)

    def kv_tile(j, masked):
        start = pl.multiple_of(j * tq, tq)

        def scores(hd):
            slab = slice(hd * HEAD_SLAB, (hd + 1) * HEAD_SLAB)
            return _dot(k_ref[pl.ds(start, tq), slab], qt_ref[slab, :])

        st_next = scores(0)
        for hd in range(MLA_HEADS):
            st = st_next
            if hd + 1 < MLA_HEADS:
                st_next = scores(hd + 1)
            vrows = slice(hd * VT_ROWS, (hd + 1) * VT_ROWS)
            if masked:
                kid = lax.broadcasted_iota(jnp.int32, (tq, tq), 0)
                qid = lax.broadcasted_iota(jnp.int32, (tq, tq), 1)
                st = jnp.where(qid >= kid, st, NEG_INF)
            m_prev = m_ref[hd]
            m_new = jnp.maximum(m_prev, jnp.max(st, axis=0, keepdims=True))
            alpha = jnp.exp2(m_prev - m_new)
            p = jnp.exp2(st - m_new).astype(BF16)
            m_ref[hd] = m_new
            acc_ref[vrows, :] = alpha * acc_ref[vrows, :] + _dot(vt_ref[j, vrows, :], p)

    def body(j, carry):
        kv_tile(j, masked=False)
        return carry

    lax.fori_loop(0, i, body, 0)
    kv_tile(i, masked=True)

    outs = []
    for hd in range(MLA_HEADS):
        r0 = hd * VT_ROWS
        inv_l = 1.0 / acc_ref[r0 + V_HEAD_DIM:r0 + V_HEAD_DIM + 1, :]
        outs.append(acc_ref[r0:r0 + V_HEAD_DIM, :] * inv_l)
    out = jnp.concatenate(outs, axis=0).T
    o_ref[...] = _rms(out, g_ref[...]).astype(BF16)


def _mla_attn(qt, k, vt, g):
    bsz, nq, _, tq = qt.shape
    s = k.shape[1]
    return pl.pallas_call(
        _mla_attn_kernel,
        out_shape=jax.ShapeDtypeStruct((bsz, s, D_MLA_OUT), BF16),
        grid=(bsz, nq),
        in_specs=[
            pl.BlockSpec((None, None, MLA_HEADS * HEAD_SLAB, tq), lambda b, i: (b, i, 0, 0)),
            pl.BlockSpec((None, s, MLA_HEADS * HEAD_SLAB), lambda b, i: (b, 0, 0)),
            pl.BlockSpec((None, nq, MLA_HEADS * VT_ROWS, tq), lambda b, i: (b, 0, 0, 0)),
            _const_spec((1, D_MLA_OUT)),
        ],
        out_specs=pl.BlockSpec((None, tq, D_MLA_OUT), lambda b, i: (b, i, 0)),
        scratch_shapes=[pltpu.VMEM((MLA_HEADS, 1, tq), F32),
                        pltpu.VMEM((MLA_HEADS * VT_ROWS, tq), F32)],
        compiler_params=pltpu.CompilerParams(dimension_semantics=("parallel", "arbitrary"),
                                             vmem_limit_bytes=VMEM_LIMIT),
        name="mla_attn",
    )(qt, k, vt, g)


def _mix_mem_kernel(ylru_ref, ymla_ref, x_ref, wout_ref, gpost_ref, gpre_ref, wmq_ref, mk_ref,
                    mv_ref, wmo_ref, gpm_ref, h_ref, o_scr):
    half = D_LRU
    yo = _dot(ylru_ref[...], wout_ref[0:half, :]) + _dot(ymla_ref[...], wout_ref[half:, :])
    h1 = x_ref[...] + _rms(yo, gpost_ref[...])

    xn = _rms(h1, gpre_ref[...]).astype(BF16)
    mq = (_dot(xn, wmq_ref[...]) * (1.0 / math.sqrt(MEM_HEAD_DIM))).astype(BF16)
    for hd in range(MEM_HEADS):
        sl = slice(hd * MEM_HEAD_DIM, (hd + 1) * MEM_HEAD_DIM)
        s = _dot_nt(mq[:, sl], mk_ref[:, sl])
        m = jnp.max(s, axis=-1, keepdims=True)
        p = jnp.exp(s - m)
        l = jnp.sum(p, axis=-1, keepdims=True)
        o_scr[:, sl] = (_dot(p.astype(BF16), mv_ref[:, sl]) / l).astype(BF16)
    h_ref[...] = h1 + _rms(_dot(o_scr[...], wmo_ref[...]), gpm_ref[...])


def _mix_mem(ylru, ymla, x, wout, gpost, gpre, wmq, mk, mv, wmo, gpm):
    bsz, s, _ = x.shape
    tm = TM_MIX
    tok = lambda w: pl.BlockSpec((None, tm, w), lambda b, i: (b, i, 0))
    memspec = pl.BlockSpec((None, N_MEM, D_MODEL), lambda b, i: (b, 0, 0))
    sq = _const_spec((D_MODEL, D_MODEL))
    vec = _const_spec((1, D_MODEL))
    return pl.pallas_call(
        _mix_mem_kernel,
        out_shape=jax.ShapeDtypeStruct((bsz, s, D_MODEL), F32),
        grid=(bsz, s // tm),
        in_specs=[tok(D_LRU), tok(D_MLA_OUT), tok(D_MODEL), sq, vec, vec, sq, memspec, memspec,
                  sq, vec],
        out_specs=tok(D_MODEL),
        scratch_shapes=[pltpu.VMEM((tm, D_MODEL), BF16)],
        compiler_params=pltpu.CompilerParams(dimension_semantics=("parallel", "parallel"),
                                             vmem_limit_bytes=VMEM_LIMIT),
        name="mix_mem",
    )(ylru, ymla, x, wout, gpost, gpre, wmq, mk, mv, wmo, gpm)


def _ffn_kernel(h_ref, gpre_ref, wg_ref, wu_ref, wd_ref, gpost_ref, o_ref):
    h = h_ref[...]
    xn = _rms(h, gpre_ref[...]).astype(BF16)
    f = None
    for lo, hi in FF_CHUNKS:
        act = (jax.nn.silu(_dot(xn, wg_ref[:, lo:hi])) * _dot(xn, wu_ref[:, lo:hi])).astype(BF16)
        part = _dot(act, wd_ref[lo:hi, :])
        f = part if f is None else f + part
    o_ref[...] = h + _rms(f, gpost_ref[...])


def _ffn(h, gpre, wg, wu, wd, gpost):
    m, _ = h.shape
    tm = TM_FFN
    single = pl.Buffered(1)
    tok = pl.BlockSpec((tm, D_MODEL), lambda i: (i, 0))
    vec = _const_spec((1, D_MODEL))
    return pl.pallas_call(
        _ffn_kernel,
        out_shape=jax.ShapeDtypeStruct((m, D_MODEL), F32),
        grid=(m // tm,),
        in_specs=[
            tok, vec,
            pl.BlockSpec((D_MODEL, D_FF), lambda i: (0, 0), pipeline_mode=single),
            pl.BlockSpec((D_MODEL, D_FF), lambda i: (0, 0), pipeline_mode=single),
            pl.BlockSpec((D_FF, D_MODEL), lambda i: (0, 0), pipeline_mode=single),
            vec,
        ],
        out_specs=tok,
        compiler_params=pltpu.CompilerParams(dimension_semantics=("parallel",),
                                             vmem_limit_bytes=VMEM_LIMIT),
        name="ffn",
    )(h, gpre, wg, wu, wd, gpost)


def _block_diag(w):
    eye = jnp.eye(LRU_BLOCKS, dtype=w.dtype)
    return jnp.einsum('hij,hg->higj', w, eye).reshape(D_LRU, D_LRU)


def _gate_weights(wa, wx):
    da, dx = _block_diag(wa), _block_diag(wx)
    groups = []
    for j in range(D_LRU // LRU_GROUP):
        sl = slice(j * LRU_GROUP, (j + 1) * LRU_GROUP)
        groups.append(jnp.concatenate([da[sl, sl], dx[sl, sl]], axis=1))
    return jnp.stack(groups).astype(BF16)


def _pad_in_proj(w_in):
    pad = jnp.zeros((D_MODEL, Z_WIDTH - w_in.shape[1]), w_in.dtype)
    return jnp.concatenate([w_in, pad], axis=1).astype(BF16)


def _q_weights(w_uq):
    dh = QK_NOPE_DIM + QK_ROPE_DIM
    w = w_uq.reshape(Q_LORA_RANK, MLA_HEADS, dh)
    pad = jnp.zeros((Q_LORA_RANK, MLA_HEADS, HEAD_SLAB - dh), w_uq.dtype)
    return jnp.concatenate([w, pad], axis=-1).reshape(Q_LORA_RANK, MLA_HEADS * HEAD_SLAB).astype(BF16)


def _kv_weights(w_ukv):
    w = w_ukv.reshape(KV_LORA_RANK, MLA_HEADS, QK_NOPE_DIM + V_HEAD_DIM)
    kn = w[:, :, :QK_NOPE_DIM].reshape(KV_LORA_RANK, MLA_HEADS * QK_NOPE_DIM)
    vv = w[:, :, QK_NOPE_DIM:].reshape(KV_LORA_RANK, MLA_HEADS * V_HEAD_DIM)
    return jnp.concatenate([kn, vv], axis=1).astype(BF16)


def _rope_freq_tile():
    half = QK_ROPE_DIM // 2
    inv_freq = ROPE_THETA ** (-jnp.arange(0, QK_ROPE_DIM, 2, dtype=F32) / QK_ROPE_DIM)
    return jnp.tile(inv_freq, LANES // half).reshape(1, LANES)


def kernel(x, mem, positions, g_pre_mix, w_in, conv_w, conv_b, lru_wa, lru_ba, lru_wx, lru_bx, lru_lambda, g_q_lat, w_uq, g_kv_lat, w_ukv, g_lru_out, g_mla_out, w_out, g_post_mix, g_pre_mem, g_mem_kv, w_mq, w_mk, w_mv, w_mo, g_post_mem, g_pre_ffn, w_gate, w_up, w_down, g_post_ffn):
    bsz, s, d = x.shape
    depth = w_in.shape[0]
    assert d == D_MODEL and s % TS_IN == 0 and s % TM_MIX == 0
    assert (bsz * s) % TM_FFN == 0 and mem.shape == (bsz, N_MEM, D_MODEL)
    pos3 = positions.reshape(bsz, s, 1)
    invf = _rope_freq_tile()
    row = lambda a: a.reshape(1, -1)

    h = x
    for l in range(depth):
        mk, mv = _mem_kv(mem, row(g_mem_kv[l]), w_mk[l].astype(BF16), w_mv[l].astype(BF16))
        qt, k, vt, ylru = _in_proj(
            h, pos3, invf, row(g_pre_mix[l]), _pad_in_proj(w_in[l]), conv_w[l], row(conv_b[l]),
            _gate_weights(lru_wa[l], lru_wx[l]), row(lru_ba[l]), row(lru_bx[l]), row(lru_lambda[l]),
            row(g_q_lat[l]), _q_weights(w_uq[l]), row(g_kv_lat[l]), _kv_weights(w_ukv[l]),
            row(g_lru_out[l]))
        ymla = _mla_attn(qt, k, vt, row(g_mla_out[l]))
        h = _mix_mem(ylru, ymla, h, w_out[l].astype(BF16), row(g_post_mix[l]), row(g_pre_mem[l]),
                     w_mq[l].astype(BF16), mk, mv, w_mo[l].astype(BF16), row(g_post_mem[l]))
        h = _ffn(h.reshape(bsz * s, d), row(g_pre_ffn[l]), w_gate[l].astype(BF16),
                 w_up[l].astype(BF16), w_down[l].astype(BF16), row(g_post_ffn[l])).reshape(bsz, s, d)
    return h
```

```python
import math

import jax
import jax.numpy as jnp
from jax import lax
from jax.experimental import pallas as pl
from jax.experimental.pallas import tpu as pltpu

F32 = jnp.float32
BF16 = jnp.bfloat16

D_MODEL = 1024
N_MEM = 256
MEM_HEADS = 4
MEM_HEAD_DIM = D_MODEL // MEM_HEADS
D_LRU = 512
LRU_BLOCKS = 8
LRU_BLOCK_DIM = D_LRU // LRU_BLOCKS
CONV_WIDTH = 4
LRU_C = 8.0
MLA_HEADS = 4
QK_NOPE_DIM = 128
QK_ROPE_DIM = 64
V_HEAD_DIM = 128
Q_LORA_RANK = 384
KV_LORA_RANK = 256
D_MLA_OUT = MLA_HEADS * V_HEAD_DIM
ROPE_THETA = 10000.0
D_FF = 2816
RMS_EPS = 1e-6
NEG_INF = -1e30

LANES = 128
SUBLANES = 8
HEAD_SLAB = 2 * LANES
BF16_ROWS = 16
VT_ROWS = V_HEAD_DIM + BF16_ROWS
Z_LRU_X = 0
Z_LRU_GATE = Z_LRU_X + D_LRU
Z_CQ = Z_LRU_GATE + D_LRU
Z_CKV = Z_CQ + Q_LORA_RANK
Z_KPE = Z_CKV + KV_LORA_RANK
Z_WIDTH = Z_KPE + LANES
MXU_TILE = 256
LRU_GROUP = MXU_TILE

TS_IN = 512
TM_MIX = 512
TM_FFN = 512
FF_CHUNKS = ((0, 1024), (1024, 2048), (2048, D_FF))
VMEM_LIMIT = 56 * 1024 * 1024


def _rms(x, g):
    ms = jnp.mean(x * x, axis=-1, keepdims=True)
    return x * lax.rsqrt(ms + RMS_EPS) * g


def _rms_plain(x):
    ms = jnp.mean(x * x, axis=-1, keepdims=True)
    return x * lax.rsqrt(ms + RMS_EPS)


def _gelu_tanh(x):
    k1 = -2.0 * math.sqrt(2.0 / math.pi) * math.log2(math.e)
    k2 = k1 * 0.044715
    return x / (1.0 + jnp.exp2(x * (k1 + k2 * (x * x))))


def _dot(a, b):
    return jnp.dot(a, b, preferred_element_type=F32)


def _dot_nt(a, b):
    return lax.dot_general(a, b, (((1,), (1,)), ((), ())), preferred_element_type=F32)


def _const_spec(shape):
    return pl.BlockSpec(shape, lambda *_: (0,) * len(shape))


def _mem_kv_kernel(mem_ref, g_ref, wk_ref, wv_ref, mk_ref, mv_ref):
    mn = _rms(mem_ref[...], g_ref[...]).astype(BF16)
    mk_ref[...] = _dot(mn, wk_ref[...]).astype(BF16)
    mv_ref[...] = _dot(mn, wv_ref[...]).astype(BF16)


def _mem_kv(mem, g, wk, wv):
    bsz = mem.shape[0]
    blk = pl.BlockSpec((None, N_MEM, D_MODEL), lambda b: (b, 0, 0))
    return pl.pallas_call(
        _mem_kv_kernel,
        out_shape=(jax.ShapeDtypeStruct((bsz, N_MEM, D_MODEL), BF16),) * 2,
        grid=(bsz,),
        in_specs=[blk, _const_spec((1, D_MODEL)), _const_spec((D_MODEL, D_MODEL)),
                  _const_spec((D_MODEL, D_MODEL))],
        out_specs=(blk, blk),
        compiler_params=pltpu.CompilerParams(dimension_semantics=("parallel",),
                                             vmem_limit_bytes=VMEM_LIMIT),
        name="mem_kv",
    )(mem, g, wk, wv)


def _shift_rows(x, prev, d):
    r = pltpu.roll(x, d, 0)
    row = lax.broadcasted_iota(jnp.int32, prev.shape, 0)
    head = jnp.where(row < d, pltpu.roll(prev, d, 0), r[0:SUBLANES])
    return jnp.concatenate([head, r[SUBLANES:]], axis=0)


def _linear_scan(a, u, h0):
    t, c = a.shape
    groups = t // SUBLANES
    a = a.reshape(groups, SUBLANES, c)
    u = u.reshape(groups, SUBLANES, c)
    row = lax.broadcasted_iota(jnp.int32, a.shape, 1)
    for s in (1, 2, 4):
        a_s = pltpu.roll(a, s, 1)
        u_s = pltpu.roll(u, s, 1)
        m = row >= s
        u = jnp.where(m, a * u_s + u, u)
        a = jnp.where(m, a * a_s, a)
    hs = []
    h = h0
    for g in range(groups):
        hg = a[g] * h + u[g]
        hs.append(hg)
        h = hg[SUBLANES - 1:SUBLANES]
    return jnp.concatenate(hs, axis=0)


def _rope(t, cos, sin):
    half = QK_ROPE_DIM // 2
    return t * cos + pltpu.roll(t, half, 1) * sin - pltpu.roll(t, LANES - half, 1) * sin


def _in_proj_kernel(x_ref, pos_ref, invf_ref, win_ref, convw_ref, convb_ref, wg_ref,
                    ba_ref, bx_ref, lam_ref, wuqt_ref, wkn_ref, wvt_ref, glru_ref,
                    qt_ref, k_ref, vt_ref, ylru_ref, xprev_ref, hprev_ref):
    ts = x_ref.shape[0]

    @pl.when(pl.program_id(1) == 0)
    def _():
        xprev_ref[...] = jnp.zeros_like(xprev_ref)
        hprev_ref[...] = jnp.zeros_like(hprev_ref)

    xn = _rms_plain(x_ref[...]).astype(BF16)
    z_tiles = {}

    def z(lo, hi):
        parts = []
        for j in range(lo // MXU_TILE, -(-hi // MXU_TILE)):
            if j not in z_tiles:
                z_tiles[j] = _dot(xn, win_ref[:, j * MXU_TILE:(j + 1) * MXU_TILE])
            a, b = max(lo, j * MXU_TILE), min(hi, (j + 1) * MXU_TILE)
            parts.append(z_tiles[j][:, a - j * MXU_TILE:b - j * MXU_TILE])
        return parts[0] if len(parts) == 1 else jnp.concatenate(parts, axis=-1)

    xl = z(Z_LRU_X, Z_LRU_GATE)
    prev = xprev_ref[...]
    u = convb_ref[...] + convw_ref[CONV_WIDTH - 1:CONV_WIDTH, :] * xl
    for d in range(1, CONV_WIDTH):
        k = CONV_WIDTH - 1 - d
        u = u + convw_ref[k:k + 1, :] * _shift_rows(xl, prev, d)
    xprev_ref[...] = xl[ts - SUBLANES:ts]

    ub = u.astype(BF16)
    pre = [_dot(ub[:, j * LRU_GROUP:(j + 1) * LRU_GROUP], wg_ref[j])
           for j in range(D_LRU // LRU_GROUP)]

    z_gate = z(Z_LRU_GATE, Z_CQ)
    z_cq = z(Z_CQ, Z_CKV)
    z_ckv = z(Z_CKV, Z_KPE)
    z_kpe = z(Z_KPE, Z_WIDTH)
    cqn = _rms_plain(z_cq)
    ckvn = _rms_plain(z_ckv)
    q_t = _dot(wuqt_ref[...], cqn.T.astype(BF16))
    v_t = _dot(wvt_ref[...], ckvn.T.astype(BF16))
    kn = _dot(ckvn.astype(BF16), wkn_ref[...])

    r = jax.nn.sigmoid(jnp.concatenate([p[:, :LRU_GROUP] for p in pre], axis=-1) + ba_ref[...])
    gi = jax.nn.sigmoid(jnp.concatenate([p[:, LRU_GROUP:] for p in pre], axis=-1) + bx_ref[...])
    nlam = -lam_ref[...]
    softplus = jnp.maximum(nlam, 0.0) + jnp.log1p(jnp.exp(-jnp.abs(nlam)))
    a = jnp.exp2(r * ((-LRU_C * math.log2(math.e)) * softplus))
    one_m_a2 = 1.0 - a * a
    root = jnp.where(one_m_a2 > 0.0, one_m_a2 * lax.rsqrt(one_m_a2), 0.0)
    v_in = root * (gi * u)
    h = _linear_scan(a, v_in, hprev_ref[SUBLANES - 1:SUBLANES, :])
    hprev_ref[...] = h[ts - SUBLANES:ts]
    y = h * _gelu_tanh(z_gate)
    ylru_ref[...] = _rms(y, glru_ref[...]).astype(BF16)

    half = QK_ROPE_DIM // 2
    ang_t = invf_ref[...] * pos_ref[...].astype(F32)
    cos_t = jnp.cos(ang_t)
    sin_t = jnp.sin(ang_t)
    for hd in range(MLA_HEADS):
        r0 = hd * HEAD_SLAB
        pe = r0 + QK_NOPE_DIM
        x1 = q_t[pe:pe + half]
        x2 = q_t[pe + half:pe + 2 * half]
        qt_ref[r0:pe, :] = q_t[r0:pe].astype(BF16)
        qt_ref[pe:pe + half, :] = (x1 * cos_t - x2 * sin_t).astype(BF16)
        qt_ref[pe + half:pe + 2 * half, :] = (x2 * cos_t + x1 * sin_t).astype(BF16)
        qt_ref[pe + 2 * half:r0 + HEAD_SLAB, :] = jnp.zeros((HEAD_SLAB - QK_NOPE_DIM - 2 * half, ts), BF16)
        v0 = hd * VT_ROWS
        vt_ref[v0:v0 + V_HEAD_DIM, :] = v_t[hd * V_HEAD_DIM:(hd + 1) * V_HEAD_DIM].astype(BF16)
        vt_ref[v0 + V_HEAD_DIM:v0 + VT_ROWS, :] = jnp.ones((BF16_ROWS, ts), BF16)
    pad = jnp.zeros((LANES - 2 * half, ts), F32)
    cos = jnp.concatenate([cos_t, cos_t, pad], axis=0).T
    sin = jnp.concatenate([sin_t, sin_t, pad], axis=0).T
    kpe = _rope(z_kpe, cos, sin).astype(BF16)
    for hd in range(MLA_HEADS):
        c0 = hd * HEAD_SLAB
        k_ref[:, c0:c0 + LANES] = kn[:, hd * LANES:(hd + 1) * LANES].astype(BF16)
        k_ref[:, c0 + LANES:c0 + HEAD_SLAB] = kpe


def _in_proj(x, pos_row, invf, win, convw, convb, wg, ba, bx, lam, wuqt, wkn, wvt, glru):
    bsz, s, _ = x.shape
    ts = TS_IN
    tok = lambda w: pl.BlockSpec((None, ts, w), lambda b, i: (b, i, 0))
    in_specs = [
        tok(D_MODEL), pl.BlockSpec((None, 1, ts), lambda b, i: (b, 0, i)),
        _const_spec((QK_ROPE_DIM // 2, 1)),
        _const_spec((D_MODEL, Z_WIDTH)), _const_spec((CONV_WIDTH, D_LRU)), _const_spec((1, D_LRU)),
        _const_spec((D_LRU // LRU_GROUP, LRU_GROUP, 2 * LRU_GROUP)),
        _const_spec((1, D_LRU)), _const_spec((1, D_LRU)), _const_spec((1, D_LRU)),
        _const_spec((MLA_HEADS * HEAD_SLAB, Q_LORA_RANK)),
        _const_spec((KV_LORA_RANK, MLA_HEADS * QK_NOPE_DIM)),
        _const_spec((D_MLA_OUT, KV_LORA_RANK)),
        _const_spec((1, D_LRU)),
    ]
    tile_t = lambda w: pl.BlockSpec((None, None, w, ts), lambda b, i: (b, i, 0, 0))
    out_shape = (
        jax.ShapeDtypeStruct((bsz, s // ts, MLA_HEADS * HEAD_SLAB, ts), BF16),
        jax.ShapeDtypeStruct((bsz, s, MLA_HEADS * HEAD_SLAB), BF16),
        jax.ShapeDtypeStruct((bsz, s // ts, MLA_HEADS * VT_ROWS, ts), BF16),
        jax.ShapeDtypeStruct((bsz, s, D_LRU), BF16),
    )
    out_specs = (tile_t(MLA_HEADS * HEAD_SLAB), tok(MLA_HEADS * HEAD_SLAB),
                 tile_t(MLA_HEADS * VT_ROWS), tok(D_LRU))
    return pl.pallas_call(
        _in_proj_kernel,
        out_shape=out_shape,
        grid=(bsz, s // ts),
        in_specs=in_specs,
        out_specs=out_specs,
        scratch_shapes=[pltpu.VMEM((SUBLANES, D_LRU), F32), pltpu.VMEM((SUBLANES, D_LRU), F32)],
        compiler_params=pltpu.CompilerParams(dimension_semantics=("parallel", "arbitrary"),
                                             vmem_limit_bytes=VMEM_LIMIT),
        name="in_proj",
    )(x, pos_row, invf, win, convw, convb, wg, ba, bx, lam, wuqt, wkn, wvt, glru)


def _mla_attn_kernel(qt_ref, k_ref, vt_ref, g_ref, o_ref, m_ref, acc_ref):
    tq = qt_ref.shape[1]
    i = pl.program_id(1)
    m_ref[...] = jnp.full(m_ref.shape, NEG_INF, F32)
    acc_ref[...] = jnp.zeros(acc_ref.shape, F32)

    def kv_tile(j, masked):
        start = pl.multiple_of(j * tq, tq)

        def scores(hd):
            slab = slice(hd * HEAD_SLAB, (hd + 1) * HEAD_SLAB)
            return _dot(k_ref[pl.ds(start, tq), slab], qt_ref[slab, :])

        st_next = scores(0)
        for hd in range(MLA_HEADS):
            st = st_next
            if hd + 1 < MLA_HEADS:
                st_next = scores(hd + 1)
            vrows = slice(hd * VT_ROWS, (hd + 1) * VT_ROWS)
            if masked:
                kid = lax.broadcasted_iota(jnp.int32, (tq, tq), 0)
                qid = lax.broadcasted_iota(jnp.int32, (tq, tq), 1)
                st = jnp.where(qid >= kid, st, NEG_INF)
            m_prev = m_ref[hd]
            m_new = jnp.maximum(m_prev, jnp.max(st, axis=0, keepdims=True))
            alpha = jnp.exp2(m_prev - m_new)
            p = jnp.exp2(st - m_new).astype(BF16)
            m_ref[hd] = m_new
            acc_ref[vrows, :] = alpha * acc_ref[vrows, :] + _dot(vt_ref[j, vrows, :], p)

    def body(j, carry):
        kv_tile(j, masked=False)
        return carry

    lax.fori_loop(0, i, body, 0)
    kv_tile(i, masked=True)

    outs = []
    for hd in range(MLA_HEADS):
        r0 = hd * VT_ROWS
        inv_l = 1.0 / acc_ref[r0 + V_HEAD_DIM:r0 + V_HEAD_DIM + 1, :]
        outs.append(acc_ref[r0:r0 + V_HEAD_DIM, :] * inv_l)
    out = jnp.concatenate(outs, axis=0).T
    o_ref[...] = _rms(out, g_ref[...]).astype(BF16)


def _mla_attn(qt, k, vt, g):
    bsz, nq, _, tq = qt.shape
    s = k.shape[1]
    return pl.pallas_call(
        _mla_attn_kernel,
        out_shape=jax.ShapeDtypeStruct((bsz, s, D_MLA_OUT), BF16),
        grid=(bsz, nq),
        in_specs=[
            pl.BlockSpec((None, None, MLA_HEADS * HEAD_SLAB, tq), lambda b, i: (b, i, 0, 0)),
            pl.BlockSpec((None, s, MLA_HEADS * HEAD_SLAB), lambda b, i: (b, 0, 0)),
            pl.BlockSpec((None, nq, MLA_HEADS * VT_ROWS, tq), lambda b, i: (b, 0, 0, 0)),
            _const_spec((1, D_MLA_OUT)),
        ],
        out_specs=pl.BlockSpec((None, tq, D_MLA_OUT), lambda b, i: (b, i, 0)),
        scratch_shapes=[pltpu.VMEM((MLA_HEADS, 1, tq), F32),
                        pltpu.VMEM((MLA_HEADS * VT_ROWS, tq), F32)],
        compiler_params=pltpu.CompilerParams(dimension_semantics=("parallel", "arbitrary"),
                                             vmem_limit_bytes=VMEM_LIMIT),
        name="mla_attn",
    )(qt, k, vt, g)


def _mix_mem_kernel(ylru_ref, ymla_ref, x_ref, wout_ref, gpost_ref, gpre_ref, wmq_ref, mk_ref,
                    mv_ref, wmo_ref, gpm_ref, h_ref, o_scr):
    half = D_LRU
    yo = _dot(ylru_ref[...], wout_ref[0:half, :]) + _dot(ymla_ref[...], wout_ref[half:, :])
    h1 = x_ref[...] + _rms(yo, gpost_ref[...])

    xn = _rms(h1, gpre_ref[...]).astype(BF16)
    mq = (_dot(xn, wmq_ref[...]) * (1.0 / math.sqrt(MEM_HEAD_DIM))).astype(BF16)
    for hd in range(MEM_HEADS):
        sl = slice(hd * MEM_HEAD_DIM, (hd + 1) * MEM_HEAD_DIM)
        s = _dot_nt(mq[:, sl], mk_ref[:, sl])
        m = jnp.max(s, axis=-1, keepdims=True)
        p = jnp.exp(s - m)
        l = jnp.sum(p, axis=-1, keepdims=True)
        o_scr[:, sl] = (_dot(p.astype(BF16), mv_ref[:, sl]) / l).astype(BF16)
    h_ref[...] = h1 + _rms(_dot(o_scr[...], wmo_ref[...]), gpm_ref[...])


def _mix_mem(ylru, ymla, x, wout, gpost, gpre, wmq, mk, mv, wmo, gpm):
    bsz, s, _ = x.shape
    tm = TM_MIX
    tok = lambda w: pl.BlockSpec((None, tm, w), lambda b, i: (b, i, 0))
    memspec = pl.BlockSpec((None, N_MEM, D_MODEL), lambda b, i: (b, 0, 0))
    sq = _const_spec((D_MODEL, D_MODEL))
    vec = _const_spec((1, D_MODEL))
    return pl.pallas_call(
        _mix_mem_kernel,
        out_shape=jax.ShapeDtypeStruct((bsz, s, D_MODEL), F32),
        grid=(bsz, s // tm),
        in_specs=[tok(D_LRU), tok(D_MLA_OUT), tok(D_MODEL), sq, vec, vec, sq, memspec, memspec,
                  sq, vec],
        out_specs=tok(D_MODEL),
        scratch_shapes=[pltpu.VMEM((tm, D_MODEL), BF16)],
        compiler_params=pltpu.CompilerParams(dimension_semantics=("parallel", "parallel"),
                                             vmem_limit_bytes=VMEM_LIMIT),
        name="mix_mem",
    )(ylru, ymla, x, wout, gpost, gpre, wmq, mk, mv, wmo, gpm)


def _ffn_kernel(h_ref, gpre_ref, wg_ref, wu_ref, wd_ref, gpost_ref, o_ref):
    h = h_ref[...]
    xn = _rms(h, gpre_ref[...]).astype(BF16)
    f = None
    for lo, hi in FF_CHUNKS:
        act = (jax.nn.silu(_dot(xn, wg_ref[:, lo:hi])) * _dot(xn, wu_ref[:, lo:hi])).astype(BF16)
        part = _dot(act, wd_ref[lo:hi, :])
        f = part if f is None else f + part
    o_ref[...] = h + _rms(f, gpost_ref[...])


def _ffn(h, gpre, wg, wu, wd, gpost):
    m, _ = h.shape
    tm = TM_FFN
    single = pl.Buffered(1)
    tok = pl.BlockSpec((tm, D_MODEL), lambda i: (i, 0))
    vec = _const_spec((1, D_MODEL))
    return pl.pallas_call(
        _ffn_kernel,
        out_shape=jax.ShapeDtypeStruct((m, D_MODEL), F32),
        grid=(m // tm,),
        in_specs=[
            tok, vec,
            pl.BlockSpec((D_MODEL, D_FF), lambda i: (0, 0), pipeline_mode=single),
            pl.BlockSpec((D_MODEL, D_FF), lambda i: (0, 0), pipeline_mode=single),
            pl.BlockSpec((D_FF, D_MODEL), lambda i: (0, 0), pipeline_mode=single),
            vec,
        ],
        out_specs=tok,
        compiler_params=pltpu.CompilerParams(dimension_semantics=("parallel",),
                                             vmem_limit_bytes=VMEM_LIMIT),
        name="ffn",
    )(h, gpre, wg, wu, wd, gpost)


def _block_diag(w):
    eye = jnp.eye(LRU_BLOCKS, dtype=w.dtype)
    return jnp.einsum('hij,hg->higj', w, eye).reshape(D_LRU, D_LRU)


def _gate_weights(wa, wx):
    da, dx = _block_diag(wa), _block_diag(wx)
    groups = []
    for j in range(D_LRU // LRU_GROUP):
        sl = slice(j * LRU_GROUP, (j + 1) * LRU_GROUP)
        groups.append(jnp.concatenate([da[sl, sl], dx[sl, sl]], axis=1))
    return jnp.stack(groups).astype(BF16)


def _pad_in_proj(w_in, gain):
    w = gain[:, None] * w_in
    pad = jnp.zeros((D_MODEL, Z_WIDTH - w.shape[1]), w.dtype)
    return jnp.concatenate([w, pad], axis=1).astype(BF16)


def _q_weights_t(w_uq, gain):
    dh = QK_NOPE_DIM + QK_ROPE_DIM
    scale = math.log2(math.e) / math.sqrt(dh)
    w = (scale * gain[:, None] * w_uq).reshape(Q_LORA_RANK, MLA_HEADS, dh)
    pad = jnp.zeros((Q_LORA_RANK, MLA_HEADS, HEAD_SLAB - dh), w.dtype)
    w = jnp.concatenate([w, pad], axis=-1).reshape(Q_LORA_RANK, MLA_HEADS * HEAD_SLAB)
    return w.T.astype(BF16)


def _kv_weights(w_ukv, gain):
    w = (gain[:, None] * w_ukv).reshape(KV_LORA_RANK, MLA_HEADS, QK_NOPE_DIM + V_HEAD_DIM)
    kn = w[:, :, :QK_NOPE_DIM].reshape(KV_LORA_RANK, MLA_HEADS * QK_NOPE_DIM)
    vv = w[:, :, QK_NOPE_DIM:].reshape(KV_LORA_RANK, MLA_HEADS * V_HEAD_DIM)
    return kn.astype(BF16), vv.T.astype(BF16)


def _rope_freq_column():
    inv_freq = ROPE_THETA ** (-jnp.arange(0, QK_ROPE_DIM, 2, dtype=F32) / QK_ROPE_DIM)
    return inv_freq.reshape(QK_ROPE_DIM // 2, 1)


def kernel(x, mem, positions, g_pre_mix, w_in, conv_w, conv_b, lru_wa, lru_ba, lru_wx, lru_bx, lru_lambda, g_q_lat, w_uq, g_kv_lat, w_ukv, g_lru_out, g_mla_out, w_out, g_post_mix, g_pre_mem, g_mem_kv, w_mq, w_mk, w_mv, w_mo, g_post_mem, g_pre_ffn, w_gate, w_up, w_down, g_post_ffn):
    bsz, s, d = x.shape
    depth = w_in.shape[0]
    assert d == D_MODEL and s % TS_IN == 0 and s % TM_MIX == 0
    assert (bsz * s) % TM_FFN == 0 and mem.shape == (bsz, N_MEM, D_MODEL)
    pos_row = positions.reshape(bsz, 1, s)
    invf = _rope_freq_column()
    row = lambda a: a.reshape(1, -1)

    h = x
    for l in range(depth):
        mk, mv = _mem_kv(mem, row(g_mem_kv[l]), w_mk[l].astype(BF16), w_mv[l].astype(BF16))
        wkn, wvt = _kv_weights(w_ukv[l], g_kv_lat[l])
        qt, k, vt, ylru = _in_proj(
            h, pos_row, invf, _pad_in_proj(w_in[l], g_pre_mix[l]), conv_w[l], row(conv_b[l]),
            _gate_weights(lru_wa[l], lru_wx[l]), row(lru_ba[l]), row(lru_bx[l]), row(lru_lambda[l]),
            _q_weights_t(w_uq[l], g_q_lat[l]), wkn, wvt, row(g_lru_out[l]))
        ymla = _mla_attn(qt, k, vt, row(g_mla_out[l]))
        h = _mix_mem(ylru, ymla, h, w_out[l].astype(BF16), row(g_post_mix[l]), row(g_pre_mem[l]),
                     w_mq[l].astype(BF16), mk, mv, w_mo[l].astype(BF16), row(g_post_mem[l]))
        h = _ffn(h.reshape(bsz * s, d), row(g_pre_ffn[l]), w_gate[l].astype(BF16),
                 w_up[l].astype(BF16), w_down[l].astype(BF16), row(g_post_ffn[l])).reshape(bsz, s, d)
    return h
```

```python
import math

import jax
import jax.numpy as jnp
from jax import lax
from jax.experimental import pallas as pl
from jax.experimental.pallas import tpu as pltpu

F32 = jnp.float32
BF16 = jnp.bfloat16

D_MODEL = 1024
N_MEM = 256
MEM_HEADS = 4
MEM_HEAD_DIM = D_MODEL // MEM_HEADS
D_LRU = 512
LRU_BLOCKS = 8
LRU_BLOCK_DIM = D_LRU // LRU_BLOCKS
CONV_WIDTH = 4
LRU_C = 8.0
MLA_HEADS = 4
QK_NOPE_DIM = 128
QK_ROPE_DIM = 64
V_HEAD_DIM = 128
Q_LORA_RANK = 384
KV_LORA_RANK = 256
D_MLA_OUT = MLA_HEADS * V_HEAD_DIM
ROPE_THETA = 10000.0
D_FF = 2816
RMS_EPS = 1e-6
NEG_INF = -1e30

LANES = 128
SUBLANES = 8
HEAD_SLAB = 2 * LANES
BF16_ROWS = 16
VT_ROWS = V_HEAD_DIM + BF16_ROWS
Z_LRU_X = 0
Z_LRU_GATE = Z_LRU_X + D_LRU
Z_CQ = Z_LRU_GATE + D_LRU
Z_CKV = Z_CQ + Q_LORA_RANK
Z_KPE = Z_CKV + KV_LORA_RANK
Z_WIDTH = Z_KPE + LANES
MXU_TILE = 256
LRU_GROUP = MXU_TILE

TS_IN = 512
TM_MIX = 512
TM_FFN = 512
FF_CHUNKS = ((0, 1024), (1024, 2048), (2048, D_FF))
VMEM_LIMIT = 56 * 1024 * 1024


def _rms(x, g):
    ms = jnp.mean(x * x, axis=-1, keepdims=True)
    return x * lax.rsqrt(ms + RMS_EPS) * g


def _rms_plain(x):
    ms = jnp.mean(x * x, axis=-1, keepdims=True)
    return x * lax.rsqrt(ms + RMS_EPS)


def _gelu_tanh(x):
    k1 = -2.0 * math.sqrt(2.0 / math.pi) * math.log2(math.e)
    k2 = k1 * 0.044715
    return x / (1.0 + jnp.exp2(x * (k1 + k2 * (x * x))))


def _dot(a, b):
    return jnp.dot(a, b, preferred_element_type=F32)


def _dot_nt(a, b):
    return lax.dot_general(a, b, (((1,), (1,)), ((), ())), preferred_element_type=F32)


def _const_spec(shape):
    return pl.BlockSpec(shape, lambda *_: (0,) * len(shape))


def _mem_kv_kernel(mem_ref, g_ref, wk_ref, wv_ref, mk_ref, mv_ref):
    mn = _rms(mem_ref[...], g_ref[...]).astype(BF16)
    mk_ref[...] = _dot(mn, wk_ref[...]).astype(BF16)
    mv_ref[...] = _dot(mn, wv_ref[...]).astype(BF16)


def _mem_kv(mem, g, wk, wv):
    bsz = mem.shape[0]
    blk = pl.BlockSpec((None, N_MEM, D_MODEL), lambda b: (b, 0, 0))
    return pl.pallas_call(
        _mem_kv_kernel,
        out_shape=(jax.ShapeDtypeStruct((bsz, N_MEM, D_MODEL), BF16),) * 2,
        grid=(bsz,),
        in_specs=[blk, _const_spec((1, D_MODEL)), _const_spec((D_MODEL, D_MODEL)),
                  _const_spec((D_MODEL, D_MODEL))],
        out_specs=(blk, blk),
        compiler_params=pltpu.CompilerParams(dimension_semantics=("parallel",),
                                             vmem_limit_bytes=VMEM_LIMIT),
        name="mem_kv",
    )(mem, g, wk, wv)


def _shift_rows(x, prev, d):
    r = pltpu.roll(x, d, 0)
    row = lax.broadcasted_iota(jnp.int32, prev.shape, 0)
    head = jnp.where(row < d, pltpu.roll(prev, d, 0), r[0:SUBLANES])
    return jnp.concatenate([head, r[SUBLANES:]], axis=0)


def _linear_scan(a, u, h0):
    t, c = a.shape
    groups = t // SUBLANES
    a = a.reshape(groups, SUBLANES, c)
    u = u.reshape(groups, SUBLANES, c)
    row = lax.broadcasted_iota(jnp.int32, a.shape, 1)
    for s in (1, 2, 4):
        a_s = pltpu.roll(a, s, 1)
        u_s = pltpu.roll(u, s, 1)
        m = row >= s
        u = jnp.where(m, a * u_s + u, u)
        a = jnp.where(m, a * a_s, a)
    hs = []
    h = h0
    for g in range(groups):
        hg = a[g] * h + u[g]
        hs.append(hg)
        h = hg[SUBLANES - 1:SUBLANES]
    return jnp.concatenate(hs, axis=0)


def _rope(t, cos, sin):
    half = QK_ROPE_DIM // 2
    return t * cos + pltpu.roll(t, half, 1) * sin - pltpu.roll(t, LANES - half, 1) * sin


def _in_proj_kernel(x_ref, pos_ref, invf_ref, win_ref, convw_ref, convb_ref, wg_ref,
                    ba_ref, bx_ref, lam_ref, wuqt_ref, wkn_ref, wvt_ref,
                    qt_ref, k_ref, vt_ref, ylru_ref, xprev_ref, hprev_ref):
    ts = x_ref.shape[0]

    @pl.when(pl.program_id(1) == 0)
    def _():
        xprev_ref[...] = jnp.zeros_like(xprev_ref)
        hprev_ref[...] = jnp.zeros_like(hprev_ref)

    xn = _rms_plain(x_ref[...]).astype(BF16)
    z_tiles = {}

    def z(lo, hi):
        parts = []
        for j in range(lo // MXU_TILE, -(-hi // MXU_TILE)):
            if j not in z_tiles:
                z_tiles[j] = _dot(xn, win_ref[:, j * MXU_TILE:(j + 1) * MXU_TILE])
            a, b = max(lo, j * MXU_TILE), min(hi, (j + 1) * MXU_TILE)
            parts.append(z_tiles[j][:, a - j * MXU_TILE:b - j * MXU_TILE])
        return parts[0] if len(parts) == 1 else jnp.concatenate(parts, axis=-1)

    xl = z(Z_LRU_X, Z_LRU_GATE)
    prev = xprev_ref[...]
    u = convb_ref[...] + convw_ref[CONV_WIDTH - 1:CONV_WIDTH, :] * xl
    for d in range(1, CONV_WIDTH):
        k = CONV_WIDTH - 1 - d
        u = u + convw_ref[k:k + 1, :] * _shift_rows(xl, prev, d)
    xprev_ref[...] = xl[ts - SUBLANES:ts]

    ub = u.astype(BF16)
    pre = [_dot(ub[:, j * LRU_GROUP:(j + 1) * LRU_GROUP], wg_ref[j])
           for j in range(D_LRU // LRU_GROUP)]

    z_gate = z(Z_LRU_GATE, Z_CQ)
    z_cq = z(Z_CQ, Z_CKV)
    z_ckv = z(Z_CKV, Z_KPE)
    z_kpe = z(Z_KPE, Z_WIDTH)
    cqn = _rms_plain(z_cq)
    ckvn = _rms_plain(z_ckv)
    q_t = _dot(wuqt_ref[...], cqn.T.astype(BF16))
    v_t = _dot(wvt_ref[...], ckvn.T.astype(BF16))
    kn = _dot(ckvn.astype(BF16), wkn_ref[...])

    r = jax.nn.sigmoid(jnp.concatenate([p[:, :LRU_GROUP] for p in pre], axis=-1) + ba_ref[...])
    gi = jax.nn.sigmoid(jnp.concatenate([p[:, LRU_GROUP:] for p in pre], axis=-1) + bx_ref[...])
    nlam = -lam_ref[...]
    softplus = jnp.maximum(nlam, 0.0) + jnp.log1p(jnp.exp(-jnp.abs(nlam)))
    a = jnp.exp2(r * ((-LRU_C * math.log2(math.e)) * softplus))
    one_m_a2 = 1.0 - a * a
    root = jnp.where(one_m_a2 > 0.0, one_m_a2 * lax.rsqrt(one_m_a2), 0.0)
    v_in = root * (gi * u)
    h = _linear_scan(a, v_in, hprev_ref[SUBLANES - 1:SUBLANES, :])
    hprev_ref[...] = h[ts - SUBLANES:ts]
    y = h * _gelu_tanh(z_gate)
    ylru_ref[...] = _rms_plain(y).astype(BF16)

    half = QK_ROPE_DIM // 2
    ang_t = invf_ref[...] * pos_ref[...].astype(F32)
    cos_t = jnp.cos(ang_t)
    sin_t = jnp.sin(ang_t)
    for hd in range(MLA_HEADS):
        r0 = hd * HEAD_SLAB
        pe = r0 + QK_NOPE_DIM
        x1 = q_t[pe:pe + half]
        x2 = q_t[pe + half:pe + 2 * half]
        qt_ref[r0:pe, :] = q_t[r0:pe].astype(BF16)
        qt_ref[pe:pe + half, :] = (x1 * cos_t - x2 * sin_t).astype(BF16)
        qt_ref[pe + half:pe + 2 * half, :] = (x2 * cos_t + x1 * sin_t).astype(BF16)
        qt_ref[pe + 2 * half:r0 + HEAD_SLAB, :] = jnp.zeros((HEAD_SLAB - QK_NOPE_DIM - 2 * half, ts), BF16)
        v0 = hd * VT_ROWS
        vt_ref[v0:v0 + V_HEAD_DIM, :] = v_t[hd * V_HEAD_DIM:(hd + 1) * V_HEAD_DIM].astype(BF16)
        vt_ref[v0 + V_HEAD_DIM:v0 + VT_ROWS, :] = jnp.ones((BF16_ROWS, ts), BF16)
    pad = jnp.zeros((LANES - 2 * half, ts), F32)
    cos = jnp.concatenate([cos_t, cos_t, pad], axis=0).T
    sin = jnp.concatenate([sin_t, sin_t, pad], axis=0).T
    kpe = _rope(z_kpe, cos, sin).astype(BF16)
    for hd in range(MLA_HEADS):
        c0 = hd * HEAD_SLAB
        k_ref[:, c0:c0 + LANES] = kn[:, hd * LANES:(hd + 1) * LANES].astype(BF16)
        k_ref[:, c0 + LANES:c0 + HEAD_SLAB] = kpe


def _in_proj(x, pos_row, invf, win, convw, convb, wg, ba, bx, lam, wuqt, wkn, wvt):
    bsz, s, _ = x.shape
    ts = TS_IN
    tok = lambda w: pl.BlockSpec((None, ts, w), lambda b, i: (b, i, 0))
    in_specs = [
        tok(D_MODEL), pl.BlockSpec((None, 1, ts), lambda b, i: (b, 0, i)),
        _const_spec((QK_ROPE_DIM // 2, 1)),
        _const_spec((D_MODEL, Z_WIDTH)), _const_spec((CONV_WIDTH, D_LRU)), _const_spec((1, D_LRU)),
        _const_spec((D_LRU // LRU_GROUP, LRU_GROUP, 2 * LRU_GROUP)),
        _const_spec((1, D_LRU)), _const_spec((1, D_LRU)), _const_spec((1, D_LRU)),
        _const_spec((MLA_HEADS * HEAD_SLAB, Q_LORA_RANK)),
        _const_spec((KV_LORA_RANK, MLA_HEADS * QK_NOPE_DIM)),
        _const_spec((D_MLA_OUT, KV_LORA_RANK)),
    ]
    tile_t = lambda w: pl.BlockSpec((None, None, w, ts), lambda b, i: (b, i, 0, 0))
    out_shape = (
        jax.ShapeDtypeStruct((bsz, s // ts, MLA_HEADS * HEAD_SLAB, ts), BF16),
        jax.ShapeDtypeStruct((bsz, s, MLA_HEADS * HEAD_SLAB), BF16),
        jax.ShapeDtypeStruct((bsz, s // ts, MLA_HEADS * VT_ROWS, ts), BF16),
        jax.ShapeDtypeStruct((bsz, s, D_LRU), BF16),
    )
    out_specs = (tile_t(MLA_HEADS * HEAD_SLAB), tok(MLA_HEADS * HEAD_SLAB),
                 tile_t(MLA_HEADS * VT_ROWS), tok(D_LRU))
    return pl.pallas_call(
        _in_proj_kernel,
        out_shape=out_shape,
        grid=(bsz, s // ts),
        in_specs=in_specs,
        out_specs=out_specs,
        scratch_shapes=[pltpu.VMEM((SUBLANES, D_LRU), F32), pltpu.VMEM((SUBLANES, D_LRU), F32)],
        compiler_params=pltpu.CompilerParams(dimension_semantics=("parallel", "arbitrary"),
                                             vmem_limit_bytes=VMEM_LIMIT),
        name="in_proj",
    )(x, pos_row, invf, win, convw, convb, wg, ba, bx, lam, wuqt, wkn, wvt)


def _mla_attn_kernel(qt_ref, k_ref, vt_ref, o_ref, acc_ref):
    tq = qt_ref.shape[1]
    nq = vt_ref.shape[0]
    i = pl.program_id(1)

    def scores(j, hd):
        slab = slice(hd * HEAD_SLAB, (hd + 1) * HEAD_SLAB)
        return _dot(k_ref[j * tq:(j + 1) * tq, slab], qt_ref[slab, :])

    def attend(n_tiles):
        units = [(j, hd) for j in range(n_tiles) for hd in range(MLA_HEADS)]
        m = [jnp.full((1, tq), NEG_INF, F32)] * MLA_HEADS
        acc = [None] * MLA_HEADS
        st_next = scores(*units[0])
        for n, (j, hd) in enumerate(units):
            st = st_next
            if n + 1 < len(units):
                st_next = scores(*units[n + 1])
            vrows = slice(hd * VT_ROWS, (hd + 1) * VT_ROWS)
            if j == n_tiles - 1:
                kid = lax.broadcasted_iota(jnp.int32, (tq, tq), 0)
                qid = lax.broadcasted_iota(jnp.int32, (tq, tq), 1)
                st = jnp.where(qid >= kid, st, NEG_INF)
            m_new = jnp.maximum(m[hd], jnp.max(st, axis=0, keepdims=True))
            p = jnp.exp2(st - m_new).astype(BF16)
            pv = _dot(vt_ref[j, vrows, :], p)
            acc[hd] = pv if acc[hd] is None else jnp.exp2(m[hd] - m_new) * acc[hd] + pv
            m[hd] = m_new
        for hd in range(MLA_HEADS):
            acc_ref[hd * VT_ROWS:(hd + 1) * VT_ROWS, :] = acc[hd]

    for n in range(nq):
        @pl.when(i == n)
        def _(n=n):
            attend(n + 1)

    outs = []
    for hd in range(MLA_HEADS):
        r0 = hd * VT_ROWS
        inv_l = 1.0 / acc_ref[r0 + V_HEAD_DIM:r0 + V_HEAD_DIM + 1, :]
        outs.append(acc_ref[r0:r0 + V_HEAD_DIM, :] * inv_l)
    out_t = jnp.concatenate(outs, axis=0)
    ms = jnp.mean(out_t * out_t, axis=0, keepdims=True)
    o_ref[...] = (out_t * lax.rsqrt(ms + RMS_EPS)).astype(BF16).T


def _mla_attn(qt, k, vt):
    bsz, nq, _, tq = qt.shape
    s = k.shape[1]
    return pl.pallas_call(
        _mla_attn_kernel,
        out_shape=jax.ShapeDtypeStruct((bsz, s, D_MLA_OUT), BF16),
        grid=(bsz, nq),
        in_specs=[
            pl.BlockSpec((None, None, MLA_HEADS * HEAD_SLAB, tq), lambda b, i: (b, i, 0, 0)),
            pl.BlockSpec((None, s, MLA_HEADS * HEAD_SLAB), lambda b, i: (b, 0, 0)),
            pl.BlockSpec((None, nq, MLA_HEADS * VT_ROWS, tq), lambda b, i: (b, 0, 0, 0)),
        ],
        out_specs=pl.BlockSpec((None, tq, D_MLA_OUT), lambda b, i: (b, i, 0)),
        scratch_shapes=[pltpu.VMEM((MLA_HEADS * VT_ROWS, tq), F32)],
        compiler_params=pltpu.CompilerParams(dimension_semantics=("parallel", "arbitrary"),
                                             vmem_limit_bytes=VMEM_LIMIT),
        name="mla_attn",
    )(qt, k, vt)


def _mix_mem_kernel(ylru_ref, ymla_ref, x_ref, wout_ref, gpost_ref, gpre_ref, wmq_ref, mk_ref,
                    mv_ref, wmo_ref, gpm_ref, h_ref, o_scr):
    half = D_LRU
    yo = _dot(ylru_ref[...], wout_ref[0:half, :]) + _dot(ymla_ref[...], wout_ref[half:, :])
    h1 = x_ref[...] + _rms(yo, gpost_ref[...])

    xn = _rms(h1, gpre_ref[...]).astype(BF16)
    mq = (_dot(xn, wmq_ref[...]) * (1.0 / math.sqrt(MEM_HEAD_DIM))).astype(BF16)
    for hd in range(MEM_HEADS):
        sl = slice(hd * MEM_HEAD_DIM, (hd + 1) * MEM_HEAD_DIM)
        s = _dot_nt(mq[:, sl], mk_ref[:, sl])
        m = jnp.max(s, axis=-1, keepdims=True)
        p = jnp.exp(s - m)
        l = jnp.sum(p, axis=-1, keepdims=True)
        o_scr[:, sl] = (_dot(p.astype(BF16), mv_ref[:, sl]) / l).astype(BF16)
    h_ref[...] = h1 + _rms(_dot(o_scr[...], wmo_ref[...]), gpm_ref[...])


def _mix_mem(ylru, ymla, x, wout, gpost, gpre, wmq, mk, mv, wmo, gpm):
    bsz, s, _ = x.shape
    tm = TM_MIX
    tok = lambda w: pl.BlockSpec((None, tm, w), lambda b, i: (b, i, 0))
    memspec = pl.BlockSpec((None, N_MEM, D_MODEL), lambda b, i: (b, 0, 0))
    sq = _const_spec((D_MODEL, D_MODEL))
    vec = _const_spec((1, D_MODEL))
    return pl.pallas_call(
        _mix_mem_kernel,
        out_shape=jax.ShapeDtypeStruct((bsz, s, D_MODEL), F32),
        grid=(bsz, s // tm),
        in_specs=[tok(D_LRU), tok(D_MLA_OUT), tok(D_MODEL), sq, vec, vec, sq, memspec, memspec,
                  sq, vec],
        out_specs=tok(D_MODEL),
        scratch_shapes=[pltpu.VMEM((tm, D_MODEL), BF16)],
        compiler_params=pltpu.CompilerParams(dimension_semantics=("parallel", "parallel"),
                                             vmem_limit_bytes=VMEM_LIMIT),
        name="mix_mem",
    )(ylru, ymla, x, wout, gpost, gpre, wmq, mk, mv, wmo, gpm)


def _ffn_kernel(h_ref, gpre_ref, wg_ref, wu_ref, wd_ref, gpost_ref, o_ref):
    h = h_ref[...]
    xn = _rms(h, gpre_ref[...]).astype(BF16)
    f = None
    for lo, hi in FF_CHUNKS:
        act = (jax.nn.silu(_dot(xn, wg_ref[:, lo:hi])) * _dot(xn, wu_ref[:, lo:hi])).astype(BF16)
        part = _dot(act, wd_ref[lo:hi, :])
        f = part if f is None else f + part
    o_ref[...] = h + _rms(f, gpost_ref[...])


def _ffn(h, gpre, wg, wu, wd, gpost):
    m, _ = h.shape
    tm = TM_FFN
    single = pl.Buffered(1)
    tok = pl.BlockSpec((tm, D_MODEL), lambda i: (i, 0))
    vec = _const_spec((1, D_MODEL))
    return pl.pallas_call(
        _ffn_kernel,
        out_shape=jax.ShapeDtypeStruct((m, D_MODEL), F32),
        grid=(m // tm,),
        in_specs=[
            tok, vec,
            pl.BlockSpec((D_MODEL, D_FF), lambda i: (0, 0), pipeline_mode=single),
            pl.BlockSpec((D_MODEL, D_FF), lambda i: (0, 0), pipeline_mode=single),
            pl.BlockSpec((D_FF, D_MODEL), lambda i: (0, 0), pipeline_mode=single),
            vec,
        ],
        out_specs=tok,
        compiler_params=pltpu.CompilerParams(dimension_semantics=("parallel",),
                                             vmem_limit_bytes=VMEM_LIMIT),
        name="ffn",
    )(h, gpre, wg, wu, wd, gpost)


def _block_diag(w):
    eye = jnp.eye(LRU_BLOCKS, dtype=w.dtype)
    return jnp.einsum('hij,hg->higj', w, eye).reshape(D_LRU, D_LRU)


def _gate_weights(wa, wx):
    da, dx = _block_diag(wa), _block_diag(wx)
    groups = []
    for j in range(D_LRU // LRU_GROUP):
        sl = slice(j * LRU_GROUP, (j + 1) * LRU_GROUP)
        groups.append(jnp.concatenate([da[sl, sl], dx[sl, sl]], axis=1))
    return jnp.stack(groups).astype(BF16)


def _pad_in_proj(w_in, gain):
    w = gain[:, None] * w_in
    pad = jnp.zeros((D_MODEL, Z_WIDTH - w.shape[1]), w.dtype)
    return jnp.concatenate([w, pad], axis=1).astype(BF16)


def _q_weights_t(w_uq, gain):
    dh = QK_NOPE_DIM + QK_ROPE_DIM
    scale = math.log2(math.e) / math.sqrt(dh)
    w = (scale * gain[:, None] * w_uq).reshape(Q_LORA_RANK, MLA_HEADS, dh)
    pad = jnp.zeros((Q_LORA_RANK, MLA_HEADS, HEAD_SLAB - dh), w.dtype)
    w = jnp.concatenate([w, pad], axis=-1).reshape(Q_LORA_RANK, MLA_HEADS * HEAD_SLAB)
    return w.T.astype(BF16)


def _kv_weights(w_ukv, gain):
    w = (gain[:, None] * w_ukv).reshape(KV_LORA_RANK, MLA_HEADS, QK_NOPE_DIM + V_HEAD_DIM)
    kn = w[:, :, :QK_NOPE_DIM].reshape(KV_LORA_RANK, MLA_HEADS * QK_NOPE_DIM)
    vv = w[:, :, QK_NOPE_DIM:].reshape(KV_LORA_RANK, MLA_HEADS * V_HEAD_DIM)
    return kn.astype(BF16), vv.T.astype(BF16)


def _rope_freq_column():
    inv_freq = ROPE_THETA ** (-jnp.arange(0, QK_ROPE_DIM, 2, dtype=F32) / QK_ROPE_DIM)
    return inv_freq.reshape(QK_ROPE_DIM // 2, 1)


def kernel(x, mem, positions, g_pre_mix, w_in, conv_w, conv_b, lru_wa, lru_ba, lru_wx, lru_bx, lru_lambda, g_q_lat, w_uq, g_kv_lat, w_ukv, g_lru_out, g_mla_out, w_out, g_post_mix, g_pre_mem, g_mem_kv, w_mq, w_mk, w_mv, w_mo, g_post_mem, g_pre_ffn, w_gate, w_up, w_down, g_post_ffn):
    bsz, s, d = x.shape
    depth = w_in.shape[0]
    assert d == D_MODEL and s % TS_IN == 0 and s % TM_MIX == 0
    assert (bsz * s) % TM_FFN == 0 and mem.shape == (bsz, N_MEM, D_MODEL)
    pos_row = positions.reshape(bsz, 1, s)
    invf = _rope_freq_column()
    row = lambda a: a.reshape(1, -1)

    h = x
    for l in range(depth):
        mk, mv = _mem_kv(mem, row(g_mem_kv[l]), w_mk[l].astype(BF16), w_mv[l].astype(BF16))
        wkn, wvt = _kv_weights(w_ukv[l], g_kv_lat[l])
        qt, k, vt, ylru = _in_proj(
            h, pos_row, invf, _pad_in_proj(w_in[l], g_pre_mix[l]), conv_w[l], row(conv_b[l]),
            _gate_weights(lru_wa[l], lru_wx[l]), row(lru_ba[l]), row(lru_bx[l]), row(lru_lambda[l]),
            _q_weights_t(w_uq[l], g_q_lat[l]), wkn, wvt)
        ymla = _mla_attn(qt, k, vt)
        wout = (jnp.concatenate([g_lru_out[l], g_mla_out[l]])[:, None] * w_out[l]).astype(BF16)
        h = _mix_mem(ylru, ymla, h, wout, row(g_post_mix[l]), row(g_pre_mem[l]),
                     w_mq[l].astype(BF16), mk, mv, w_mo[l].astype(BF16), row(g_post_mem[l]))
        h = _ffn(h.reshape(bsz * s, d), row(g_pre_ffn[l]), w_gate[l].astype(BF16),
                 w_up[l].astype(BF16), w_down[l].astype(BF16), row(g_post_ffn[l])).reshape(bsz, s, d)
    return h
```

```python
import math

import jax
import jax.numpy as jnp
from jax import lax
from jax.experimental import pallas as pl
from jax.experimental.pallas import tpu as pltpu

F32 = jnp.float32
BF16 = jnp.bfloat16

D_MODEL = 1024
N_MEM = 256
MEM_HEADS = 4
MEM_HEAD_DIM = D_MODEL // MEM_HEADS
D_LRU = 512
LRU_BLOCKS = 8
LRU_BLOCK_DIM = D_LRU // LRU_BLOCKS
CONV_WIDTH = 4
LRU_C = 8.0
MLA_HEADS = 4
QK_NOPE_DIM = 128
QK_ROPE_DIM = 64
V_HEAD_DIM = 128
Q_LORA_RANK = 384
KV_LORA_RANK = 256
D_MLA_OUT = MLA_HEADS * V_HEAD_DIM
ROPE_THETA = 10000.0
D_FF = 2816
RMS_EPS = 1e-6
NEG_INF = -1e30

LANES = 128
SUBLANES = 8
HEAD_SLAB = 2 * LANES
BF16_ROWS = 16
VT_ROWS = V_HEAD_DIM + BF16_ROWS
Z_LRU_X = 0
Z_LRU_GATE = Z_LRU_X + D_LRU
Z_CQ = Z_LRU_GATE + D_LRU
Z_CKV = Z_CQ + Q_LORA_RANK
Z_KPE = Z_CKV + KV_LORA_RANK
Z_WIDTH = Z_KPE + LANES
MXU_TILE = 256
LRU_GROUP = MXU_TILE

TS_IN = 512
TM_MIX = 1024
ROW_SUB = 512
TM_FFN = 1024
FF_CHUNKS = ((0, 1024), (1024, 2048), (2048, D_FF))
VMEM_LIMIT = 56 * 1024 * 1024


def _rms(x, g):
    ms = jnp.mean(x * x, axis=-1, keepdims=True)
    return x * lax.rsqrt(ms + RMS_EPS) * g


def _rms_plain(x):
    ms = jnp.mean(x * x, axis=-1, keepdims=True)
    return x * lax.rsqrt(ms + RMS_EPS)


def _gelu_tanh(x):
    k1 = -2.0 * math.sqrt(2.0 / math.pi) * math.log2(math.e)
    k2 = k1 * 0.044715
    return x / (1.0 + jnp.exp2(x * (k1 + k2 * (x * x))))


def _dot(a, b):
    return jnp.dot(a, b, preferred_element_type=F32)


def _dot_nt(a, b):
    return lax.dot_general(a, b, (((1,), (1,)), ((), ())), preferred_element_type=F32)


def _const_spec(shape):
    return pl.BlockSpec(shape, lambda *_: (0,) * len(shape))


def _mem_kv_kernel(mem_ref, g_ref, wk_ref, wv_ref, mk_ref, mv_ref):
    mn = _rms(mem_ref[...], g_ref[...]).astype(BF16)
    mk_ref[...] = _dot(mn, wk_ref[...]).astype(BF16)
    mv_ref[...] = _dot(mn, wv_ref[...]).astype(BF16)


def _mem_kv(mem, g, wk, wv):
    bsz = mem.shape[0]
    blk = pl.BlockSpec((None, N_MEM, D_MODEL), lambda b: (b, 0, 0))
    return pl.pallas_call(
        _mem_kv_kernel,
        out_shape=(jax.ShapeDtypeStruct((bsz, N_MEM, D_MODEL), BF16),) * 2,
        grid=(bsz,),
        in_specs=[blk, _const_spec((1, D_MODEL)), _const_spec((D_MODEL, D_MODEL)),
                  _const_spec((D_MODEL, D_MODEL))],
        out_specs=(blk, blk),
        compiler_params=pltpu.CompilerParams(dimension_semantics=("parallel",),
                                             vmem_limit_bytes=VMEM_LIMIT),
        name="mem_kv",
    )(mem, g, wk, wv)


def _shift_rows(x, prev, d):
    r = pltpu.roll(x, d, 0)
    row = lax.broadcasted_iota(jnp.int32, prev.shape, 0)
    head = jnp.where(row < d, pltpu.roll(prev, d, 0), r[0:SUBLANES])
    return jnp.concatenate([head, r[SUBLANES:]], axis=0)


def _linear_scan(a, u, h0):
    t, c = a.shape
    groups = t // SUBLANES
    a = a.reshape(groups, SUBLANES, c)
    u = u.reshape(groups, SUBLANES, c)
    row = lax.broadcasted_iota(jnp.int32, a.shape, 1)
    for s in (1, 2, 4):
        a_s = pltpu.roll(a, s, 1)
        u_s = pltpu.roll(u, s, 1)
        m = row >= s
        u = jnp.where(m, a * u_s + u, u)
        a = jnp.where(m, a * a_s, a)
    hs = []
    h = h0
    for g in range(groups):
        hg = a[g] * h + u[g]
        hs.append(hg)
        h = hg[SUBLANES - 1:SUBLANES]
    return jnp.concatenate(hs, axis=0)


def _rope(t, cos, sin):
    half = QK_ROPE_DIM // 2
    return t * cos + pltpu.roll(t, half, 1) * sin - pltpu.roll(t, LANES - half, 1) * sin


def _in_proj_kernel(x_ref, pos_ref, invf_ref, win_ref, convw_ref, convb_ref, wg_ref,
                    ba_ref, bx_ref, lam_ref, wuqt_ref, wkn_ref, wvt_ref,
                    qt_ref, k_ref, vt_ref, ylru_ref, xprev_ref, hprev_ref):
    ts = x_ref.shape[0]

    @pl.when(pl.program_id(1) == 0)
    def _():
        xprev_ref[...] = jnp.zeros_like(xprev_ref)
        hprev_ref[...] = jnp.zeros_like(hprev_ref)

    xn = _rms_plain(x_ref[...]).astype(BF16)
    z_tiles = {}

    def z(lo, hi):
        parts = []
        for j in range(lo // MXU_TILE, -(-hi // MXU_TILE)):
            if j not in z_tiles:
                z_tiles[j] = _dot(xn, win_ref[:, j * MXU_TILE:(j + 1) * MXU_TILE])
            a, b = max(lo, j * MXU_TILE), min(hi, (j + 1) * MXU_TILE)
            parts.append(z_tiles[j][:, a - j * MXU_TILE:b - j * MXU_TILE])
        return parts[0] if len(parts) == 1 else jnp.concatenate(parts, axis=-1)

    xl = z(Z_LRU_X, Z_LRU_GATE)
    prev = xprev_ref[...]
    u = convb_ref[...] + convw_ref[CONV_WIDTH - 1:CONV_WIDTH, :] * xl
    for d in range(1, CONV_WIDTH):
        k = CONV_WIDTH - 1 - d
        u = u + convw_ref[k:k + 1, :] * _shift_rows(xl, prev, d)
    xprev_ref[...] = xl[ts - SUBLANES:ts]

    ub = u.astype(BF16)
    pre = [_dot(ub[:, j * LRU_GROUP:(j + 1) * LRU_GROUP], wg_ref[j])
           for j in range(D_LRU // LRU_GROUP)]

    z_gate = z(Z_LRU_GATE, Z_CQ)
    z_cq = z(Z_CQ, Z_CKV)
    z_ckv = z(Z_CKV, Z_KPE)
    z_kpe = z(Z_KPE, Z_WIDTH)
    cqn = _rms_plain(z_cq)
    ckvn = _rms_plain(z_ckv)
    q_t = _dot(wuqt_ref[...], cqn.T.astype(BF16))
    v_t = _dot(wvt_ref[...], ckvn.T.astype(BF16))
    kn = _dot(ckvn.astype(BF16), wkn_ref[...])

    r = jax.nn.sigmoid(jnp.concatenate([p[:, :LRU_GROUP] for p in pre], axis=-1) + ba_ref[...])
    gi = jax.nn.sigmoid(jnp.concatenate([p[:, LRU_GROUP:] for p in pre], axis=-1) + bx_ref[...])
    nlam = -lam_ref[...]
    softplus = jnp.maximum(nlam, 0.0) + jnp.log1p(jnp.exp(-jnp.abs(nlam)))
    a = jnp.exp2(r * ((-LRU_C * math.log2(math.e)) * softplus))
    one_m_a2 = 1.0 - a * a
    root = jnp.where(one_m_a2 > 0.0, one_m_a2 * lax.rsqrt(one_m_a2), 0.0)
    v_in = root * (gi * u)
    h = _linear_scan(a, v_in, hprev_ref[SUBLANES - 1:SUBLANES, :])
    hprev_ref[...] = h[ts - SUBLANES:ts]
    y = h * _gelu_tanh(z_gate)
    ylru_ref[...] = _rms_plain(y).astype(BF16)

    half = QK_ROPE_DIM // 2
    ang_t = invf_ref[...] * pos_ref[...].astype(F32)
    cos_t = jnp.cos(ang_t)
    sin_t = jnp.sin(ang_t)
    for hd in range(MLA_HEADS):
        r0 = hd * HEAD_SLAB
        pe = r0 + QK_NOPE_DIM
        x1 = q_t[pe:pe + half]
        x2 = q_t[pe + half:pe + 2 * half]
        qt_ref[r0:pe, :] = q_t[r0:pe].astype(BF16)
        qt_ref[pe:pe + half, :] = (x1 * cos_t - x2 * sin_t).astype(BF16)
        qt_ref[pe + half:pe + 2 * half, :] = (x2 * cos_t + x1 * sin_t).astype(BF16)
        qt_ref[pe + 2 * half:r0 + HEAD_SLAB, :] = jnp.zeros((HEAD_SLAB - QK_NOPE_DIM - 2 * half, ts), BF16)
        v0 = hd * VT_ROWS
        vt_ref[v0:v0 + V_HEAD_DIM, :] = v_t[hd * V_HEAD_DIM:(hd + 1) * V_HEAD_DIM].astype(BF16)
        vt_ref[v0 + V_HEAD_DIM:v0 + VT_ROWS, :] = jnp.ones((BF16_ROWS, ts), BF16)
    pad = jnp.zeros((LANES - 2 * half, ts), F32)
    cos = jnp.concatenate([cos_t, cos_t, pad], axis=0).T
    sin = jnp.concatenate([sin_t, sin_t, pad], axis=0).T
    kpe = _rope(z_kpe, cos, sin).astype(BF16)
    for hd in range(MLA_HEADS):
        c0 = hd * HEAD_SLAB
        k_ref[:, c0:c0 + LANES] = kn[:, hd * LANES:(hd + 1) * LANES].astype(BF16)
        k_ref[:, c0 + LANES:c0 + HEAD_SLAB] = kpe


def _in_proj(x, pos_row, invf, win, convw, convb, wg, ba, bx, lam, wuqt, wkn, wvt):
    bsz, s, _ = x.shape
    ts = TS_IN
    tok = lambda w: pl.BlockSpec((None, ts, w), lambda b, i: (b, i, 0))
    in_specs = [
        tok(D_MODEL), pl.BlockSpec((None, 1, ts), lambda b, i: (b, 0, i)),
        _const_spec((QK_ROPE_DIM // 2, 1)),
        _const_spec((D_MODEL, Z_WIDTH)), _const_spec((CONV_WIDTH, D_LRU)), _const_spec((1, D_LRU)),
        _const_spec((D_LRU // LRU_GROUP, LRU_GROUP, 2 * LRU_GROUP)),
        _const_spec((1, D_LRU)), _const_spec((1, D_LRU)), _const_spec((1, D_LRU)),
        _const_spec((MLA_HEADS * HEAD_SLAB, Q_LORA_RANK)),
        _const_spec((KV_LORA_RANK, MLA_HEADS * QK_NOPE_DIM)),
        _const_spec((D_MLA_OUT, KV_LORA_RANK)),
    ]
    tile_t = lambda w: pl.BlockSpec((None, None, w, ts), lambda b, i: (b, i, 0, 0))
    out_shape = (
        jax.ShapeDtypeStruct((bsz, s // ts, MLA_HEADS * HEAD_SLAB, ts), BF16),
        jax.ShapeDtypeStruct((bsz, s, MLA_HEADS * HEAD_SLAB), BF16),
        jax.ShapeDtypeStruct((bsz, s // ts, MLA_HEADS * VT_ROWS, ts), BF16),
        jax.ShapeDtypeStruct((bsz, s, D_LRU), BF16),
    )
    out_specs = (tile_t(MLA_HEADS * HEAD_SLAB), tok(MLA_HEADS * HEAD_SLAB),
                 tile_t(MLA_HEADS * VT_ROWS), tok(D_LRU))
    return pl.pallas_call(
        _in_proj_kernel,
        out_shape=out_shape,
        grid=(bsz, s // ts),
        in_specs=in_specs,
        out_specs=out_specs,
        scratch_shapes=[pltpu.VMEM((SUBLANES, D_LRU), F32), pltpu.VMEM((SUBLANES, D_LRU), F32)],
        compiler_params=pltpu.CompilerParams(dimension_semantics=("parallel", "arbitrary"),
                                             vmem_limit_bytes=VMEM_LIMIT),
        name="in_proj",
    )(x, pos_row, invf, win, convw, convb, wg, ba, bx, lam, wuqt, wkn, wvt)


def _mla_attn_kernel(qt_ref, k_ref, vt_ref, o_ref, acc_ref):
    tq = qt_ref.shape[1]
    nq = vt_ref.shape[0]
    i = pl.program_id(1)

    def scores(j, hd):
        slab = slice(hd * HEAD_SLAB, (hd + 1) * HEAD_SLAB)
        return _dot(k_ref[j * tq:(j + 1) * tq, slab], qt_ref[slab, :])

    def attend(n_tiles):
        units = [(j, hd) for j in range(n_tiles) for hd in range(MLA_HEADS)]
        m = [jnp.full((1, tq), NEG_INF, F32)] * MLA_HEADS
        acc = [None] * MLA_HEADS
        st_next = scores(*units[0])
        for n, (j, hd) in enumerate(units):
            st = st_next
            if n + 1 < len(units):
                st_next = scores(*units[n + 1])
            vrows = slice(hd * VT_ROWS, (hd + 1) * VT_ROWS)
            if j == n_tiles - 1:
                kid = lax.broadcasted_iota(jnp.int32, (tq, tq), 0)
                qid = lax.broadcasted_iota(jnp.int32, (tq, tq), 1)
                st = jnp.where(qid >= kid, st, NEG_INF)
            m_new = jnp.maximum(m[hd], jnp.max(st, axis=0, keepdims=True))
            p = jnp.exp2(st - m_new).astype(BF16)
            pv = _dot(vt_ref[j, vrows, :], p)
            acc[hd] = pv if acc[hd] is None else jnp.exp2(m[hd] - m_new) * acc[hd] + pv
            m[hd] = m_new
        for hd in range(MLA_HEADS):
            acc_ref[hd * VT_ROWS:(hd + 1) * VT_ROWS, :] = acc[hd]

    for n in range(nq):
        @pl.when(i == n)
        def _(n=n):
            attend(n + 1)

    outs = []
    for hd in range(MLA_HEADS):
        r0 = hd * VT_ROWS
        inv_l = 1.0 / acc_ref[r0 + V_HEAD_DIM:r0 + V_HEAD_DIM + 1, :]
        outs.append(acc_ref[r0:r0 + V_HEAD_DIM, :] * inv_l)
    out_t = jnp.concatenate(outs, axis=0)
    ms = jnp.mean(out_t * out_t, axis=0, keepdims=True)
    o_ref[...] = (out_t * lax.rsqrt(ms + RMS_EPS)).astype(BF16).T


def _mla_attn(qt, k, vt):
    bsz, nq, _, tq = qt.shape
    s = k.shape[1]
    return pl.pallas_call(
        _mla_attn_kernel,
        out_shape=jax.ShapeDtypeStruct((bsz, s, D_MLA_OUT), BF16),
        grid=(bsz, nq),
        in_specs=[
            pl.BlockSpec((None, None, MLA_HEADS * HEAD_SLAB, tq), lambda b, i: (b, i, 0, 0)),
            pl.BlockSpec((None, s, MLA_HEADS * HEAD_SLAB), lambda b, i: (b, 0, 0)),
            pl.BlockSpec((None, nq, MLA_HEADS * VT_ROWS, tq), lambda b, i: (b, 0, 0, 0)),
        ],
        out_specs=pl.BlockSpec((None, tq, D_MLA_OUT), lambda b, i: (b, i, 0)),
        scratch_shapes=[pltpu.VMEM((MLA_HEADS * VT_ROWS, tq), F32)],
        compiler_params=pltpu.CompilerParams(dimension_semantics=("parallel", "arbitrary"),
                                             vmem_limit_bytes=VMEM_LIMIT),
        name="mla_attn",
    )(qt, k, vt)


def _mix_mem_kernel(ylru_ref, ymla_ref, x_ref, wout_ref, gpost_ref, gpre_ref, wmq_ref, mk_ref,
                    mv_ref, wmo_ref, gpm_ref, h_ref, o_scr):
    subs = [slice(r, r + ROW_SUB) for r in range(0, x_ref.shape[0], ROW_SUB)]
    yo = [_dot(ylru_ref[r, :], wout_ref[0:D_LRU, :]) + _dot(ymla_ref[r, :], wout_ref[D_LRU:, :])
          for r in subs]
    h1 = [x_ref[r, :] + _rms(y, gpost_ref[...]) for r, y in zip(subs, yo)]

    xn = [_rms(h, gpre_ref[...]).astype(BF16) for h in h1]
    mq = [(_dot(v, wmq_ref[...]) * (1.0 / math.sqrt(MEM_HEAD_DIM))).astype(BF16) for v in xn]
    for r, q in zip(subs, mq):
        for hd in range(MEM_HEADS):
            sl = slice(hd * MEM_HEAD_DIM, (hd + 1) * MEM_HEAD_DIM)
            s = _dot_nt(q[:, sl], mk_ref[:, sl])
            m = jnp.max(s, axis=-1, keepdims=True)
            p = jnp.exp(s - m)
            l = jnp.sum(p, axis=-1, keepdims=True)
            o_scr[r, sl] = (_dot(p.astype(BF16), mv_ref[:, sl]) / l).astype(BF16)
    mo = [_dot(o_scr[r, :], wmo_ref[...]) for r in subs]
    for r, h, o in zip(subs, h1, mo):
        h_ref[r, :] = h + _rms(o, gpm_ref[...])


def _mix_mem(ylru, ymla, x, wout, gpost, gpre, wmq, mk, mv, wmo, gpm):
    bsz, s, _ = x.shape
    tm = TM_MIX
    tok = lambda w: pl.BlockSpec((None, tm, w), lambda b, i: (b, i, 0))
    memspec = pl.BlockSpec((None, N_MEM, D_MODEL), lambda b, i: (b, 0, 0))
    sq = pl.BlockSpec((D_MODEL, D_MODEL), lambda b, i: (0, 0), pipeline_mode=pl.Buffered(1))
    vec = _const_spec((1, D_MODEL))
    return pl.pallas_call(
        _mix_mem_kernel,
        out_shape=jax.ShapeDtypeStruct((bsz, s, D_MODEL), F32),
        grid=(bsz, s // tm),
        in_specs=[tok(D_LRU), tok(D_MLA_OUT), tok(D_MODEL), sq, vec, vec, sq, memspec, memspec,
                  sq, vec],
        out_specs=tok(D_MODEL),
        scratch_shapes=[pltpu.VMEM((tm, D_MODEL), BF16)],
        compiler_params=pltpu.CompilerParams(dimension_semantics=("parallel", "parallel"),
                                             vmem_limit_bytes=VMEM_LIMIT),
        name="mix_mem",
    )(ylru, ymla, x, wout, gpost, gpre, wmq, mk, mv, wmo, gpm)


def _ffn_kernel(h_ref, gpre_ref, wg_ref, wu_ref, wd_ref, gpost_ref, o_ref):
    subs = [slice(r, r + ROW_SUB) for r in range(0, h_ref.shape[0], ROW_SUB)]
    xn = [_rms(h_ref[r, :], gpre_ref[...]).astype(BF16) for r in subs]
    f = [None] * len(subs)
    for lo, hi in FF_CHUNKS:
        for n, v in enumerate(xn):
            act = (jax.nn.silu(_dot(v, wg_ref[:, lo:hi])) * _dot(v, wu_ref[:, lo:hi])).astype(BF16)
            part = _dot(act, wd_ref[lo:hi, :])
            f[n] = part if f[n] is None else f[n] + part
    for r, fr in zip(subs, f):
        o_ref[r, :] = h_ref[r, :] + _rms(fr, gpost_ref[...])


def _ffn(h, gpre, wg, wu, wd, gpost):
    m, _ = h.shape
    tm = TM_FFN
    single = pl.Buffered(1)
    tok = pl.BlockSpec((tm, D_MODEL), lambda i: (i, 0))
    vec = _const_spec((1, D_MODEL))
    return pl.pallas_call(
        _ffn_kernel,
        out_shape=jax.ShapeDtypeStruct((m, D_MODEL), F32),
        grid=(m // tm,),
        in_specs=[
            tok, vec,
            pl.BlockSpec((D_MODEL, D_FF), lambda i: (0, 0), pipeline_mode=single),
            pl.BlockSpec((D_MODEL, D_FF), lambda i: (0, 0), pipeline_mode=single),
            pl.BlockSpec((D_FF, D_MODEL), lambda i: (0, 0), pipeline_mode=single),
            vec,
        ],
        out_specs=tok,
        compiler_params=pltpu.CompilerParams(dimension_semantics=("parallel",),
                                             vmem_limit_bytes=VMEM_LIMIT),
        name="ffn",
    )(h, gpre, wg, wu, wd, gpost)


def _block_diag(w):
    eye = jnp.eye(LRU_BLOCKS, dtype=w.dtype)
    return jnp.einsum('hij,hg->higj', w, eye).reshape(D_LRU, D_LRU)


def _gate_weights(wa, wx):
    da, dx = _block_diag(wa), _block_diag(wx)
    groups = []
    for j in range(D_LRU // LRU_GROUP):
        sl = slice(j * LRU_GROUP, (j + 1) * LRU_GROUP)
        groups.append(jnp.concatenate([da[sl, sl], dx[sl, sl]], axis=1))
    return jnp.stack(groups).astype(BF16)


def _pad_in_proj(w_in, gain):
    w = gain[:, None] * w_in
    pad = jnp.zeros((D_MODEL, Z_WIDTH - w.shape[1]), w.dtype)
    return jnp.concatenate([w, pad], axis=1).astype(BF16)


def _q_weights_t(w_uq, gain):
    dh = QK_NOPE_DIM + QK_ROPE_DIM
    scale = math.log2(math.e) / math.sqrt(dh)
    w = (scale * gain[:, None] * w_uq).reshape(Q_LORA_RANK, MLA_HEADS, dh)
    pad = jnp.zeros((Q_LORA_RANK, MLA_HEADS, HEAD_SLAB - dh), w.dtype)
    w = jnp.concatenate([w, pad], axis=-1).reshape(Q_LORA_RANK, MLA_HEADS * HEAD_SLAB)
    return w.T.astype(BF16)


def _kv_weights(w_ukv, gain):
    w = (gain[:, None] * w_ukv).reshape(KV_LORA_RANK, MLA_HEADS, QK_NOPE_DIM + V_HEAD_DIM)
    kn = w[:, :, :QK_NOPE_DIM].reshape(KV_LORA_RANK, MLA_HEADS * QK_NOPE_DIM)
    vv = w[:, :, QK_NOPE_DIM:].reshape(KV_LORA_RANK, MLA_HEADS * V_HEAD_DIM)
    return kn.astype(BF16), vv.T.astype(BF16)


def _rope_freq_column():
    inv_freq = ROPE_THETA ** (-jnp.arange(0, QK_ROPE_DIM, 2, dtype=F32) / QK_ROPE_DIM)
    return inv_freq.reshape(QK_ROPE_DIM // 2, 1)


def kernel(x, mem, positions, g_pre_mix, w_in, conv_w, conv_b, lru_wa, lru_ba, lru_wx, lru_bx, lru_lambda, g_q_lat, w_uq, g_kv_lat, w_ukv, g_lru_out, g_mla_out, w_out, g_post_mix, g_pre_mem, g_mem_kv, w_mq, w_mk, w_mv, w_mo, g_post_mem, g_pre_ffn, w_gate, w_up, w_down, g_post_ffn):
    bsz, s, d = x.shape
    depth = w_in.shape[0]
    assert d == D_MODEL and s % TS_IN == 0 and s % TM_MIX == 0
    assert (bsz * s) % TM_FFN == 0 and mem.shape == (bsz, N_MEM, D_MODEL)
    pos_row = positions.reshape(bsz, 1, s)
    invf = _rope_freq_column()
    row = lambda a: a.reshape(1, -1)

    h = x
    for l in range(depth):
        mk, mv = _mem_kv(mem, row(g_mem_kv[l]), w_mk[l].astype(BF16), w_mv[l].astype(BF16))
        wkn, wvt = _kv_weights(w_ukv[l], g_kv_lat[l])
        qt, k, vt, ylru = _in_proj(
            h, pos_row, invf, _pad_in_proj(w_in[l], g_pre_mix[l]), conv_w[l], row(conv_b[l]),
            _gate_weights(lru_wa[l], lru_wx[l]), row(lru_ba[l]), row(lru_bx[l]), row(lru_lambda[l]),
            _q_weights_t(w_uq[l], g_q_lat[l]), wkn, wvt)
        ymla = _mla_attn(qt, k, vt)
        wout = (jnp.concatenate([g_lru_out[l], g_mla_out[l]])[:, None] * w_out[l]).astype(BF16)
        h = _mix_mem(ylru, ymla, h, wout, row(g_post_mix[l]), row(g_pre_mem[l]),
                     w_mq[l].astype(BF16), mk, mv, w_mo[l].astype(BF16), row(g_post_mem[l]))
        h = _ffn(h.reshape(bsz * s, d), row(g_pre_ffn[l]), w_gate[l].astype(BF16),
                 w_up[l].astype(BF16), w_down[l].astype(BF16), row(g_post_ffn[l])).reshape(bsz, s, d)
    return h
```

```python
import math

import jax
import jax.numpy as jnp
from jax import lax
from jax.experimental import pallas as pl
from jax.experimental.pallas import tpu as pltpu

F32 = jnp.float32
BF16 = jnp.bfloat16

D_MODEL = 1024
N_MEM = 256
MEM_HEADS = 4
MEM_HEAD_DIM = D_MODEL // MEM_HEADS
D_LRU = 512
LRU_BLOCKS = 8
LRU_BLOCK_DIM = D_LRU // LRU_BLOCKS
CONV_WIDTH = 4
LRU_C = 8.0
MLA_HEADS = 4
QK_NOPE_DIM = 128
QK_ROPE_DIM = 64
V_HEAD_DIM = 128
Q_LORA_RANK = 384
KV_LORA_RANK = 256
D_MLA_OUT = MLA_HEADS * V_HEAD_DIM
ROPE_THETA = 10000.0
D_FF = 2816
RMS_EPS = 1e-6
NEG_INF = -1e30

LANES = 128
SUBLANES = 8
HEAD_SLAB = 2 * LANES
BF16_ROWS = 16
VT_ROWS = V_HEAD_DIM + BF16_ROWS
Z_LRU_X = 0
Z_LRU_GATE = Z_LRU_X + D_LRU
Z_CQ = Z_LRU_GATE + D_LRU
Z_CKV = Z_CQ + Q_LORA_RANK
Z_KPE = Z_CKV + KV_LORA_RANK
Z_WIDTH = Z_KPE + LANES
MXU_TILE = 256
LRU_GROUP = MXU_TILE

TS_IN = 512
TM_MIX = 1024
SCORE_LOOKAHEAD = 2
ROW_SUB = 512
TM_FFN = 1024
FF_CHUNKS = ((0, 1024), (1024, 2048), (2048, D_FF))
VMEM_LIMIT = 56 * 1024 * 1024


def _rms(x, g):
    ms = jnp.mean(x * x, axis=-1, keepdims=True)
    return x * lax.rsqrt(ms + RMS_EPS) * g


def _rms_plain(x):
    ms = jnp.mean(x * x, axis=-1, keepdims=True)
    return x * lax.rsqrt(ms + RMS_EPS)


def _gelu_tanh(x):
    k1 = -2.0 * math.sqrt(2.0 / math.pi) * math.log2(math.e)
    k2 = k1 * 0.044715
    return x / (1.0 + jnp.exp2(x * (k1 + k2 * (x * x))))


def _dot(a, b):
    return jnp.dot(a, b, preferred_element_type=F32)


def _dot_nt(a, b):
    return lax.dot_general(a, b, (((1,), (1,)), ((), ())), preferred_element_type=F32)


def _const_spec(shape):
    return pl.BlockSpec(shape, lambda *_: (0,) * len(shape))


def _mem_kv_kernel(mem_ref, g_ref, wk_ref, wv_ref, mk_ref, mv_ref):
    mn = _rms(mem_ref[...], g_ref[...]).astype(BF16)
    mk_ref[...] = _dot(mn, wk_ref[...]).astype(BF16)
    mv_ref[...] = _dot(mn, wv_ref[...]).astype(BF16)


def _mem_kv(mem, g, wk, wv):
    bsz = mem.shape[0]
    blk = pl.BlockSpec((None, N_MEM, D_MODEL), lambda b: (b, 0, 0))
    return pl.pallas_call(
        _mem_kv_kernel,
        out_shape=(jax.ShapeDtypeStruct((bsz, N_MEM, D_MODEL), BF16),) * 2,
        grid=(bsz,),
        in_specs=[blk, _const_spec((1, D_MODEL)), _const_spec((D_MODEL, D_MODEL)),
                  _const_spec((D_MODEL, D_MODEL))],
        out_specs=(blk, blk),
        compiler_params=pltpu.CompilerParams(dimension_semantics=("parallel",),
                                             vmem_limit_bytes=VMEM_LIMIT),
        name="mem_kv",
    )(mem, g, wk, wv)


def _shift_rows(x, prev, d):
    r = pltpu.roll(x, d, 0)
    row = lax.broadcasted_iota(jnp.int32, prev.shape, 0)
    head = jnp.where(row < d, pltpu.roll(prev, d, 0), r[0:SUBLANES])
    return jnp.concatenate([head, r[SUBLANES:]], axis=0)


def _linear_scan(a, u, h0):
    t, c = a.shape
    groups = t // SUBLANES
    a = a.reshape(groups, SUBLANES, c)
    u = u.reshape(groups, SUBLANES, c)
    row = lax.broadcasted_iota(jnp.int32, a.shape, 1)
    for s in (1, 2, 4):
        a_s = pltpu.roll(a, s, 1)
        u_s = pltpu.roll(u, s, 1)
        m = row >= s
        u = jnp.where(m, a * u_s + u, u)
        a = jnp.where(m, a * a_s, a)
    hs = []
    h = h0
    for g in range(groups):
        hg = a[g] * h + u[g]
        hs.append(hg)
        h = hg[SUBLANES - 1:SUBLANES]
    return jnp.concatenate(hs, axis=0)


def _rope(t, cos, sin):
    half = QK_ROPE_DIM // 2
    return t * cos + pltpu.roll(t, half, 1) * sin - pltpu.roll(t, LANES - half, 1) * sin


def _in_proj_kernel(x_ref, pos_ref, invf_ref, win_ref, convw_ref, convb_ref, wg_ref,
                    ba_ref, bx_ref, lam_ref, wuqt_ref, wkn_ref, wvt_ref,
                    qt_ref, k_ref, vt_ref, ylru_ref, xprev_ref, hprev_ref):
    ts = x_ref.shape[0]

    @pl.when(pl.program_id(1) == 0)
    def _():
        xprev_ref[...] = jnp.zeros_like(xprev_ref)
        hprev_ref[...] = jnp.zeros_like(hprev_ref)

    xn = _rms_plain(x_ref[...]).astype(BF16)
    z_tiles = {}

    def z(lo, hi):
        parts = []
        for j in range(lo // MXU_TILE, -(-hi // MXU_TILE)):
            if j not in z_tiles:
                z_tiles[j] = _dot(xn, win_ref[:, j * MXU_TILE:(j + 1) * MXU_TILE])
            a, b = max(lo, j * MXU_TILE), min(hi, (j + 1) * MXU_TILE)
            parts.append(z_tiles[j][:, a - j * MXU_TILE:b - j * MXU_TILE])
        return parts[0] if len(parts) == 1 else jnp.concatenate(parts, axis=-1)

    xl = z(Z_LRU_X, Z_LRU_GATE)
    prev = xprev_ref[...]
    u = convb_ref[...] + convw_ref[CONV_WIDTH - 1:CONV_WIDTH, :] * xl
    for d in range(1, CONV_WIDTH):
        k = CONV_WIDTH - 1 - d
        u = u + convw_ref[k:k + 1, :] * _shift_rows(xl, prev, d)
    xprev_ref[...] = xl[ts - SUBLANES:ts]

    ub = u.astype(BF16)
    pre = [_dot(ub[:, j * LRU_GROUP:(j + 1) * LRU_GROUP], wg_ref[j])
           for j in range(D_LRU // LRU_GROUP)]

    z_gate = z(Z_LRU_GATE, Z_CQ)
    z_cq = z(Z_CQ, Z_CKV)
    z_ckv = z(Z_CKV, Z_KPE)
    z_kpe = z(Z_KPE, Z_WIDTH)
    cqn = _rms_plain(z_cq)
    ckvn = _rms_plain(z_ckv)
    q_t = _dot(wuqt_ref[...], cqn.T.astype(BF16))
    v_t = _dot(wvt_ref[...], ckvn.T.astype(BF16))
    kn = _dot(ckvn.astype(BF16), wkn_ref[...])

    r = jax.nn.sigmoid(jnp.concatenate([p[:, :LRU_GROUP] for p in pre], axis=-1) + ba_ref[...])
    gi = jax.nn.sigmoid(jnp.concatenate([p[:, LRU_GROUP:] for p in pre], axis=-1) + bx_ref[...])
    nlam = -lam_ref[...]
    softplus = jnp.maximum(nlam, 0.0) + jnp.log1p(jnp.exp(-jnp.abs(nlam)))
    a = jnp.exp2(r * ((-LRU_C * math.log2(math.e)) * softplus))
    one_m_a2 = 1.0 - a * a
    root = jnp.where(one_m_a2 > 0.0, one_m_a2 * lax.rsqrt(one_m_a2), 0.0)
    v_in = root * (gi * u)
    h = _linear_scan(a, v_in, hprev_ref[SUBLANES - 1:SUBLANES, :])
    hprev_ref[...] = h[ts - SUBLANES:ts]
    y = h * _gelu_tanh(z_gate)
    ylru_ref[...] = _rms_plain(y).astype(BF16)

    half = QK_ROPE_DIM // 2
    ang_t = invf_ref[...] * pos_ref[...].astype(F32)
    cos_t = jnp.cos(ang_t)
    sin_t = jnp.sin(ang_t)
    for hd in range(MLA_HEADS):
        r0 = hd * HEAD_SLAB
        pe = r0 + QK_NOPE_DIM
        x1 = q_t[pe:pe + half]
        x2 = q_t[pe + half:pe + 2 * half]
        qt_ref[r0:pe, :] = q_t[r0:pe].astype(BF16)
        qt_ref[pe:pe + half, :] = (x1 * cos_t - x2 * sin_t).astype(BF16)
        qt_ref[pe + half:pe + 2 * half, :] = (x2 * cos_t + x1 * sin_t).astype(BF16)
        qt_ref[pe + 2 * half:r0 + HEAD_SLAB, :] = jnp.zeros((HEAD_SLAB - QK_NOPE_DIM - 2 * half, ts), BF16)
        v0 = hd * VT_ROWS
        vt_ref[v0:v0 + V_HEAD_DIM, :] = v_t[hd * V_HEAD_DIM:(hd + 1) * V_HEAD_DIM].astype(BF16)
        vt_ref[v0 + V_HEAD_DIM:v0 + VT_ROWS, :] = jnp.ones((BF16_ROWS, ts), BF16)
    pad = jnp.zeros((LANES - 2 * half, ts), F32)
    cos = jnp.concatenate([cos_t, cos_t, pad], axis=0).T
    sin = jnp.concatenate([sin_t, sin_t, pad], axis=0).T
    kpe = _rope(z_kpe, cos, sin).astype(BF16)
    for hd in range(MLA_HEADS):
        c0 = hd * HEAD_SLAB
        k_ref[:, c0:c0 + LANES] = kn[:, hd * LANES:(hd + 1) * LANES].astype(BF16)
        k_ref[:, c0 + LANES:c0 + HEAD_SLAB] = kpe


def _in_proj(x, pos_row, invf, win, convw, convb, wg, ba, bx, lam, wuqt, wkn, wvt):
    bsz, s, _ = x.shape
    ts = TS_IN
    tok = lambda w: pl.BlockSpec((None, ts, w), lambda b, i: (b, i, 0))
    in_specs = [
        tok(D_MODEL), pl.BlockSpec((None, 1, ts), lambda b, i: (b, 0, i)),
        _const_spec((QK_ROPE_DIM // 2, 1)),
        _const_spec((D_MODEL, Z_WIDTH)), _const_spec((CONV_WIDTH, D_LRU)), _const_spec((1, D_LRU)),
        _const_spec((D_LRU // LRU_GROUP, LRU_GROUP, 2 * LRU_GROUP)),
        _const_spec((1, D_LRU)), _const_spec((1, D_LRU)), _const_spec((1, D_LRU)),
        _const_spec((MLA_HEADS * HEAD_SLAB, Q_LORA_RANK)),
        _const_spec((KV_LORA_RANK, MLA_HEADS * QK_NOPE_DIM)),
        _const_spec((D_MLA_OUT, KV_LORA_RANK)),
    ]
    tile_t = lambda w: pl.BlockSpec((None, None, w, ts), lambda b, i: (b, i, 0, 0))
    out_shape = (
        jax.ShapeDtypeStruct((bsz, s // ts, MLA_HEADS * HEAD_SLAB, ts), BF16),
        jax.ShapeDtypeStruct((bsz, s, MLA_HEADS * HEAD_SLAB), BF16),
        jax.ShapeDtypeStruct((bsz, s // ts, MLA_HEADS * VT_ROWS, ts), BF16),
        jax.ShapeDtypeStruct((bsz, s, D_LRU), BF16),
    )
    out_specs = (tile_t(MLA_HEADS * HEAD_SLAB), tok(MLA_HEADS * HEAD_SLAB),
                 tile_t(MLA_HEADS * VT_ROWS), tok(D_LRU))
    return pl.pallas_call(
        _in_proj_kernel,
        out_shape=out_shape,
        grid=(bsz, s // ts),
        in_specs=in_specs,
        out_specs=out_specs,
        scratch_shapes=[pltpu.VMEM((SUBLANES, D_LRU), F32), pltpu.VMEM((SUBLANES, D_LRU), F32)],
        compiler_params=pltpu.CompilerParams(dimension_semantics=("parallel", "arbitrary"),
                                             vmem_limit_bytes=VMEM_LIMIT),
        name="in_proj",
    )(x, pos_row, invf, win, convw, convb, wg, ba, bx, lam, wuqt, wkn, wvt)


def _mla_attn_kernel(qt_ref, k_ref, vt_ref, o_ref, acc_ref):
    tq = qt_ref.shape[1]
    nq = vt_ref.shape[0]
    i = pl.program_id(1)

    hq = tq // 2

    def scores(j, hd, diagonal):
        slab = slice(hd * HEAD_SLAB, (hd + 1) * HEAD_SLAB)
        if not diagonal:
            st = _dot(k_ref[j * tq:(j + 1) * tq, slab], qt_ref[slab, :])
            return [(0, 0, False, st[:, :hq]), (1, 0, False, st[:, hq:])]
        return [(0, 0, True, _dot(k_ref[j * tq:j * tq + hq, slab], qt_ref[slab, 0:hq])),
                (1, 0, True, _dot(k_ref[j * tq:(j + 1) * tq, slab], qt_ref[slab, hq:tq]))]

    def attend(n_tiles):
        units = [(j, hd, j == n_tiles - 1) for j in range(n_tiles) for hd in range(MLA_HEADS)]
        m = [[jnp.full((1, hq), NEG_INF, F32)] * 2 for _ in range(MLA_HEADS)]
        acc = [[None, None] for _ in range(MLA_HEADS)]
        ahead = [scores(*u) for u in units[:SCORE_LOOKAHEAD]]
        for n, (j, hd, _) in enumerate(units):
            pieces = ahead.pop(0)
            if n + SCORE_LOOKAHEAD < len(units):
                ahead.append(scores(*units[n + SCORE_LOOKAHEAD]))
            vrows = slice(hd * VT_ROWS, (hd + 1) * VT_ROWS)
            for half, k0, masked, st in pieces:
                nk = st.shape[0]
                if masked:
                    kid = lax.broadcasted_iota(jnp.int32, (nk, hq), 0) + k0
                    qid = lax.broadcasted_iota(jnp.int32, (nk, hq), 1) + half * hq
                    st = jnp.where(qid >= kid, st, NEG_INF)
                m_old = m[hd][half]
                m_new = jnp.maximum(m_old, jnp.max(st, axis=0, keepdims=True))
                p = jnp.exp2(st - m_new).astype(BF16)
                pv = _dot(vt_ref[j, vrows, k0:k0 + nk], p)
                old = acc[hd][half]
                acc[hd][half] = pv if old is None else jnp.exp2(m_old - m_new) * old + pv
                m[hd][half] = m_new
        for hd in range(MLA_HEADS):
            for half in range(2):
                acc_ref[hd * VT_ROWS:(hd + 1) * VT_ROWS, half * hq:(half + 1) * hq] = acc[hd][half]

    for n in range(nq):
        @pl.when(i == n)
        def _(n=n):
            attend(n + 1)

    outs = []
    for hd in range(MLA_HEADS):
        r0 = hd * VT_ROWS
        inv_l = 1.0 / acc_ref[r0 + V_HEAD_DIM:r0 + V_HEAD_DIM + 1, :]
        outs.append(acc_ref[r0:r0 + V_HEAD_DIM, :] * inv_l)
    out_t = jnp.concatenate(outs, axis=0)
    ms = jnp.mean(out_t * out_t, axis=0, keepdims=True)
    o_ref[...] = (out_t * lax.rsqrt(ms + RMS_EPS)).astype(BF16).T


def _mla_attn(qt, k, vt):
    bsz, nq, _, tq = qt.shape
    s = k.shape[1]
    return pl.pallas_call(
        _mla_attn_kernel,
        out_shape=jax.ShapeDtypeStruct((bsz, s, D_MLA_OUT), BF16),
        grid=(bsz, nq),
        in_specs=[
            pl.BlockSpec((None, None, MLA_HEADS * HEAD_SLAB, tq), lambda b, i: (b, i, 0, 0)),
            pl.BlockSpec((None, s, MLA_HEADS * HEAD_SLAB), lambda b, i: (b, 0, 0)),
            pl.BlockSpec((None, nq, MLA_HEADS * VT_ROWS, tq), lambda b, i: (b, 0, 0, 0)),
        ],
        out_specs=pl.BlockSpec((None, tq, D_MLA_OUT), lambda b, i: (b, i, 0)),
        scratch_shapes=[pltpu.VMEM((MLA_HEADS * VT_ROWS, tq), F32)],
        compiler_params=pltpu.CompilerParams(dimension_semantics=("parallel", "arbitrary"),
                                             vmem_limit_bytes=VMEM_LIMIT),
        name="mla_attn",
    )(qt, k, vt)


def _mix_mem_kernel(ylru_ref, ymla_ref, x_ref, wout_ref, gpost_ref, gpre_ref, wmq_ref, mk_ref,
                    mv_ref, wmo_ref, gpm_ref, h_ref, o_scr):
    subs = [slice(r, r + ROW_SUB) for r in range(0, x_ref.shape[0], ROW_SUB)]
    yo = [_dot(ylru_ref[r, :], wout_ref[0:D_LRU, :]) + _dot(ymla_ref[r, :], wout_ref[D_LRU:, :])
          for r in subs]
    h1 = [x_ref[r, :] + _rms(y, gpost_ref[...]) for r, y in zip(subs, yo)]

    xn = [_rms(h, gpre_ref[...]).astype(BF16) for h in h1]
    mq = [(_dot(v, wmq_ref[...]) * (1.0 / math.sqrt(MEM_HEAD_DIM))).astype(BF16) for v in xn]
    for r, q in zip(subs, mq):
        for hd in range(MEM_HEADS):
            sl = slice(hd * MEM_HEAD_DIM, (hd + 1) * MEM_HEAD_DIM)
            s = _dot_nt(q[:, sl], mk_ref[:, sl])
            m = jnp.max(s, axis=-1, keepdims=True)
            p = jnp.exp(s - m)
            l = jnp.sum(p, axis=-1, keepdims=True)
            o_scr[r, sl] = (_dot(p.astype(BF16), mv_ref[:, sl]) / l).astype(BF16)
    mo = [_dot(o_scr[r, :], wmo_ref[...]) for r in subs]
    for r, h, o in zip(subs, h1, mo):
        h_ref[r, :] = h + _rms(o, gpm_ref[...])


def _mix_mem(ylru, ymla, x, wout, gpost, gpre, wmq, mk, mv, wmo, gpm):
    bsz, s, _ = x.shape
    tm = TM_MIX
    tok = lambda w: pl.BlockSpec((None, tm, w), lambda b, i: (b, i, 0))
    memspec = pl.BlockSpec((None, N_MEM, D_MODEL), lambda b, i: (b, 0, 0))
    sq = pl.BlockSpec((D_MODEL, D_MODEL), lambda b, i: (0, 0), pipeline_mode=pl.Buffered(1))
    vec = _const_spec((1, D_MODEL))
    return pl.pallas_call(
        _mix_mem_kernel,
        out_shape=jax.ShapeDtypeStruct((bsz, s, D_MODEL), F32),
        grid=(bsz, s // tm),
        in_specs=[tok(D_LRU), tok(D_MLA_OUT), tok(D_MODEL), sq, vec, vec, sq, memspec, memspec,
                  sq, vec],
        out_specs=tok(D_MODEL),
        scratch_shapes=[pltpu.VMEM((tm, D_MODEL), BF16)],
        compiler_params=pltpu.CompilerParams(dimension_semantics=("parallel", "parallel"),
                                             vmem_limit_bytes=VMEM_LIMIT),
        name="mix_mem",
    )(ylru, ymla, x, wout, gpost, gpre, wmq, mk, mv, wmo, gpm)


def _ffn_kernel(h_ref, gpre_ref, wg_ref, wu_ref, wd_ref, gpost_ref, o_ref):
    subs = [slice(r, r + ROW_SUB) for r in range(0, h_ref.shape[0], ROW_SUB)]
    xn = [_rms(h_ref[r, :], gpre_ref[...]).astype(BF16) for r in subs]
    f = [None] * len(subs)
    for lo, hi in FF_CHUNKS:
        for n, v in enumerate(xn):
            act = (jax.nn.silu(_dot(v, wg_ref[:, lo:hi])) * _dot(v, wu_ref[:, lo:hi])).astype(BF16)
            part = _dot(act, wd_ref[lo:hi, :])
            f[n] = part if f[n] is None else f[n] + part
    for r, fr in zip(subs, f):
        o_ref[r, :] = h_ref[r, :] + _rms(fr, gpost_ref[...])


def _ffn(h, gpre, wg, wu, wd, gpost):
    m, _ = h.shape
    tm = TM_FFN
    single = pl.Buffered(1)
    tok = pl.BlockSpec((tm, D_MODEL), lambda i: (i, 0))
    vec = _const_spec((1, D_MODEL))
    return pl.pallas_call(
        _ffn_kernel,
        out_shape=jax.ShapeDtypeStruct((m, D_MODEL), F32),
        grid=(m // tm,),
        in_specs=[
            tok, vec,
            pl.BlockSpec((D_MODEL, D_FF), lambda i: (0, 0), pipeline_mode=single),
            pl.BlockSpec((D_MODEL, D_FF), lambda i: (0, 0), pipeline_mode=single),
            pl.BlockSpec((D_FF, D_MODEL), lambda i: (0, 0), pipeline_mode=single),
            vec,
        ],
        out_specs=tok,
        compiler_params=pltpu.CompilerParams(dimension_semantics=("parallel",),
                                             vmem_limit_bytes=VMEM_LIMIT),
        name="ffn",
    )(h, gpre, wg, wu, wd, gpost)


def _block_diag(w):
    eye = jnp.eye(LRU_BLOCKS, dtype=w.dtype)
    return jnp.einsum('hij,hg->higj', w, eye).reshape(D_LRU, D_LRU)


def _gate_weights(wa, wx):
    da, dx = _block_diag(wa), _block_diag(wx)
    groups = []
    for j in range(D_LRU // LRU_GROUP):
        sl = slice(j * LRU_GROUP, (j + 1) * LRU_GROUP)
        groups.append(jnp.concatenate([da[sl, sl], dx[sl, sl]], axis=1))
    return jnp.stack(groups).astype(BF16)


def _pad_in_proj(w_in, gain):
    w = gain[:, None] * w_in
    pad = jnp.zeros((D_MODEL, Z_WIDTH - w.shape[1]), w.dtype)
    return jnp.concatenate([w, pad], axis=1).astype(BF16)


def _q_weights_t(w_uq, gain):
    dh = QK_NOPE_DIM + QK_ROPE_DIM
    scale = math.log2(math.e) / math.sqrt(dh)
    w = (scale * gain[:, None] * w_uq).reshape(Q_LORA_RANK, MLA_HEADS, dh)
    pad = jnp.zeros((Q_LORA_RANK, MLA_HEADS, HEAD_SLAB - dh), w.dtype)
    w = jnp.concatenate([w, pad], axis=-1).reshape(Q_LORA_RANK, MLA_HEADS * HEAD_SLAB)
    return w.T.astype(BF16)


def _kv_weights(w_ukv, gain):
    w = (gain[:, None] * w_ukv).reshape(KV_LORA_RANK, MLA_HEADS, QK_NOPE_DIM + V_HEAD_DIM)
    kn = w[:, :, :QK_NOPE_DIM].reshape(KV_LORA_RANK, MLA_HEADS * QK_NOPE_DIM)
    vv = w[:, :, QK_NOPE_DIM:].reshape(KV_LORA_RANK, MLA_HEADS * V_HEAD_DIM)
    return kn.astype(BF16), vv.T.astype(BF16)


def _rope_freq_column():
    inv_freq = ROPE_THETA ** (-jnp.arange(0, QK_ROPE_DIM, 2, dtype=F32) / QK_ROPE_DIM)
    return inv_freq.reshape(QK_ROPE_DIM // 2, 1)


def kernel(x, mem, positions, g_pre_mix, w_in, conv_w, conv_b, lru_wa, lru_ba, lru_wx, lru_bx, lru_lambda, g_q_lat, w_uq, g_kv_lat, w_ukv, g_lru_out, g_mla_out, w_out, g_post_mix, g_pre_mem, g_mem_kv, w_mq, w_mk, w_mv, w_mo, g_post_mem, g_pre_ffn, w_gate, w_up, w_down, g_post_ffn):
    bsz, s, d = x.shape
    depth = w_in.shape[0]
    assert d == D_MODEL and s % TS_IN == 0 and s % TM_MIX == 0
    assert (bsz * s) % TM_FFN == 0 and mem.shape == (bsz, N_MEM, D_MODEL)
    pos_row = positions.reshape(bsz, 1, s)
    invf = _rope_freq_column()
    row = lambda a: a.reshape(1, -1)

    h = x
    for l in range(depth):
        mk, mv = _mem_kv(mem, row(g_mem_kv[l]), w_mk[l].astype(BF16), w_mv[l].astype(BF16))
        wkn, wvt = _kv_weights(w_ukv[l], g_kv_lat[l])
        qt, k, vt, ylru = _in_proj(
            h, pos_row, invf, _pad_in_proj(w_in[l], g_pre_mix[l]), conv_w[l], row(conv_b[l]),
            _gate_weights(lru_wa[l], lru_wx[l]), row(lru_ba[l]), row(lru_bx[l]), row(lru_lambda[l]),
            _q_weights_t(w_uq[l], g_q_lat[l]), wkn, wvt)
        ymla = _mla_attn(qt, k, vt)
        wout = (jnp.concatenate([g_lru_out[l], g_mla_out[l]])[:, None] * w_out[l]).astype(BF16)
        h = _mix_mem(ylru, ymla, h, wout, row(g_post_mix[l]), row(g_pre_mem[l]),
                     w_mq[l].astype(BF16), mk, mv, w_mo[l].astype(BF16), row(g_post_mem[l]))
        h = _ffn(h.reshape(bsz * s, d), row(g_pre_ffn[l]), w_gate[l].astype(BF16),
                 w_up[l].astype(BF16), w_down[l].astype(BF16), row(g_post_ffn[l])).reshape(bsz, s, d)
    return h
```

```python
import math

import jax
import jax.numpy as jnp
from jax import lax
from jax.experimental import pallas as pl
from jax.experimental.pallas import tpu as pltpu

F32 = jnp.float32
BF16 = jnp.bfloat16

D_MODEL = 1024
N_MEM = 256
MEM_HEADS = 4
MEM_HEAD_DIM = D_MODEL // MEM_HEADS
D_LRU = 512
LRU_BLOCKS = 8
LRU_BLOCK_DIM = D_LRU // LRU_BLOCKS
CONV_WIDTH = 4
LRU_C = 8.0
MLA_HEADS = 4
QK_NOPE_DIM = 128
QK_ROPE_DIM = 64
V_HEAD_DIM = 128
Q_LORA_RANK = 384
KV_LORA_RANK = 256
D_MLA_OUT = MLA_HEADS * V_HEAD_DIM
ROPE_THETA = 10000.0
D_FF = 2816
RMS_EPS = 1e-6
NEG_INF = -1e30

LANES = 128
SUBLANES = 8
HEAD_SLAB = 2 * LANES
BF16_ROWS = 16
VT_ROWS = V_HEAD_DIM + BF16_ROWS
Z_LRU_X = 0
Z_LRU_GATE = Z_LRU_X + D_LRU
Z_CQ = Z_LRU_GATE + D_LRU
Z_CKV = Z_CQ + Q_LORA_RANK
Z_KPE = Z_CKV + KV_LORA_RANK
Z_WIDTH = Z_KPE + LANES
MXU_TILE = 256
LRU_GROUP = MXU_TILE

TS_IN = 512
IN_TILES_PER_STEP = 2
TM_MIX = 1024
Q_TILES_PER_STEP = 4
SCORE_LOOKAHEAD = 2
ROW_SUB = 512
TM_FFN = 1024
FF_CHUNKS = ((0, 1024), (1024, 2048), (2048, D_FF))
VMEM_LIMIT = 56 * 1024 * 1024


def _rms(x, g):
    ms = jnp.mean(x * x, axis=-1, keepdims=True)
    return x * lax.rsqrt(ms + RMS_EPS) * g


def _rms_plain(x):
    ms = jnp.mean(x * x, axis=-1, keepdims=True)
    return x * lax.rsqrt(ms + RMS_EPS)


def _gelu_tanh(x):
    k1 = -2.0 * math.sqrt(2.0 / math.pi) * math.log2(math.e)
    k2 = k1 * 0.044715
    return x / (1.0 + jnp.exp2(x * (k1 + k2 * (x * x))))


def _dot(a, b):
    return jnp.dot(a, b, preferred_element_type=F32)


def _dot_nt(a, b):
    return lax.dot_general(a, b, (((1,), (1,)), ((), ())), preferred_element_type=F32)


def _const_spec(shape):
    return pl.BlockSpec(shape, lambda *_: (0,) * len(shape))


def _shift_rows(x, prev, d):
    r = pltpu.roll(x, d, 0)
    row = lax.broadcasted_iota(jnp.int32, prev.shape, 0)
    head = jnp.where(row < d, pltpu.roll(prev, d, 0), r[0:SUBLANES])
    return jnp.concatenate([head, r[SUBLANES:]], axis=0)


def _linear_scan(a, u, h0):
    t, c = a.shape
    groups = t // SUBLANES
    a = a.reshape(groups, SUBLANES, c)
    u = u.reshape(groups, SUBLANES, c)
    row = lax.broadcasted_iota(jnp.int32, a.shape, 1)
    for s in (1, 2, 4):
        a_s = pltpu.roll(a, s, 1)
        u_s = pltpu.roll(u, s, 1)
        m = row >= s
        u = jnp.where(m, a * u_s + u, u)
        a = jnp.where(m, a * a_s, a)
    hs = []
    h = h0
    for g in range(groups):
        hg = a[g] * h + u[g]
        hs.append(hg)
        h = hg[SUBLANES - 1:SUBLANES]
    return jnp.concatenate(hs, axis=0)


def _rope(t, cos, sin):
    half = QK_ROPE_DIM // 2
    return t * cos + pltpu.roll(t, half, 1) * sin - pltpu.roll(t, LANES - half, 1) * sin


def _in_proj_kernel(x_ref, pos_ref, invf_ref, win_ref, convw_ref, convb_ref, wg_ref,
                    ba_ref, bx_ref, lam_ref, wuqt_ref, wkn_ref, wvt_ref,
                    qt_ref, k_ref, vt_ref, ylru_ref, xprev_ref, hprev_ref):
    ts = TS_IN
    tiles = range(x_ref.shape[0] // ts)
    rows = [slice(n * ts, (n + 1) * ts) for n in tiles]
    half = QK_ROPE_DIM // 2

    @pl.when(pl.program_id(1) == 0)
    def _():
        xprev_ref[...] = jnp.zeros_like(xprev_ref)
        hprev_ref[...] = jnp.zeros_like(hprev_ref)

    xn = [_rms_plain(x_ref[r, :]).astype(BF16) for r in rows]
    z_tiles = [{} for _ in tiles]

    def z(n, lo, hi):
        parts = []
        for j in range(lo // MXU_TILE, -(-hi // MXU_TILE)):
            if j not in z_tiles[n]:
                z_tiles[n][j] = _dot(xn[n], win_ref[:, j * MXU_TILE:(j + 1) * MXU_TILE])
            a, b = max(lo, j * MXU_TILE), min(hi, (j + 1) * MXU_TILE)
            parts.append(z_tiles[n][j][:, a - j * MXU_TILE:b - j * MXU_TILE])
        return parts[0] if len(parts) == 1 else jnp.concatenate(parts, axis=-1)

    xl = [z(n, Z_LRU_X, Z_LRU_GATE) for n in tiles]
    u = []
    prev = xprev_ref[...]
    for n in tiles:
        un = convb_ref[...] + convw_ref[CONV_WIDTH - 1:CONV_WIDTH, :] * xl[n]
        for d in range(1, CONV_WIDTH):
            k = CONV_WIDTH - 1 - d
            un = un + convw_ref[k:k + 1, :] * _shift_rows(xl[n], prev, d)
        u.append(un)
        prev = xl[n][ts - SUBLANES:ts]
    xprev_ref[...] = prev
    pre = [[_dot(u[n][:, j * LRU_GROUP:(j + 1) * LRU_GROUP].astype(BF16), wg_ref[j])
            for j in range(D_LRU // LRU_GROUP)] for n in tiles]

    z_gate = [z(n, Z_LRU_GATE, Z_CQ) for n in tiles]
    z_cq = [z(n, Z_CQ, Z_CKV) for n in tiles]
    z_ckv = [z(n, Z_CKV, Z_KPE) for n in tiles]
    z_kpe = [z(n, Z_KPE, Z_WIDTH) for n in tiles]
    cqn = [_rms_plain(v) for v in z_cq]
    ckvn = [_rms_plain(v) for v in z_ckv]
    q_t = [_dot(wuqt_ref[...], v.T.astype(BF16)) for v in cqn]
    v_t = [_dot(wvt_ref[...], v.T.astype(BF16)) for v in ckvn]
    kn = [_dot(v.astype(BF16), wkn_ref[...]) for v in ckvn]

    nlam = -lam_ref[...]
    softplus = jnp.maximum(nlam, 0.0) + jnp.log1p(jnp.exp(-jnp.abs(nlam)))
    h_last = hprev_ref[SUBLANES - 1:SUBLANES, :]
    for n in tiles:
        r = 1.0 / (1.0 + jnp.exp2(
            jnp.concatenate([p[:, :LRU_GROUP] for p in pre[n]], axis=-1) + ba_ref[...]))
        gi = 1.0 / (1.0 + jnp.exp2(
            jnp.concatenate([p[:, LRU_GROUP:] for p in pre[n]], axis=-1) + bx_ref[...]))
        a = jnp.exp2(r * ((-LRU_C * math.log2(math.e)) * softplus))
        one_m_a2 = 1.0 - a * a
        root = jnp.where(one_m_a2 > 0.0, one_m_a2 * lax.rsqrt(one_m_a2), 0.0)
        h = _linear_scan(a, root * (gi * u[n]), h_last)
        h_last = h[ts - 1:ts]
        if n == tiles[-1]:
            hprev_ref[...] = h[ts - SUBLANES:ts]
        ylru_ref[rows[n], :] = _rms_plain(h * _gelu_tanh(z_gate[n])).astype(BF16)

    for n in tiles:
        ang_t = invf_ref[...] * pos_ref[:, rows[n]].astype(F32)
        cos_t = jnp.cos(ang_t)
        sin_t = jnp.sin(ang_t)
        for hd in range(MLA_HEADS):
            r0 = hd * HEAD_SLAB
            pe = r0 + QK_NOPE_DIM
            x1 = q_t[n][pe:pe + half]
            x2 = q_t[n][pe + half:pe + 2 * half]
            qt_ref[n, r0:pe, :] = q_t[n][r0:pe].astype(BF16)
            qt_ref[n, pe:pe + half, :] = (x1 * cos_t - x2 * sin_t).astype(BF16)
            qt_ref[n, pe + half:pe + 2 * half, :] = (x2 * cos_t + x1 * sin_t).astype(BF16)
            qt_ref[n, pe + 2 * half:r0 + HEAD_SLAB, :] = jnp.zeros(
                (HEAD_SLAB - QK_NOPE_DIM - 2 * half, ts), BF16)
            v0 = hd * VT_ROWS
            vt_ref[n, v0:v0 + V_HEAD_DIM, :] = v_t[n][hd * V_HEAD_DIM:(hd + 1) * V_HEAD_DIM].astype(BF16)
            vt_ref[n, v0 + V_HEAD_DIM:v0 + VT_ROWS, :] = jnp.ones((BF16_ROWS, ts), BF16)
        pad = jnp.zeros((LANES - 2 * half, ts), F32)
        cos = jnp.concatenate([cos_t, cos_t, pad], axis=0).T
        sin = jnp.concatenate([sin_t, sin_t, pad], axis=0).T
        kpe = _rope(z_kpe[n], cos, sin).astype(BF16)
        for hd in range(MLA_HEADS):
            c0 = hd * HEAD_SLAB
            k_ref[rows[n], c0:c0 + LANES] = kn[n][:, hd * LANES:(hd + 1) * LANES].astype(BF16)
            k_ref[rows[n], c0 + LANES:c0 + HEAD_SLAB] = kpe


def _in_proj(x, pos_row, invf, win, convw, convb, wg, ba, bx, lam, wuqt, wkn, wvt):
    bsz, s, _ = x.shape
    ts = TS_IN
    step = TS_IN * IN_TILES_PER_STEP
    tok = lambda w: pl.BlockSpec((None, step, w), lambda b, i: (b, i, 0))
    in_specs = [
        tok(D_MODEL), pl.BlockSpec((None, 1, step), lambda b, i: (b, 0, i)),
        _const_spec((QK_ROPE_DIM // 2, 1)),
        _const_spec((D_MODEL, Z_WIDTH)), _const_spec((CONV_WIDTH, D_LRU)), _const_spec((1, D_LRU)),
        _const_spec((D_LRU // LRU_GROUP, LRU_GROUP, 2 * LRU_GROUP)),
        _const_spec((1, D_LRU)), _const_spec((1, D_LRU)), _const_spec((1, D_LRU)),
        _const_spec((MLA_HEADS * HEAD_SLAB, Q_LORA_RANK)),
        _const_spec((KV_LORA_RANK, MLA_HEADS * QK_NOPE_DIM)),
        _const_spec((D_MLA_OUT, KV_LORA_RANK)),
    ]
    tile_t = lambda w: pl.BlockSpec((None, IN_TILES_PER_STEP, w, ts), lambda b, i: (b, i, 0, 0))
    out_shape = (
        jax.ShapeDtypeStruct((bsz, s // ts, MLA_HEADS * HEAD_SLAB, ts), BF16),
        jax.ShapeDtypeStruct((bsz, s, MLA_HEADS * HEAD_SLAB), BF16),
        jax.ShapeDtypeStruct((bsz, s // ts, MLA_HEADS * VT_ROWS, ts), BF16),
        jax.ShapeDtypeStruct((bsz, s, D_LRU), BF16),
    )
    out_specs = (tile_t(MLA_HEADS * HEAD_SLAB), tok(MLA_HEADS * HEAD_SLAB),
                 tile_t(MLA_HEADS * VT_ROWS), tok(D_LRU))
    return pl.pallas_call(
        _in_proj_kernel,
        out_shape=out_shape,
        grid=(bsz, s // step),
        in_specs=in_specs,
        out_specs=out_specs,
        scratch_shapes=[pltpu.VMEM((SUBLANES, D_LRU), F32), pltpu.VMEM((SUBLANES, D_LRU), F32)],
        compiler_params=pltpu.CompilerParams(dimension_semantics=("parallel", "arbitrary"),
                                             vmem_limit_bytes=VMEM_LIMIT),
        name="in_proj",
    )(x, pos_row, invf, win, convw, convb, wg, ba, bx, lam, wuqt, wkn, wvt)


def _mla_attn_kernel(qt_ref, k_ref, vt_ref, o_ref):
    n_local, _, tq = qt_ref.shape
    nq = vt_ref.shape[0]
    i = pl.program_id(1)
    hq = tq // 2

    def scores(ql, j, hd, diagonal):
        slab = slice(hd * HEAD_SLAB, (hd + 1) * HEAD_SLAB)
        if not diagonal:
            st = _dot(k_ref[j * tq:(j + 1) * tq, slab], qt_ref[ql, slab, :])
            return [(0, 0, False, st[:, :hq]), (1, 0, False, st[:, hq:])]
        return [(0, 0, True, _dot(k_ref[j * tq:j * tq + hq, slab], qt_ref[ql, slab, 0:hq])),
                (1, 0, True, _dot(k_ref[j * tq:(j + 1) * tq, slab], qt_ref[ql, slab, hq:tq]))]

    def finish(ql, acc):
        outs = []
        for hd in range(MLA_HEADS):
            a = jnp.concatenate(acc[hd], axis=1)
            outs.append(a[:V_HEAD_DIM] * (1.0 / a[V_HEAD_DIM:V_HEAD_DIM + 1]))
        out_t = jnp.concatenate(outs, axis=0)
        ms = jnp.mean(out_t * out_t, axis=0, keepdims=True)
        o_ref[ql * tq:(ql + 1) * tq, :] = (out_t * lax.rsqrt(ms + RMS_EPS)).astype(BF16).T

    def attend(first_tile):
        units = [(ql, j, hd, j == first_tile + ql)
                 for ql in range(n_local) for j in range(first_tile + ql + 1)
                 for hd in range(MLA_HEADS)]
        ahead = [scores(*u) for u in units[:SCORE_LOOKAHEAD]]
        m = acc = None
        for n, (ql, j, hd, diagonal) in enumerate(units):
            if j == 0 and hd == 0:
                m = [[jnp.full((1, hq), NEG_INF, F32)] * 2 for _ in range(MLA_HEADS)]
                acc = [[None, None] for _ in range(MLA_HEADS)]
            pieces = ahead.pop(0)
            if n + SCORE_LOOKAHEAD < len(units):
                ahead.append(scores(*units[n + SCORE_LOOKAHEAD]))
            vrows = slice(hd * VT_ROWS, (hd + 1) * VT_ROWS)
            for half, k0, masked, st in pieces:
                nk = st.shape[0]
                if masked:
                    kid = lax.broadcasted_iota(jnp.int32, (nk, hq), 0) + k0
                    qid = lax.broadcasted_iota(jnp.int32, (nk, hq), 1) + half * hq
                    st = jnp.where(qid >= kid, st, NEG_INF)
                m_old = m[hd][half]
                m_new = jnp.maximum(m_old, jnp.max(st, axis=0, keepdims=True))
                p = jnp.exp2(st - m_new).astype(BF16)
                pv = _dot(vt_ref[j, vrows, k0:k0 + nk], p)
                old = acc[hd][half]
                acc[hd][half] = pv if old is None else jnp.exp2(m_old - m_new) * old + pv
                m[hd][half] = m_new
            if diagonal and hd == MLA_HEADS - 1:
                finish(ql, acc)

    for n in range(nq // n_local):
        @pl.when(i == n)
        def _(n=n):
            attend(n * n_local)


def _mla_attn(qt, k, vt):
    bsz, nq, _, tq = qt.shape
    s = k.shape[1]
    per = Q_TILES_PER_STEP
    return pl.pallas_call(
        _mla_attn_kernel,
        out_shape=jax.ShapeDtypeStruct((bsz, s, D_MLA_OUT), BF16),
        grid=(bsz, nq // per),
        in_specs=[
            pl.BlockSpec((None, per, MLA_HEADS * HEAD_SLAB, tq), lambda b, i: (b, i, 0, 0)),
            pl.BlockSpec((None, s, MLA_HEADS * HEAD_SLAB), lambda b, i: (b, 0, 0)),
            pl.BlockSpec((None, nq, MLA_HEADS * VT_ROWS, tq), lambda b, i: (b, 0, 0, 0)),
        ],
        out_specs=pl.BlockSpec((None, per * tq, D_MLA_OUT), lambda b, i: (b, i, 0)),
        compiler_params=pltpu.CompilerParams(dimension_semantics=("parallel", "arbitrary"),
                                             vmem_limit_bytes=VMEM_LIMIT),
        name="mla_attn",
    )(qt, k, vt)


def _mix_mem_kernel(ylru_ref, ymla_ref, x_ref, mem_ref, wout_ref, gpost_ref, gpre_ref, wmq_ref,
                    gmem_ref, wmk_ref, wmv_ref, wmo_ref, gpm_ref, h_ref, o_scr, mk_ref, mv_ref):
    @pl.when(pl.program_id(1) == 0)
    def _():
        mn = _rms(mem_ref[...], gmem_ref[...]).astype(BF16)
        mk_ref[...] = _dot(mn, wmk_ref[...]).astype(BF16)
        mv_ref[...] = _dot(mn, wmv_ref[...]).astype(BF16)

    subs = [slice(r, r + ROW_SUB) for r in range(0, x_ref.shape[0], ROW_SUB)]
    yo = [_dot(ylru_ref[r, :], wout_ref[0:D_LRU, :]) + _dot(ymla_ref[r, :], wout_ref[D_LRU:, :])
          for r in subs]
    h1 = [x_ref[r, :] + _rms(y, gpost_ref[...]) for r, y in zip(subs, yo)]

    xn = [_rms(h, gpre_ref[...]).astype(BF16) for h in h1]
    mq = [(_dot(v, wmq_ref[...]) * (1.0 / math.sqrt(MEM_HEAD_DIM))).astype(BF16) for v in xn]
    for r, q in zip(subs, mq):
        for hd in range(MEM_HEADS):
            sl = slice(hd * MEM_HEAD_DIM, (hd + 1) * MEM_HEAD_DIM)
            s = _dot_nt(q[:, sl], mk_ref[:, sl])
            m = jnp.max(s, axis=-1, keepdims=True)
            p = jnp.exp(s - m)
            l = jnp.sum(p, axis=-1, keepdims=True)
            o_scr[r, sl] = (_dot(p.astype(BF16), mv_ref[:, sl]) / l).astype(BF16)
    mo = [_dot(o_scr[r, :], wmo_ref[...]) for r in subs]
    for r, h, o in zip(subs, h1, mo):
        h_ref[r, :] = h + _rms(o, gpm_ref[...])


def _mix_mem(ylru, ymla, x, mem, wout, gpost, gpre, wmq, gmem, wmk, wmv, wmo, gpm):
    bsz, s, _ = x.shape
    tm = TM_MIX
    tok = lambda w: pl.BlockSpec((None, tm, w), lambda b, i: (b, i, 0))
    memspec = pl.BlockSpec((None, N_MEM, D_MODEL), lambda b, i: (b, 0, 0))
    sq = pl.BlockSpec((D_MODEL, D_MODEL), lambda b, i: (0, 0), pipeline_mode=pl.Buffered(1))
    vec = _const_spec((1, D_MODEL))
    return pl.pallas_call(
        _mix_mem_kernel,
        out_shape=jax.ShapeDtypeStruct((bsz, s, D_MODEL), F32),
        grid=(bsz, s // tm),
        in_specs=[tok(D_LRU), tok(D_MLA_OUT), tok(D_MODEL), memspec, sq, vec, vec, sq, vec, sq, sq,
                  sq, vec],
        out_specs=tok(D_MODEL),
        scratch_shapes=[pltpu.VMEM((tm, D_MODEL), BF16), pltpu.VMEM((N_MEM, D_MODEL), BF16),
                        pltpu.VMEM((N_MEM, D_MODEL), BF16)],
        compiler_params=pltpu.CompilerParams(dimension_semantics=("parallel", "arbitrary"),
                                             vmem_limit_bytes=VMEM_LIMIT),
        name="mix_mem",
    )(ylru, ymla, x, mem, wout, gpost, gpre, wmq, gmem, wmk, wmv, wmo, gpm)


def _ffn_kernel(h_ref, gpre_ref, wg_ref, wu_ref, wd_ref, gpost_ref, o_ref):
    subs = [slice(r, r + ROW_SUB) for r in range(0, h_ref.shape[0], ROW_SUB)]
    xn = [_rms(h_ref[r, :], gpre_ref[...]).astype(BF16) for r in subs]
    f = [None] * len(subs)
    for lo, hi in FF_CHUNKS:
        for n, v in enumerate(xn):
            act = (jax.nn.silu(_dot(v, wg_ref[:, lo:hi])) * _dot(v, wu_ref[:, lo:hi])).astype(BF16)
            part = _dot(act, wd_ref[lo:hi, :])
            f[n] = part if f[n] is None else f[n] + part
    for r, fr in zip(subs, f):
        o_ref[r, :] = h_ref[r, :] + _rms(fr, gpost_ref[...])


def _ffn(h, gpre, wg, wu, wd, gpost):
    m, _ = h.shape
    tm = TM_FFN
    single = pl.Buffered(1)
    tok = pl.BlockSpec((tm, D_MODEL), lambda i: (i, 0))
    vec = _const_spec((1, D_MODEL))
    return pl.pallas_call(
        _ffn_kernel,
        out_shape=jax.ShapeDtypeStruct((m, D_MODEL), F32),
        grid=(m // tm,),
        in_specs=[
            tok, vec,
            pl.BlockSpec((D_MODEL, D_FF), lambda i: (0, 0), pipeline_mode=single),
            pl.BlockSpec((D_MODEL, D_FF), lambda i: (0, 0), pipeline_mode=single),
            pl.BlockSpec((D_FF, D_MODEL), lambda i: (0, 0), pipeline_mode=single),
            vec,
        ],
        out_specs=tok,
        compiler_params=pltpu.CompilerParams(dimension_semantics=("parallel",),
                                             vmem_limit_bytes=VMEM_LIMIT),
        name="ffn",
    )(h, gpre, wg, wu, wd, gpost)


def _block_diag(w):
    eye = jnp.eye(LRU_BLOCKS, dtype=w.dtype)
    return jnp.einsum('hij,hg->higj', w, eye).reshape(D_LRU, D_LRU)


def _gate_weights(wa, wx):
    da, dx = _block_diag(wa) * -math.log2(math.e), _block_diag(wx) * -math.log2(math.e)
    groups = []
    for j in range(D_LRU // LRU_GROUP):
        sl = slice(j * LRU_GROUP, (j + 1) * LRU_GROUP)
        groups.append(jnp.concatenate([da[sl, sl], dx[sl, sl]], axis=1))
    return jnp.stack(groups).astype(BF16)


def _pad_in_proj(w_in, gain):
    w = gain[:, None] * w_in
    pad = jnp.zeros((D_MODEL, Z_WIDTH - w.shape[1]), w.dtype)
    return jnp.concatenate([w, pad], axis=1).astype(BF16)


def _q_weights_t(w_uq, gain):
    dh = QK_NOPE_DIM + QK_ROPE_DIM
    scale = math.log2(math.e) / math.sqrt(dh)
    w = (scale * gain[:, None] * w_uq).reshape(Q_LORA_RANK, MLA_HEADS, dh)
    pad = jnp.zeros((Q_LORA_RANK, MLA_HEADS, HEAD_SLAB - dh), w.dtype)
    w = jnp.concatenate([w, pad], axis=-1).reshape(Q_LORA_RANK, MLA_HEADS * HEAD_SLAB)
    return w.T.astype(BF16)


def _kv_weights(w_ukv, gain):
    w = (gain[:, None] * w_ukv).reshape(KV_LORA_RANK, MLA_HEADS, QK_NOPE_DIM + V_HEAD_DIM)
    kn = w[:, :, :QK_NOPE_DIM].reshape(KV_LORA_RANK, MLA_HEADS * QK_NOPE_DIM)
    vv = w[:, :, QK_NOPE_DIM:].reshape(KV_LORA_RANK, MLA_HEADS * V_HEAD_DIM)
    return kn.astype(BF16), vv.T.astype(BF16)


def _rope_freq_column():
    inv_freq = ROPE_THETA ** (-jnp.arange(0, QK_ROPE_DIM, 2, dtype=F32) / QK_ROPE_DIM)
    return inv_freq.reshape(QK_ROPE_DIM // 2, 1)


def kernel(x, mem, positions, g_pre_mix, w_in, conv_w, conv_b, lru_wa, lru_ba, lru_wx, lru_bx, lru_lambda, g_q_lat, w_uq, g_kv_lat, w_ukv, g_lru_out, g_mla_out, w_out, g_post_mix, g_pre_mem, g_mem_kv, w_mq, w_mk, w_mv, w_mo, g_post_mem, g_pre_ffn, w_gate, w_up, w_down, g_post_ffn):
    bsz, s, d = x.shape
    depth = w_in.shape[0]
    assert d == D_MODEL and s % TS_IN == 0 and s % TM_MIX == 0
    assert (bsz * s) % TM_FFN == 0 and mem.shape == (bsz, N_MEM, D_MODEL)
    pos_row = positions.reshape(bsz, 1, s)
    invf = _rope_freq_column()
    row = lambda a: a.reshape(1, -1)

    h = x
    for l in range(depth):
        wkn, wvt = _kv_weights(w_ukv[l], g_kv_lat[l])
        neg_log2e = -math.log2(math.e)
        qt, k, vt, ylru = _in_proj(
            h, pos_row, invf, _pad_in_proj(w_in[l], g_pre_mix[l]), conv_w[l], row(conv_b[l]),
            _gate_weights(lru_wa[l], lru_wx[l]), row(neg_log2e * lru_ba[l]),
            row(neg_log2e * lru_bx[l]), row(lru_lambda[l]),
            _q_weights_t(w_uq[l], g_q_lat[l]), wkn, wvt)
        ymla = _mla_attn(qt, k, vt)
        wout = (jnp.concatenate([g_lru_out[l], g_mla_out[l]])[:, None] * w_out[l]).astype(BF16)
        h = _mix_mem(ylru, ymla, h, mem, wout, row(g_post_mix[l]), row(g_pre_mem[l]),
                     w_mq[l].astype(BF16), row(g_mem_kv[l]), w_mk[l].astype(BF16),
                     w_mv[l].astype(BF16), w_mo[l].astype(BF16), row(g_post_mem[l]))
        h = _ffn(h.reshape(bsz * s, d), row(g_pre_ffn[l]), w_gate[l].astype(BF16),
                 w_up[l].astype(BF16), w_down[l].astype(BF16), row(g_post_ffn[l])).reshape(bsz, s, d)
    return h
```

```python
import math

import jax
import jax.numpy as jnp
from jax import lax
from jax.experimental import pallas as pl
from jax.experimental.pallas import tpu as pltpu

F32 = jnp.float32
BF16 = jnp.bfloat16

D_MODEL = 1024
N_MEM = 256
MEM_HEADS = 4
MEM_HEAD_DIM = D_MODEL // MEM_HEADS
D_LRU = 512
LRU_BLOCKS = 8
LRU_BLOCK_DIM = D_LRU // LRU_BLOCKS
CONV_WIDTH = 4
LRU_C = 8.0
MLA_HEADS = 4
QK_NOPE_DIM = 128
QK_ROPE_DIM = 64
V_HEAD_DIM = 128
Q_LORA_RANK = 384
KV_LORA_RANK = 256
D_MLA_OUT = MLA_HEADS * V_HEAD_DIM
ROPE_THETA = 10000.0
D_FF = 2816
RMS_EPS = 1e-6
NEG_INF = -1e30

LANES = 128
SUBLANES = 8
HEAD_SLAB = 2 * LANES
BF16_ROWS = 16
VT_ROWS = V_HEAD_DIM + BF16_ROWS
Z_LRU_X = 0
Z_LRU_GATE = Z_LRU_X + D_LRU
Z_CQ = Z_LRU_GATE + D_LRU
Z_CKV = Z_CQ + Q_LORA_RANK
Z_KPE = Z_CKV + KV_LORA_RANK
Z_WIDTH = Z_KPE + LANES
MXU_TILE = 256
LRU_GROUP = MXU_TILE

TS_IN = 512
IN_TILES_PER_STEP = 2
TM_MIX = 1024
Q_TILES_PER_STEP = 4
SCORE_LOOKAHEAD = 2
ROW_SUB = 256
TM_FFN = 1024
FF_CHUNKS = ((0, 1024), (1024, 2048), (2048, D_FF))
VMEM_LIMIT = 56 * 1024 * 1024


def _rms(x, g):
    ms = jnp.mean(x * x, axis=-1, keepdims=True)
    return x * lax.rsqrt(ms + RMS_EPS) * g


def _rms_plain(x):
    ms = jnp.mean(x * x, axis=-1, keepdims=True)
    return x * lax.rsqrt(ms + RMS_EPS)


def _gelu_tanh(x):
    k1 = -2.0 * math.sqrt(2.0 / math.pi) * math.log2(math.e)
    k2 = k1 * 0.044715
    return x / (1.0 + jnp.exp2(x * (k1 + k2 * (x * x))))


def _dot(a, b):
    return jnp.dot(a, b, preferred_element_type=F32)


def _dot_nt(a, b):
    return lax.dot_general(a, b, (((1,), (1,)), ((), ())), preferred_element_type=F32)


def _const_spec(shape):
    return pl.BlockSpec(shape, lambda *_: (0,) * len(shape))


def _shift_rows(x, prev, d):
    r = pltpu.roll(x, d, 0)
    row = lax.broadcasted_iota(jnp.int32, prev.shape, 0)
    head = jnp.where(row < d, pltpu.roll(prev, d, 0), r[0:SUBLANES])
    return jnp.concatenate([head, r[SUBLANES:]], axis=0)


def _linear_scan(a, u, h0):
    t, c = a.shape
    groups = t // SUBLANES
    a = a.reshape(groups, SUBLANES, c)
    u = u.reshape(groups, SUBLANES, c)
    row = lax.broadcasted_iota(jnp.int32, a.shape, 1)
    for s in (1, 2, 4):
        a_s = pltpu.roll(a, s, 1)
        u_s = pltpu.roll(u, s, 1)
        m = row >= s
        u = jnp.where(m, a * u_s + u, u)
        a = jnp.where(m, a * a_s, a)
    hs = []
    h = h0
    for g in range(groups):
        hg = a[g] * h + u[g]
        hs.append(hg)
        h = hg[SUBLANES - 1:SUBLANES]
    return jnp.concatenate(hs, axis=0)


def _rope(t, cos, sin):
    half = QK_ROPE_DIM // 2
    return t * cos + pltpu.roll(t, half, 1) * sin - pltpu.roll(t, LANES - half, 1) * sin


def _in_proj_kernel(x_ref, pos_ref, invf_ref, win_ref, convw_ref, convb_ref, wg_ref,
                    ba_ref, bx_ref, lam_ref, wuqt_ref, wkn_ref, wvt_ref,
                    qt_ref, k_ref, vt_ref, ylru_ref, xprev_ref, hprev_ref):
    ts = TS_IN
    tiles = range(x_ref.shape[0] // ts)
    rows = [slice(n * ts, (n + 1) * ts) for n in tiles]
    half = QK_ROPE_DIM // 2

    @pl.when(pl.program_id(1) == 0)
    def _():
        xprev_ref[...] = jnp.zeros_like(xprev_ref)
        hprev_ref[...] = jnp.zeros_like(hprev_ref)

    xn = [_rms_plain(x_ref[r, :]).astype(BF16) for r in rows]
    z_tiles = [{} for _ in tiles]

    def z(n, lo, hi):
        parts = []
        for j in range(lo // MXU_TILE, -(-hi // MXU_TILE)):
            if j not in z_tiles[n]:
                z_tiles[n][j] = _dot(xn[n], win_ref[:, j * MXU_TILE:(j + 1) * MXU_TILE])
            a, b = max(lo, j * MXU_TILE), min(hi, (j + 1) * MXU_TILE)
            parts.append(z_tiles[n][j][:, a - j * MXU_TILE:b - j * MXU_TILE])
        return parts[0] if len(parts) == 1 else jnp.concatenate(parts, axis=-1)

    xl = [z(n, Z_LRU_X, Z_LRU_GATE) for n in tiles]
    u = []
    prev = xprev_ref[...]
    for n in tiles:
        un = convb_ref[...] + convw_ref[CONV_WIDTH - 1:CONV_WIDTH, :] * xl[n]
        for d in range(1, CONV_WIDTH):
            k = CONV_WIDTH - 1 - d
            un = un + convw_ref[k:k + 1, :] * _shift_rows(xl[n], prev, d)
        u.append(un)
        prev = xl[n][ts - SUBLANES:ts]
    xprev_ref[...] = prev
    pre = [[_dot(u[n][:, j * LRU_GROUP:(j + 1) * LRU_GROUP].astype(BF16), wg_ref[j])
            for j in range(D_LRU // LRU_GROUP)] for n in tiles]

    z_gate = [z(n, Z_LRU_GATE, Z_CQ) for n in tiles]
    z_cq = [z(n, Z_CQ, Z_CKV) for n in tiles]
    z_ckv = [z(n, Z_CKV, Z_KPE) for n in tiles]
    z_kpe = [z(n, Z_KPE, Z_WIDTH) for n in tiles]
    cqn = [_rms_plain(v) for v in z_cq]
    ckvn = [_rms_plain(v) for v in z_ckv]
    q_t = [_dot(wuqt_ref[...], v.T.astype(BF16)) for v in cqn]
    v_t = [_dot(wvt_ref[...], v.T.astype(BF16)) for v in ckvn]
    kn = [_dot(v.astype(BF16), wkn_ref[...]) for v in ckvn]

    nlam = -lam_ref[...]
    softplus = jnp.maximum(nlam, 0.0) + jnp.log1p(jnp.exp(-jnp.abs(nlam)))
    h_last = hprev_ref[SUBLANES - 1:SUBLANES, :]
    for n in tiles:
        r = 1.0 / (1.0 + jnp.exp2(
            jnp.concatenate([p[:, :LRU_GROUP] for p in pre[n]], axis=-1) + ba_ref[...]))
        gi = 1.0 / (1.0 + jnp.exp2(
            jnp.concatenate([p[:, LRU_GROUP:] for p in pre[n]], axis=-1) + bx_ref[...]))
        a = jnp.exp2(r * ((-LRU_C * math.log2(math.e)) * softplus))
        one_m_a2 = 1.0 - a * a
        root = jnp.where(one_m_a2 > 0.0, one_m_a2 * lax.rsqrt(one_m_a2), 0.0)
        h = _linear_scan(a, root * (gi * u[n]), h_last)
        h_last = h[ts - 1:ts]
        if n == tiles[-1]:
            hprev_ref[...] = h[ts - SUBLANES:ts]
        ylru_ref[rows[n], :] = _rms_plain(h * _gelu_tanh(z_gate[n])).astype(BF16)

    for n in tiles:
        ang_t = invf_ref[...] * pos_ref[:, rows[n]].astype(F32)
        cos_t = jnp.cos(ang_t)
        sin_t = jnp.sin(ang_t)
        for hd in range(MLA_HEADS):
            r0 = hd * HEAD_SLAB
            pe = r0 + QK_NOPE_DIM
            x1 = q_t[n][pe:pe + half]
            x2 = q_t[n][pe + half:pe + 2 * half]
            qt_ref[n, r0:pe, :] = q_t[n][r0:pe].astype(BF16)
            qt_ref[n, pe:pe + half, :] = (x1 * cos_t - x2 * sin_t).astype(BF16)
            qt_ref[n, pe + half:pe + 2 * half, :] = (x2 * cos_t + x1 * sin_t).astype(BF16)
            qt_ref[n, pe + 2 * half:r0 + HEAD_SLAB, :] = jnp.zeros(
                (HEAD_SLAB - QK_NOPE_DIM - 2 * half, ts), BF16)
            v0 = hd * VT_ROWS
            vt_ref[n, v0:v0 + V_HEAD_DIM, :] = v_t[n][hd * V_HEAD_DIM:(hd + 1) * V_HEAD_DIM].astype(BF16)
            vt_ref[n, v0 + V_HEAD_DIM:v0 + VT_ROWS, :] = jnp.ones((BF16_ROWS, ts), BF16)
        pad = jnp.zeros((LANES - 2 * half, ts), F32)
        cos = jnp.concatenate([cos_t, cos_t, pad], axis=0).T
        sin = jnp.concatenate([sin_t, sin_t, pad], axis=0).T
        kpe = _rope(z_kpe[n], cos, sin).astype(BF16)
        for hd in range(MLA_HEADS):
            c0 = hd * HEAD_SLAB
            k_ref[rows[n], c0:c0 + LANES] = kn[n][:, hd * LANES:(hd + 1) * LANES].astype(BF16)
            k_ref[rows[n], c0 + LANES:c0 + HEAD_SLAB] = kpe


def _in_proj(x, pos_row, invf, win, convw, convb, wg, ba, bx, lam, wuqt, wkn, wvt):
    bsz, s, _ = x.shape
    ts = TS_IN
    step = TS_IN * IN_TILES_PER_STEP
    tok = lambda w: pl.BlockSpec((None, step, w), lambda b, i: (b, i, 0))
    in_specs = [
        tok(D_MODEL), pl.BlockSpec((None, 1, step), lambda b, i: (b, 0, i)),
        _const_spec((QK_ROPE_DIM // 2, 1)),
        _const_spec((D_MODEL, Z_WIDTH)), _const_spec((CONV_WIDTH, D_LRU)), _const_spec((1, D_LRU)),
        _const_spec((D_LRU // LRU_GROUP, LRU_GROUP, 2 * LRU_GROUP)),
        _const_spec((1, D_LRU)), _const_spec((1, D_LRU)), _const_spec((1, D_LRU)),
        _const_spec((MLA_HEADS * HEAD_SLAB, Q_LORA_RANK)),
        _const_spec((KV_LORA_RANK, MLA_HEADS * QK_NOPE_DIM)),
        _const_spec((D_MLA_OUT, KV_LORA_RANK)),
    ]
    tile_t = lambda w: pl.BlockSpec((None, IN_TILES_PER_STEP, w, ts), lambda b, i: (b, i, 0, 0))
    out_shape = (
        jax.ShapeDtypeStruct((bsz, s // ts, MLA_HEADS * HEAD_SLAB, ts), BF16),
        jax.ShapeDtypeStruct((bsz, s, MLA_HEADS * HEAD_SLAB), BF16),
        jax.ShapeDtypeStruct((bsz, s // ts, MLA_HEADS * VT_ROWS, ts), BF16),
        jax.ShapeDtypeStruct((bsz, s, D_LRU), BF16),
    )
    out_specs = (tile_t(MLA_HEADS * HEAD_SLAB), tok(MLA_HEADS * HEAD_SLAB),
                 tile_t(MLA_HEADS * VT_ROWS), tok(D_LRU))
    return pl.pallas_call(
        _in_proj_kernel,
        out_shape=out_shape,
        grid=(bsz, s // step),
        in_specs=in_specs,
        out_specs=out_specs,
        scratch_shapes=[pltpu.VMEM((SUBLANES, D_LRU), F32), pltpu.VMEM((SUBLANES, D_LRU), F32)],
        compiler_params=pltpu.CompilerParams(dimension_semantics=("parallel", "arbitrary"),
                                             vmem_limit_bytes=VMEM_LIMIT),
        name="in_proj",
    )(x, pos_row, invf, win, convw, convb, wg, ba, bx, lam, wuqt, wkn, wvt)


def _mla_attn_kernel(qt_ref, k_ref, vt_ref, o_ref):
    n_local, _, tq = qt_ref.shape
    nq = vt_ref.shape[0]
    i = pl.program_id(1)
    hq = tq // 2

    def scores(ql, j, hd, diagonal):
        slab = slice(hd * HEAD_SLAB, (hd + 1) * HEAD_SLAB)
        if not diagonal:
            st = _dot(k_ref[j * tq:(j + 1) * tq, slab], qt_ref[ql, slab, :])
            return [(0, 0, False, st[:, :hq]), (1, 0, False, st[:, hq:])]
        return [(0, 0, True, _dot(k_ref[j * tq:j * tq + hq, slab], qt_ref[ql, slab, 0:hq])),
                (1, 0, True, _dot(k_ref[j * tq:(j + 1) * tq, slab], qt_ref[ql, slab, hq:tq]))]

    def finish(ql, acc):
        outs = []
        for hd in range(MLA_HEADS):
            a = jnp.concatenate(acc[hd], axis=1)
            outs.append(a[:V_HEAD_DIM] * (1.0 / a[V_HEAD_DIM:V_HEAD_DIM + 1]))
        out_t = jnp.concatenate(outs, axis=0)
        ms = jnp.mean(out_t * out_t, axis=0, keepdims=True)
        o_ref[ql * tq:(ql + 1) * tq, :] = (out_t * lax.rsqrt(ms + RMS_EPS)).astype(BF16).T

    def attend(first_tile):
        units = [(ql, j, hd, j == first_tile + ql)
                 for ql in range(n_local) for j in range(first_tile + ql + 1)
                 for hd in range(MLA_HEADS)]
        ahead = [scores(*u) for u in units[:SCORE_LOOKAHEAD]]
        m = acc = None
        for n, (ql, j, hd, diagonal) in enumerate(units):
            if j == 0 and hd == 0:
                m = [[jnp.full((1, hq), NEG_INF, F32)] * 2 for _ in range(MLA_HEADS)]
                acc = [[None, None] for _ in range(MLA_HEADS)]
            pieces = ahead.pop(0)
            if n + SCORE_LOOKAHEAD < len(units):
                ahead.append(scores(*units[n + SCORE_LOOKAHEAD]))
            vrows = slice(hd * VT_ROWS, (hd + 1) * VT_ROWS)
            for half, k0, masked, st in pieces:
                nk = st.shape[0]
                if masked:
                    kid = lax.broadcasted_iota(jnp.int32, (nk, hq), 0) + k0
                    qid = lax.broadcasted_iota(jnp.int32, (nk, hq), 1) + half * hq
                    st = jnp.where(qid >= kid, st, NEG_INF)
                m_old = m[hd][half]
                m_new = jnp.maximum(m_old, jnp.max(st, axis=0, keepdims=True))
                p = jnp.exp2(st - m_new).astype(BF16)
                pv = _dot(vt_ref[j, vrows, k0:k0 + nk], p)
                old = acc[hd][half]
                acc[hd][half] = pv if old is None else jnp.exp2(m_old - m_new) * old + pv
                m[hd][half] = m_new
            if diagonal and hd == MLA_HEADS - 1:
                finish(ql, acc)

    for n in range(nq // n_local):
        @pl.when(i == n)
        def _(n=n):
            attend(n * n_local)


def _mla_attn(qt, k, vt):
    bsz, nq, _, tq = qt.shape
    s = k.shape[1]
    per = Q_TILES_PER_STEP
    return pl.pallas_call(
        _mla_attn_kernel,
        out_shape=jax.ShapeDtypeStruct((bsz, s, D_MLA_OUT), BF16),
        grid=(bsz, nq // per),
        in_specs=[
            pl.BlockSpec((None, per, MLA_HEADS * HEAD_SLAB, tq), lambda b, i: (b, i, 0, 0)),
            pl.BlockSpec((None, s, MLA_HEADS * HEAD_SLAB), lambda b, i: (b, 0, 0)),
            pl.BlockSpec((None, nq, MLA_HEADS * VT_ROWS, tq), lambda b, i: (b, 0, 0, 0)),
        ],
        out_specs=pl.BlockSpec((None, per * tq, D_MLA_OUT), lambda b, i: (b, i, 0)),
        compiler_params=pltpu.CompilerParams(dimension_semantics=("parallel", "arbitrary"),
                                             vmem_limit_bytes=VMEM_LIMIT),
        name="mla_attn",
    )(qt, k, vt)


def _mix_mem_kernel(ylru_ref, ymla_ref, x_ref, mem_ref, wout_ref, gpost_ref, gpre_ref, wmq_ref,
                    gmem_ref, wmk_ref, wmv_ref, wmo_ref, gpm_ref, h_ref, o_scr, mk_ref, mv_ref):
    @pl.when(pl.program_id(1) == 0)
    def _():
        mn = _rms(mem_ref[...], gmem_ref[...]).astype(BF16)
        mk_ref[...] = _dot(mn, wmk_ref[...]).astype(BF16)
        mv_ref[...] = _dot(mn, wmv_ref[...]).astype(BF16)

    subs = [slice(r, r + ROW_SUB) for r in range(0, x_ref.shape[0], ROW_SUB)]
    yo = [_dot(ylru_ref[r, :], wout_ref[0:D_LRU, :]) + _dot(ymla_ref[r, :], wout_ref[D_LRU:, :])
          for r in subs]
    h1 = [x_ref[r, :] + _rms(y, gpost_ref[...]) for r, y in zip(subs, yo)]

    xn = [_rms(h, gpre_ref[...]).astype(BF16) for h in h1]
    mq = [(_dot(v, wmq_ref[...]) * (1.0 / math.sqrt(MEM_HEAD_DIM))).astype(BF16) for v in xn]
    for r, q in zip(subs, mq):
        for hd in range(MEM_HEADS):
            sl = slice(hd * MEM_HEAD_DIM, (hd + 1) * MEM_HEAD_DIM)
            s = _dot_nt(q[:, sl], mk_ref[:, sl])
            m = jnp.max(s, axis=-1, keepdims=True)
            p = jnp.exp(s - m)
            l = jnp.sum(p, axis=-1, keepdims=True)
            o_scr[r, sl] = (_dot(p.astype(BF16), mv_ref[:, sl]) / l).astype(BF16)
    mo = [_dot(o_scr[r, :], wmo_ref[...]) for r in subs]
    for r, h, o in zip(subs, h1, mo):
        h_ref[r, :] = h + _rms(o, gpm_ref[...])


def _mix_mem(ylru, ymla, x, mem, wout, gpost, gpre, wmq, gmem, wmk, wmv, wmo, gpm):
    bsz, s, _ = x.shape
    tm = TM_MIX
    tok = lambda w: pl.BlockSpec((None, tm, w), lambda b, i: (b, i, 0))
    memspec = pl.BlockSpec((None, N_MEM, D_MODEL), lambda b, i: (b, 0, 0))
    sq = pl.BlockSpec((D_MODEL, D_MODEL), lambda b, i: (0, 0), pipeline_mode=pl.Buffered(1))
    vec = _const_spec((1, D_MODEL))
    return pl.pallas_call(
        _mix_mem_kernel,
        out_shape=jax.ShapeDtypeStruct((bsz, s, D_MODEL), F32),
        grid=(bsz, s // tm),
        in_specs=[tok(D_LRU), tok(D_MLA_OUT), tok(D_MODEL), memspec, sq, vec, vec, sq, vec, sq, sq,
                  sq, vec],
        out_specs=tok(D_MODEL),
        scratch_shapes=[pltpu.VMEM((tm, D_MODEL), BF16), pltpu.VMEM((N_MEM, D_MODEL), BF16),
                        pltpu.VMEM((N_MEM, D_MODEL), BF16)],
        compiler_params=pltpu.CompilerParams(dimension_semantics=("parallel", "arbitrary"),
                                             vmem_limit_bytes=VMEM_LIMIT),
        name="mix_mem",
    )(ylru, ymla, x, mem, wout, gpost, gpre, wmq, gmem, wmk, wmv, wmo, gpm)


def _ffn_kernel(h_ref, gpre_ref, wg_ref, wu_ref, wd_ref, gpost_ref, o_ref):
    subs = [slice(r, r + ROW_SUB) for r in range(0, h_ref.shape[0], ROW_SUB)]
    xn = [_rms(h_ref[r, :], gpre_ref[...]).astype(BF16) for r in subs]
    f = [None] * len(subs)
    for lo, hi in FF_CHUNKS:
        for n, v in enumerate(xn):
            act = (jax.nn.silu(_dot(v, wg_ref[:, lo:hi])) * _dot(v, wu_ref[:, lo:hi])).astype(BF16)
            part = _dot(act, wd_ref[lo:hi, :])
            f[n] = part if f[n] is None else f[n] + part
    for r, fr in zip(subs, f):
        o_ref[r, :] = h_ref[r, :] + _rms(fr, gpost_ref[...])


def _ffn(h, gpre, wg, wu, wd, gpost):
    m, _ = h.shape
    tm = TM_FFN
    single = pl.Buffered(1)
    tok = pl.BlockSpec((tm, D_MODEL), lambda i: (i, 0))
    vec = _const_spec((1, D_MODEL))
    return pl.pallas_call(
        _ffn_kernel,
        out_shape=jax.ShapeDtypeStruct((m, D_MODEL), F32),
        grid=(m // tm,),
        in_specs=[
            tok, vec,
            pl.BlockSpec((D_MODEL, D_FF), lambda i: (0, 0), pipeline_mode=single),
            pl.BlockSpec((D_MODEL, D_FF), lambda i: (0, 0), pipeline_mode=single),
            pl.BlockSpec((D_FF, D_MODEL), lambda i: (0, 0), pipeline_mode=single),
            vec,
        ],
        out_specs=tok,
        compiler_params=pltpu.CompilerParams(dimension_semantics=("parallel",),
                                             vmem_limit_bytes=VMEM_LIMIT),
        name="ffn",
    )(h, gpre, wg, wu, wd, gpost)


def _block_diag(w):
    eye = jnp.eye(LRU_BLOCKS, dtype=w.dtype)
    return jnp.einsum('hij,hg->higj', w, eye).reshape(D_LRU, D_LRU)


def _gate_weights(wa, wx):
    da, dx = _block_diag(wa) * -math.log2(math.e), _block_diag(wx) * -math.log2(math.e)
    groups = []
    for j in range(D_LRU // LRU_GROUP):
        sl = slice(j * LRU_GROUP, (j + 1) * LRU_GROUP)
        groups.append(jnp.concatenate([da[sl, sl], dx[sl, sl]], axis=1))
    return jnp.stack(groups).astype(BF16)


def _pad_in_proj(w_in, gain):
    w = gain[:, None] * w_in
    pad = jnp.zeros((D_MODEL, Z_WIDTH - w.shape[1]), w.dtype)
    return jnp.concatenate([w, pad], axis=1).astype(BF16)


def _q_weights_t(w_uq, gain):
    dh = QK_NOPE_DIM + QK_ROPE_DIM
    scale = math.log2(math.e) / math.sqrt(dh)
    w = (scale * gain[:, None] * w_uq).reshape(Q_LORA_RANK, MLA_HEADS, dh)
    pad = jnp.zeros((Q_LORA_RANK, MLA_HEADS, HEAD_SLAB - dh), w.dtype)
    w = jnp.concatenate([w, pad], axis=-1).reshape(Q_LORA_RANK, MLA_HEADS * HEAD_SLAB)
    return w.T.astype(BF16)


def _kv_weights(w_ukv, gain):
    w = (gain[:, None] * w_ukv).reshape(KV_LORA_RANK, MLA_HEADS, QK_NOPE_DIM + V_HEAD_DIM)
    kn = w[:, :, :QK_NOPE_DIM].reshape(KV_LORA_RANK, MLA_HEADS * QK_NOPE_DIM)
    vv = w[:, :, QK_NOPE_DIM:].reshape(KV_LORA_RANK, MLA_HEADS * V_HEAD_DIM)
    return kn.astype(BF16), vv.T.astype(BF16)


def _rope_freq_column():
    inv_freq = ROPE_THETA ** (-jnp.arange(0, QK_ROPE_DIM, 2, dtype=F32) / QK_ROPE_DIM)
    return inv_freq.reshape(QK_ROPE_DIM // 2, 1)


def kernel(x, mem, positions, g_pre_mix, w_in, conv_w, conv_b, lru_wa, lru_ba, lru_wx, lru_bx, lru_lambda, g_q_lat, w_uq, g_kv_lat, w_ukv, g_lru_out, g_mla_out, w_out, g_post_mix, g_pre_mem, g_mem_kv, w_mq, w_mk, w_mv, w_mo, g_post_mem, g_pre_ffn, w_gate, w_up, w_down, g_post_ffn):
    bsz, s, d = x.shape
    depth = w_in.shape[0]
    assert d == D_MODEL and s % TS_IN == 0 and s % TM_MIX == 0
    assert (bsz * s) % TM_FFN == 0 and mem.shape == (bsz, N_MEM, D_MODEL)
    pos_row = positions.reshape(bsz, 1, s)
    invf = _rope_freq_column()
    row = lambda a: a.reshape(1, -1)

    h = x
    for l in range(depth):
        wkn, wvt = _kv_weights(w_ukv[l], g_kv_lat[l])
        neg_log2e = -math.log2(math.e)
        qt, k, vt, ylru = _in_proj(
            h, pos_row, invf, _pad_in_proj(w_in[l], g_pre_mix[l]), conv_w[l], row(conv_b[l]),
            _gate_weights(lru_wa[l], lru_wx[l]), row(neg_log2e * lru_ba[l]),
            row(neg_log2e * lru_bx[l]), row(lru_lambda[l]),
            _q_weights_t(w_uq[l], g_q_lat[l]), wkn, wvt)
        ymla = _mla_attn(qt, k, vt)
        wout = (jnp.concatenate([g_lru_out[l], g_mla_out[l]])[:, None] * w_out[l]).astype(BF16)
        h = _mix_mem(ylru, ymla, h, mem, wout, row(g_post_mix[l]), row(g_pre_mem[l]),
                     w_mq[l].astype(BF16), row(g_mem_kv[l]), w_mk[l].astype(BF16),
                     w_mv[l].astype(BF16), w_mo[l].astype(BF16), row(g_post_mem[l]))
        h = _ffn(h.reshape(bsz * s, d), row(g_pre_ffn[l]), w_gate[l].astype(BF16),
                 w_up[l].astype(BF16), w_down[l].astype(BF16), row(g_post_ffn[l])).reshape(bsz, s, d)
    return h
```

```python
import math

import jax
import jax.numpy as jnp
from jax import lax
from jax.experimental import pallas as pl
from jax.experimental.pallas import tpu as pltpu

F32 = jnp.float32
BF16 = jnp.bfloat16

D_MODEL = 1024
N_MEM = 256
MEM_HEADS = 4
MEM_HEAD_DIM = D_MODEL // MEM_HEADS
D_LRU = 512
LRU_BLOCKS = 8
LRU_BLOCK_DIM = D_LRU // LRU_BLOCKS
CONV_WIDTH = 4
LRU_C = 8.0
MLA_HEADS = 4
QK_NOPE_DIM = 128
QK_ROPE_DIM = 64
V_HEAD_DIM = 128
Q_LORA_RANK = 384
KV_LORA_RANK = 256
D_MLA_OUT = MLA_HEADS * V_HEAD_DIM
ROPE_THETA = 10000.0
D_FF = 2816
RMS_EPS = 1e-6
NEG_INF = -1e30

LANES = 128
SUBLANES = 8
HEAD_SLAB = 2 * LANES
BF16_ROWS = 16
VT_ROWS = V_HEAD_DIM + BF16_ROWS
Z_LRU_X = 0
Z_LRU_GATE = Z_LRU_X + D_LRU
Z_CQ = Z_LRU_GATE + D_LRU
Z_CKV = Z_CQ + Q_LORA_RANK
Z_KPE = Z_CKV + KV_LORA_RANK
Z_WIDTH = Z_KPE + LANES
MXU_TILE = 256
LRU_GROUP = MXU_TILE

TS_IN = 512
IN_TILES_PER_STEP = 2
IN_SUB = 512
TM_MIX = 1024
Q_TILES_PER_STEP = 4
SCORE_LOOKAHEAD = 2
ROW_SUB = 256
TM_FFN = 1024
FF_CHUNKS = ((0, 1024), (1024, 2048), (2048, D_FF))
VMEM_LIMIT = 56 * 1024 * 1024


def _rms(x, g):
    ms = jnp.mean(x * x, axis=-1, keepdims=True)
    return x * lax.rsqrt(ms + RMS_EPS) * g


def _rms_plain(x):
    ms = jnp.mean(x * x, axis=-1, keepdims=True)
    return x * lax.rsqrt(ms + RMS_EPS)


def _gelu_tanh(x):
    k1 = -2.0 * math.sqrt(2.0 / math.pi) * math.log2(math.e)
    k2 = k1 * 0.044715
    return x / (1.0 + jnp.exp2(x * (k1 + k2 * (x * x))))


def _dot(a, b):
    return jnp.dot(a, b, preferred_element_type=F32)


def _dot_nt(a, b):
    return lax.dot_general(a, b, (((1,), (1,)), ((), ())), preferred_element_type=F32)


def _const_spec(shape):
    return pl.BlockSpec(shape, lambda *_: (0,) * len(shape))


def _shift_rows(x, prev, d):
    r = pltpu.roll(x, d, 0)
    row = lax.broadcasted_iota(jnp.int32, prev.shape, 0)
    head = jnp.where(row < d, pltpu.roll(prev, d, 0), r[0:SUBLANES])
    return jnp.concatenate([head, r[SUBLANES:]], axis=0)


def _linear_scan(a, u, h0):
    t, c = a.shape
    groups = t // SUBLANES
    a = a.reshape(groups, SUBLANES, c)
    u = u.reshape(groups, SUBLANES, c)
    row = lax.broadcasted_iota(jnp.int32, a.shape, 1)
    for s in (1, 2, 4):
        a_s = pltpu.roll(a, s, 1)
        u_s = pltpu.roll(u, s, 1)
        m = row >= s
        u = jnp.where(m, a * u_s + u, u)
        a = jnp.where(m, a * a_s, a)
    hs = []
    h = h0
    for g in range(groups):
        hg = a[g] * h + u[g]
        hs.append(hg)
        h = hg[SUBLANES - 1:SUBLANES]
    return jnp.concatenate(hs, axis=0)


def _rope(t, cos, sin):
    half = QK_ROPE_DIM // 2
    return t * cos + pltpu.roll(t, half, 1) * sin - pltpu.roll(t, LANES - half, 1) * sin


def _in_proj_kernel(x_ref, pos_ref, invf_ref, win_ref, convw_ref, convb_ref, wg_ref,
                    ba_ref, bx_ref, lam_ref, wuqt_ref, wkn_ref, wvt_ref,
                    qt_ref, k_ref, vt_ref, ylru_ref, xprev_ref, hprev_ref):
    ts = IN_SUB
    tiles = range(x_ref.shape[0] // ts)
    rows = [slice(n * ts, (n + 1) * ts) for n in tiles]
    half = QK_ROPE_DIM // 2

    @pl.when(pl.program_id(1) == 0)
    def _():
        xprev_ref[...] = jnp.zeros_like(xprev_ref)
        hprev_ref[...] = jnp.zeros_like(hprev_ref)

    xn = [_rms_plain(x_ref[r, :]).astype(BF16) for r in rows]
    z_tiles = [{} for _ in tiles]

    def z(n, lo, hi):
        parts = []
        for j in range(lo // MXU_TILE, -(-hi // MXU_TILE)):
            if j not in z_tiles[n]:
                z_tiles[n][j] = _dot(xn[n], win_ref[:, j * MXU_TILE:(j + 1) * MXU_TILE])
            a, b = max(lo, j * MXU_TILE), min(hi, (j + 1) * MXU_TILE)
            parts.append(z_tiles[n][j][:, a - j * MXU_TILE:b - j * MXU_TILE])
        return parts[0] if len(parts) == 1 else jnp.concatenate(parts, axis=-1)

    xl = [z(n, Z_LRU_X, Z_LRU_GATE) for n in tiles]
    u = []
    prev = xprev_ref[...]
    for n in tiles:
        un = convb_ref[...] + convw_ref[CONV_WIDTH - 1:CONV_WIDTH, :] * xl[n]
        for d in range(1, CONV_WIDTH):
            k = CONV_WIDTH - 1 - d
            un = un + convw_ref[k:k + 1, :] * _shift_rows(xl[n], prev, d)
        u.append(un)
        prev = xl[n][ts - SUBLANES:ts]
    xprev_ref[...] = prev
    pre = [[_dot(u[n][:, j * LRU_GROUP:(j + 1) * LRU_GROUP].astype(BF16), wg_ref[j])
            for j in range(D_LRU // LRU_GROUP)] for n in tiles]

    z_gate = [z(n, Z_LRU_GATE, Z_CQ) for n in tiles]
    z_cq = [z(n, Z_CQ, Z_CKV) for n in tiles]
    z_ckv = [z(n, Z_CKV, Z_KPE) for n in tiles]
    z_kpe = [z(n, Z_KPE, Z_WIDTH) for n in tiles]
    cqn = [_rms_plain(v) for v in z_cq]
    ckvn = [_rms_plain(v) for v in z_ckv]
    q_t = [_dot(wuqt_ref[...], v.T.astype(BF16)) for v in cqn]
    v_t = [_dot(wvt_ref[...], v.T.astype(BF16)) for v in ckvn]
    kn = [_dot(v.astype(BF16), wkn_ref[...]) for v in ckvn]

    nlam = -lam_ref[...]
    softplus = jnp.maximum(nlam, 0.0) + jnp.log1p(jnp.exp(-jnp.abs(nlam)))
    h_last = hprev_ref[SUBLANES - 1:SUBLANES, :]
    for n in tiles:
        r = 1.0 / (1.0 + jnp.exp2(
            jnp.concatenate([p[:, :LRU_GROUP] for p in pre[n]], axis=-1) + ba_ref[...]))
        gi = 1.0 / (1.0 + jnp.exp2(
            jnp.concatenate([p[:, LRU_GROUP:] for p in pre[n]], axis=-1) + bx_ref[...]))
        a = jnp.exp2(r * ((-LRU_C * math.log2(math.e)) * softplus))
        one_m_a2 = 1.0 - a * a
        root = jnp.where(one_m_a2 > 0.0, one_m_a2 * lax.rsqrt(one_m_a2), 0.0)
        h = _linear_scan(a, root * (gi * u[n]), h_last)
        h_last = h[ts - 1:ts]
        if n == tiles[-1]:
            hprev_ref[...] = h[ts - SUBLANES:ts]
        ylru_ref[rows[n], :] = _rms_plain(h * _gelu_tanh(z_gate[n])).astype(BF16)

    for n in tiles:
        ang_t = invf_ref[...] * pos_ref[:, rows[n]].astype(F32)
        cos_t = jnp.cos(ang_t)
        sin_t = jnp.sin(ang_t)
        t_out = n * ts // TS_IN
        cols = slice(n * ts % TS_IN, n * ts % TS_IN + ts)
        for hd in range(MLA_HEADS):
            r0 = hd * HEAD_SLAB
            pe = r0 + QK_NOPE_DIM
            x1 = q_t[n][pe:pe + half]
            x2 = q_t[n][pe + half:pe + 2 * half]
            qt_ref[t_out, r0:pe, cols] = q_t[n][r0:pe].astype(BF16)
            qt_ref[t_out, pe:pe + half, cols] = (x1 * cos_t - x2 * sin_t).astype(BF16)
            qt_ref[t_out, pe + half:pe + 2 * half, cols] = (x2 * cos_t + x1 * sin_t).astype(BF16)
            qt_ref[t_out, pe + 2 * half:r0 + HEAD_SLAB, cols] = jnp.zeros(
                (HEAD_SLAB - QK_NOPE_DIM - 2 * half, ts), BF16)
            v0 = hd * VT_ROWS
            vt_ref[t_out, v0:v0 + V_HEAD_DIM, cols] = (
                v_t[n][hd * V_HEAD_DIM:(hd + 1) * V_HEAD_DIM].astype(BF16))
            vt_ref[t_out, v0 + V_HEAD_DIM:v0 + VT_ROWS, cols] = jnp.ones((BF16_ROWS, ts), BF16)
        pad = jnp.zeros((LANES - 2 * half, ts), F32)
        cos = jnp.concatenate([cos_t, cos_t, pad], axis=0).T
        sin = jnp.concatenate([sin_t, sin_t, pad], axis=0).T
        kpe = _rope(z_kpe[n], cos, sin).astype(BF16)
        for hd in range(MLA_HEADS):
            c0 = hd * HEAD_SLAB
            k_ref[rows[n], c0:c0 + LANES] = kn[n][:, hd * LANES:(hd + 1) * LANES].astype(BF16)
            k_ref[rows[n], c0 + LANES:c0 + HEAD_SLAB] = kpe


def _in_proj(x, pos_row, invf, win, convw, convb, wg, ba, bx, lam, wuqt, wkn, wvt):
    bsz, s, _ = x.shape
    ts = TS_IN
    step = TS_IN * IN_TILES_PER_STEP
    tok = lambda w: pl.BlockSpec((None, step, w), lambda b, i: (b, i, 0))
    in_specs = [
        tok(D_MODEL), pl.BlockSpec((None, 1, step), lambda b, i: (b, 0, i)),
        _const_spec((QK_ROPE_DIM // 2, 1)),
        _const_spec((D_MODEL, Z_WIDTH)), _const_spec((CONV_WIDTH, D_LRU)), _const_spec((1, D_LRU)),
        _const_spec((D_LRU // LRU_GROUP, LRU_GROUP, 2 * LRU_GROUP)),
        _const_spec((1, D_LRU)), _const_spec((1, D_LRU)), _const_spec((1, D_LRU)),
        _const_spec((MLA_HEADS * HEAD_SLAB, Q_LORA_RANK)),
        _const_spec((KV_LORA_RANK, MLA_HEADS * QK_NOPE_DIM)),
        _const_spec((D_MLA_OUT, KV_LORA_RANK)),
    ]
    tile_t = lambda w: pl.BlockSpec((None, IN_TILES_PER_STEP, w, ts), lambda b, i: (b, i, 0, 0))
    out_shape = (
        jax.ShapeDtypeStruct((bsz, s // ts, MLA_HEADS * HEAD_SLAB, ts), BF16),
        jax.ShapeDtypeStruct((bsz, s, MLA_HEADS * HEAD_SLAB), BF16),
        jax.ShapeDtypeStruct((bsz, s // ts, MLA_HEADS * VT_ROWS, ts), BF16),
        jax.ShapeDtypeStruct((bsz, s, D_LRU), BF16),
    )
    out_specs = (tile_t(MLA_HEADS * HEAD_SLAB), tok(MLA_HEADS * HEAD_SLAB),
                 tile_t(MLA_HEADS * VT_ROWS), tok(D_LRU))
    return pl.pallas_call(
        _in_proj_kernel,
        out_shape=out_shape,
        grid=(bsz, s // step),
        in_specs=in_specs,
        out_specs=out_specs,
        scratch_shapes=[pltpu.VMEM((SUBLANES, D_LRU), F32), pltpu.VMEM((SUBLANES, D_LRU), F32)],
        compiler_params=pltpu.CompilerParams(dimension_semantics=("parallel", "arbitrary"),
                                             vmem_limit_bytes=VMEM_LIMIT),
        name="in_proj",
    )(x, pos_row, invf, win, convw, convb, wg, ba, bx, lam, wuqt, wkn, wvt)


def _mla_attn_kernel(qt_ref, k_ref, vt_ref, o_ref):
    n_local, _, tq = qt_ref.shape
    nq = vt_ref.shape[0]
    i = pl.program_id(1)
    hq = tq // 2

    def scores(ql, j, hd, diagonal):
        slab = slice(hd * HEAD_SLAB, (hd + 1) * HEAD_SLAB)
        if not diagonal:
            st = _dot(k_ref[j * tq:(j + 1) * tq, slab], qt_ref[ql, slab, :])
            return [(0, 0, False, st[:, :hq]), (1, 0, False, st[:, hq:])]
        return [(0, 0, True, _dot(k_ref[j * tq:j * tq + hq, slab], qt_ref[ql, slab, 0:hq])),
                (1, 0, True, _dot(k_ref[j * tq:(j + 1) * tq, slab], qt_ref[ql, slab, hq:tq]))]

    def finish(ql, acc):
        outs = []
        for hd in range(MLA_HEADS):
            a = jnp.concatenate(acc[hd], axis=1)
            outs.append(a[:V_HEAD_DIM] * (1.0 / a[V_HEAD_DIM:V_HEAD_DIM + 1]))
        out_t = jnp.concatenate(outs, axis=0)
        ms = jnp.mean(out_t * out_t, axis=0, keepdims=True)
        o_ref[ql * tq:(ql + 1) * tq, :] = (out_t * lax.rsqrt(ms + RMS_EPS)).astype(BF16).T

    def attend(first_tile):
        units = [(ql, j, hd, j == first_tile + ql)
                 for ql in range(n_local) for j in range(first_tile + ql + 1)
                 for hd in range(MLA_HEADS)]
        ahead = [scores(*u) for u in units[:SCORE_LOOKAHEAD]]
        m = acc = None
        for n, (ql, j, hd, diagonal) in enumerate(units):
            if j == 0 and hd == 0:
                m = [[jnp.full((1, hq), NEG_INF, F32)] * 2 for _ in range(MLA_HEADS)]
                acc = [[None, None] for _ in range(MLA_HEADS)]
            pieces = ahead.pop(0)
            if n + SCORE_LOOKAHEAD < len(units):
                ahead.append(scores(*units[n + SCORE_LOOKAHEAD]))
            vrows = slice(hd * VT_ROWS, (hd + 1) * VT_ROWS)
            for half, k0, masked, st in pieces:
                nk = st.shape[0]
                if masked:
                    kid = lax.broadcasted_iota(jnp.int32, (nk, hq), 0) + k0
                    qid = lax.broadcasted_iota(jnp.int32, (nk, hq), 1) + half * hq
                    st = jnp.where(qid >= kid, st, NEG_INF)
                m_old = m[hd][half]
                m_new = jnp.maximum(m_old, jnp.max(st, axis=0, keepdims=True))
                p = jnp.exp2(st - m_new).astype(BF16)
                pv = _dot(vt_ref[j, vrows, k0:k0 + nk], p)
                old = acc[hd][half]
                acc[hd][half] = pv if old is None else jnp.exp2(m_old - m_new) * old + pv
                m[hd][half] = m_new
            if diagonal and hd == MLA_HEADS - 1:
                finish(ql, acc)

    for n in range(nq // n_local):
        @pl.when(i == n)
        def _(n=n):
            attend(n * n_local)


def _mla_attn(qt, k, vt):
    bsz, nq, _, tq = qt.shape
    s = k.shape[1]
    per = Q_TILES_PER_STEP
    return pl.pallas_call(
        _mla_attn_kernel,
        out_shape=jax.ShapeDtypeStruct((bsz, s, D_MLA_OUT), BF16),
        grid=(bsz, nq // per),
        in_specs=[
            pl.BlockSpec((None, per, MLA_HEADS * HEAD_SLAB, tq), lambda b, i: (b, i, 0, 0)),
            pl.BlockSpec((None, s, MLA_HEADS * HEAD_SLAB), lambda b, i: (b, 0, 0)),
            pl.BlockSpec((None, nq, MLA_HEADS * VT_ROWS, tq), lambda b, i: (b, 0, 0, 0)),
        ],
        out_specs=pl.BlockSpec((None, per * tq, D_MLA_OUT), lambda b, i: (b, i, 0)),
        compiler_params=pltpu.CompilerParams(dimension_semantics=("parallel", "arbitrary"),
                                             vmem_limit_bytes=VMEM_LIMIT),
        name="mla_attn",
    )(qt, k, vt)


def _mix_mem_kernel(ylru_ref, ymla_ref, x_ref, mem_ref, wout_ref, gpost_ref, gpre_ref, wmq_ref,
                    gmem_ref, wmk_ref, wmv_ref, wmo_ref, gpm_ref, h_ref, o_scr, mk_ref, mv_ref):
    first = pl.program_id(1) == 0
    for build_mem in (True, False):
        @pl.when(first == build_mem)
        def _(build_mem=build_mem):
            subs = [slice(r, r + ROW_SUB) for r in range(0, x_ref.shape[0], ROW_SUB)]
            yo = [_dot(ylru_ref[r, :], wout_ref[0:D_LRU, :])
                  + _dot(ymla_ref[r, :], wout_ref[D_LRU:, :]) for r in subs]
            h1 = [x_ref[r, :] + _rms(y, gpost_ref[...]) for r, y in zip(subs, yo)]

            xn = [_rms(h, gpre_ref[...]).astype(BF16) for h in h1]
            mq = [(_dot(v, wmq_ref[...]) * (1.0 / math.sqrt(MEM_HEAD_DIM))).astype(BF16)
                  for v in xn]
            if build_mem:
                mn = _rms(mem_ref[...], gmem_ref[...]).astype(BF16)
                mk = _dot(mn, wmk_ref[...]).astype(BF16)
                mv = _dot(mn, wmv_ref[...]).astype(BF16)
                mk_ref[...] = mk
                mv_ref[...] = mv
            else:
                mk, mv = mk_ref, mv_ref
            for r, q in zip(subs, mq):
                for hd in range(MEM_HEADS):
                    sl = slice(hd * MEM_HEAD_DIM, (hd + 1) * MEM_HEAD_DIM)
                    s = _dot_nt(q[:, sl], mk[:, sl])
                    m = jnp.max(s, axis=-1, keepdims=True)
                    p = jnp.exp(s - m)
                    l = jnp.sum(p, axis=-1, keepdims=True)
                    o_scr[r, sl] = (_dot(p.astype(BF16), mv[:, sl]) / l).astype(BF16)
            mo = [_dot(o_scr[r, :], wmo_ref[...]) for r in subs]
            for r, h, o in zip(subs, h1, mo):
                h_ref[r, :] = h + _rms(o, gpm_ref[...])


def _mix_mem(ylru, ymla, x, mem, wout, gpost, gpre, wmq, gmem, wmk, wmv, wmo, gpm):
    bsz, s, _ = x.shape
    tm = TM_MIX
    tok = lambda w: pl.BlockSpec((None, tm, w), lambda b, i: (b, i, 0))
    memspec = pl.BlockSpec((None, N_MEM, D_MODEL), lambda b, i: (b, 0, 0))
    sq = pl.BlockSpec((D_MODEL, D_MODEL), lambda b, i: (0, 0), pipeline_mode=pl.Buffered(1))
    vec = _const_spec((1, D_MODEL))
    return pl.pallas_call(
        _mix_mem_kernel,
        out_shape=jax.ShapeDtypeStruct((bsz, s, D_MODEL), F32),
        grid=(bsz, s // tm),
        in_specs=[tok(D_LRU), tok(D_MLA_OUT), tok(D_MODEL), memspec, sq, vec, vec, sq, vec, sq, sq,
                  sq, vec],
        out_specs=tok(D_MODEL),
        scratch_shapes=[pltpu.VMEM((tm, D_MODEL), BF16), pltpu.VMEM((N_MEM, D_MODEL), BF16),
                        pltpu.VMEM((N_MEM, D_MODEL), BF16)],
        compiler_params=pltpu.CompilerParams(dimension_semantics=("parallel", "arbitrary"),
                                             vmem_limit_bytes=VMEM_LIMIT),
        name="mix_mem",
    )(ylru, ymla, x, mem, wout, gpost, gpre, wmq, gmem, wmk, wmv, wmo, gpm)


def _ffn_kernel(h_ref, gpre_ref, wg_ref, wu_ref, wd_ref, gpost_ref, o_ref):
    subs = [slice(r, r + ROW_SUB) for r in range(0, h_ref.shape[0], ROW_SUB)]
    xn = [_rms(h_ref[r, :], gpre_ref[...]).astype(BF16) for r in subs]
    f = [None] * len(subs)
    for lo, hi in FF_CHUNKS:
        for n, v in enumerate(xn):
            act = (jax.nn.silu(_dot(v, wg_ref[:, lo:hi])) * _dot(v, wu_ref[:, lo:hi])).astype(BF16)
            part = _dot(act, wd_ref[lo:hi, :])
            f[n] = part if f[n] is None else f[n] + part
    for r, fr in zip(subs, f):
        o_ref[r, :] = h_ref[r, :] + _rms(fr, gpost_ref[...])


def _ffn(h, gpre, wg, wu, wd, gpost):
    m, _ = h.shape
    tm = TM_FFN
    single = pl.Buffered(1)
    tok = pl.BlockSpec((tm, D_MODEL), lambda i: (i, 0))
    vec = _const_spec((1, D_MODEL))
    return pl.pallas_call(
        _ffn_kernel,
        out_shape=jax.ShapeDtypeStruct((m, D_MODEL), F32),
        grid=(m // tm,),
        in_specs=[
            tok, vec,
            pl.BlockSpec((D_MODEL, D_FF), lambda i: (0, 0), pipeline_mode=single),
            pl.BlockSpec((D_MODEL, D_FF), lambda i: (0, 0), pipeline_mode=single),
            pl.BlockSpec((D_FF, D_MODEL), lambda i: (0, 0), pipeline_mode=single),
            vec,
        ],
        out_specs=tok,
        compiler_params=pltpu.CompilerParams(dimension_semantics=("parallel",),
                                             vmem_limit_bytes=VMEM_LIMIT),
        name="ffn",
    )(h, gpre, wg, wu, wd, gpost)


def _block_diag(w):
    eye = jnp.eye(LRU_BLOCKS, dtype=w.dtype)
    return jnp.einsum('hij,hg->higj', w, eye).reshape(D_LRU, D_LRU)


def _gate_weights(wa, wx):
    da, dx = _block_diag(wa) * -math.log2(math.e), _block_diag(wx) * -math.log2(math.e)
    groups = []
    for j in range(D_LRU // LRU_GROUP):
        sl = slice(j * LRU_GROUP, (j + 1) * LRU_GROUP)
        groups.append(jnp.concatenate([da[sl, sl], dx[sl, sl]], axis=1))
    return jnp.stack(groups).astype(BF16)


def _pad_in_proj(w_in, gain):
    w = gain[:, None] * w_in
    pad = jnp.zeros((D_MODEL, Z_WIDTH - w.shape[1]), w.dtype)
    return jnp.concatenate([w, pad], axis=1).astype(BF16)


def _q_weights_t(w_uq, gain):
    dh = QK_NOPE_DIM + QK_ROPE_DIM
    scale = math.log2(math.e) / math.sqrt(dh)
    w = (scale * gain[:, None] * w_uq).reshape(Q_LORA_RANK, MLA_HEADS, dh)
    pad = jnp.zeros((Q_LORA_RANK, MLA_HEADS, HEAD_SLAB - dh), w.dtype)
    w = jnp.concatenate([w, pad], axis=-1).reshape(Q_LORA_RANK, MLA_HEADS * HEAD_SLAB)
    return w.T.astype(BF16)


def _kv_weights(w_ukv, gain):
    w = (gain[:, None] * w_ukv).reshape(KV_LORA_RANK, MLA_HEADS, QK_NOPE_DIM + V_HEAD_DIM)
    kn = w[:, :, :QK_NOPE_DIM].reshape(KV_LORA_RANK, MLA_HEADS * QK_NOPE_DIM)
    vv = w[:, :, QK_NOPE_DIM:].reshape(KV_LORA_RANK, MLA_HEADS * V_HEAD_DIM)
    return kn.astype(BF16), vv.T.astype(BF16)


def _rope_freq_column():
    inv_freq = ROPE_THETA ** (-jnp.arange(0, QK_ROPE_DIM, 2, dtype=F32) / QK_ROPE_DIM)
    return inv_freq.reshape(QK_ROPE_DIM // 2, 1)


def kernel(x, mem, positions, g_pre_mix, w_in, conv_w, conv_b, lru_wa, lru_ba, lru_wx, lru_bx, lru_lambda, g_q_lat, w_uq, g_kv_lat, w_ukv, g_lru_out, g_mla_out, w_out, g_post_mix, g_pre_mem, g_mem_kv, w_mq, w_mk, w_mv, w_mo, g_post_mem, g_pre_ffn, w_gate, w_up, w_down, g_post_ffn):
    bsz, s, d = x.shape
    depth = w_in.shape[0]
    assert d == D_MODEL and s % TS_IN == 0 and s % TM_MIX == 0
    assert (bsz * s) % TM_FFN == 0 and mem.shape == (bsz, N_MEM, D_MODEL)
    pos_row = positions.reshape(bsz, 1, s)
    invf = _rope_freq_column()
    row = lambda a: a.reshape(1, -1)

    h = x
    for l in range(depth):
        wkn, wvt = _kv_weights(w_ukv[l], g_kv_lat[l])
        neg_log2e = -math.log2(math.e)
        qt, k, vt, ylru = _in_proj(
            h, pos_row, invf, _pad_in_proj(w_in[l], g_pre_mix[l]), conv_w[l], row(conv_b[l]),
            _gate_weights(lru_wa[l], lru_wx[l]), row(neg_log2e * lru_ba[l]),
            row(neg_log2e * lru_bx[l]), row(lru_lambda[l]),
            _q_weights_t(w_uq[l], g_q_lat[l]), wkn, wvt)
        ymla = _mla_attn(qt, k, vt)
        wout = (jnp.concatenate([g_lru_out[l], g_mla_out[l]])[:, None] * w_out[l]).astype(BF16)
        h = _mix_mem(ylru, ymla, h, mem, wout, row(g_post_mix[l]), row(g_pre_mem[l]),
                     w_mq[l].astype(BF16), row(g_mem_kv[l]), w_mk[l].astype(BF16),
                     w_mv[l].astype(BF16), w_mo[l].astype(BF16), row(g_post_mem[l]))
        h = _ffn(h.reshape(bsz * s, d), row(g_pre_ffn[l]), w_gate[l].astype(BF16),
                 w_up[l].astype(BF16), w_down[l].astype(BF16), row(g_post_ffn[l])).reshape(bsz, s, d)
    return h
```

```python
import math

import jax
import jax.numpy as jnp
from jax import lax
from jax.experimental import pallas as pl
from jax.experimental.pallas import tpu as pltpu

F32 = jnp.float32
BF16 = jnp.bfloat16

D_MODEL = 1024
N_MEM = 256
MEM_HEADS = 4
MEM_HEAD_DIM = D_MODEL // MEM_HEADS
D_LRU = 512
LRU_BLOCKS = 8
LRU_BLOCK_DIM = D_LRU // LRU_BLOCKS
CONV_WIDTH = 4
LRU_C = 8.0
MLA_HEADS = 4
QK_NOPE_DIM = 128
QK_ROPE_DIM = 64
V_HEAD_DIM = 128
Q_LORA_RANK = 384
KV_LORA_RANK = 256
D_MLA_OUT = MLA_HEADS * V_HEAD_DIM
ROPE_THETA = 10000.0
D_FF = 2816
RMS_EPS = 1e-6
NEG_INF = -1e30

LANES = 128
SUBLANES = 8
HEAD_SLAB = 2 * LANES
BF16_ROWS = 16
VT_ROWS = V_HEAD_DIM + BF16_ROWS
Z_LRU_X = 0
Z_LRU_GATE = Z_LRU_X + D_LRU
Z_CQ = Z_LRU_GATE + D_LRU
Z_CKV = Z_CQ + Q_LORA_RANK
Z_KPE = Z_CKV + KV_LORA_RANK
Z_WIDTH = Z_KPE + LANES
MXU_TILE = 256
LRU_GROUP = MXU_TILE

TS_IN = 512
IN_TILES_PER_STEP = 2
IN_SUB = 512
TM_MIX = 1024
Q_TILES_PER_STEP = 4
SCORE_LOOKAHEAD = 2
ROW_SUB = 256
TM_FFN = 1024
FF_CHUNKS = ((0, 1536), (1536, D_FF))
VMEM_LIMIT = 56 * 1024 * 1024


def _rms(x, g):
    ms = jnp.mean(x * x, axis=-1, keepdims=True)
    return x * lax.rsqrt(ms + RMS_EPS) * g


def _rms_plain(x):
    ms = jnp.mean(x * x, axis=-1, keepdims=True)
    return x * lax.rsqrt(ms + RMS_EPS)


def _gelu_tanh(x):
    k1 = -2.0 * math.sqrt(2.0 / math.pi) * math.log2(math.e)
    k2 = k1 * 0.044715
    return x / (1.0 + jnp.exp2(x * (k1 + k2 * (x * x))))


def _dot(a, b):
    return jnp.dot(a, b, preferred_element_type=F32)


def _dot_nt(a, b):
    return lax.dot_general(a, b, (((1,), (1,)), ((), ())), preferred_element_type=F32)


def _const_spec(shape):
    return pl.BlockSpec(shape, lambda *_: (0,) * len(shape))


def _shift_rows(x, prev, d):
    r = pltpu.roll(x, d, 0)
    row = lax.broadcasted_iota(jnp.int32, prev.shape, 0)
    head = jnp.where(row < d, pltpu.roll(prev, d, 0), r[0:SUBLANES])
    return jnp.concatenate([head, r[SUBLANES:]], axis=0)


def _linear_scan(a, u, h0):
    t, c = a.shape
    groups = t // SUBLANES
    a = a.reshape(groups, SUBLANES, c)
    u = u.reshape(groups, SUBLANES, c)
    row = lax.broadcasted_iota(jnp.int32, a.shape, 1)
    for s in (1, 2, 4):
        a_s = pltpu.roll(a, s, 1)
        u_s = pltpu.roll(u, s, 1)
        m = row >= s
        u = jnp.where(m, a * u_s + u, u)
        a = jnp.where(m, a * a_s, a)
    hs = []
    h = h0
    for g in range(groups):
        hg = a[g] * h + u[g]
        hs.append(hg)
        h = hg[SUBLANES - 1:SUBLANES]
    return jnp.concatenate(hs, axis=0)


def _rope(t, cos, sin):
    half = QK_ROPE_DIM // 2
    return t * cos + pltpu.roll(t, half, 1) * sin - pltpu.roll(t, LANES - half, 1) * sin


def _in_proj_kernel(x_ref, pos_ref, invf_ref, win_ref, convw_ref, convb_ref, wg_ref,
                    ba_ref, bx_ref, lam_ref, wuqt_ref, wkn_ref, wvt_ref,
                    qt_ref, k_ref, vt_ref, ylru_ref, xprev_ref, hprev_ref):
    ts = IN_SUB
    tiles = range(x_ref.shape[0] // ts)
    rows = [slice(n * ts, (n + 1) * ts) for n in tiles]
    half = QK_ROPE_DIM // 2

    @pl.when(pl.program_id(1) == 0)
    def _():
        xprev_ref[...] = jnp.zeros_like(xprev_ref)
        hprev_ref[...] = jnp.zeros_like(hprev_ref)

    xn = [_rms_plain(x_ref[r, :]).astype(BF16) for r in rows]
    z_tiles = [{} for _ in tiles]

    def z(n, lo, hi):
        parts = []
        for j in range(lo // MXU_TILE, -(-hi // MXU_TILE)):
            if j not in z_tiles[n]:
                z_tiles[n][j] = _dot(xn[n], win_ref[:, j * MXU_TILE:(j + 1) * MXU_TILE])
            a, b = max(lo, j * MXU_TILE), min(hi, (j + 1) * MXU_TILE)
            parts.append(z_tiles[n][j][:, a - j * MXU_TILE:b - j * MXU_TILE])
        return parts[0] if len(parts) == 1 else jnp.concatenate(parts, axis=-1)

    xl = [z(n, Z_LRU_X, Z_LRU_GATE) for n in tiles]
    u = []
    prev = xprev_ref[...]
    for n in tiles:
        un = convb_ref[...] + convw_ref[CONV_WIDTH - 1:CONV_WIDTH, :] * xl[n]
        for d in range(1, CONV_WIDTH):
            k = CONV_WIDTH - 1 - d
            un = un + convw_ref[k:k + 1, :] * _shift_rows(xl[n], prev, d)
        u.append(un)
        prev = xl[n][ts - SUBLANES:ts]
    xprev_ref[...] = prev
    pre = [[_dot(u[n][:, j * LRU_GROUP:(j + 1) * LRU_GROUP].astype(BF16), wg_ref[j])
            for j in range(D_LRU // LRU_GROUP)] for n in tiles]

    z_gate = [z(n, Z_LRU_GATE, Z_CQ) for n in tiles]
    z_cq = [z(n, Z_CQ, Z_CKV) for n in tiles]
    z_ckv = [z(n, Z_CKV, Z_KPE) for n in tiles]
    z_kpe = [z(n, Z_KPE, Z_WIDTH) for n in tiles]
    cqn = [_rms_plain(v) for v in z_cq]
    ckvn = [_rms_plain(v) for v in z_ckv]
    q_t = [_dot(wuqt_ref[...], v.T.astype(BF16)) for v in cqn]
    v_t = [_dot(wvt_ref[...], v.T.astype(BF16)) for v in ckvn]
    kn = [_dot(v.astype(BF16), wkn_ref[...]) for v in ckvn]

    nlam = -lam_ref[...]
    softplus = jnp.maximum(nlam, 0.0) + jnp.log1p(jnp.exp(-jnp.abs(nlam)))
    h_last = hprev_ref[SUBLANES - 1:SUBLANES, :]
    for n in tiles:
        r = 1.0 / (1.0 + jnp.exp2(
            jnp.concatenate([p[:, :LRU_GROUP] for p in pre[n]], axis=-1) + ba_ref[...]))
        gi = 1.0 / (1.0 + jnp.exp2(
            jnp.concatenate([p[:, LRU_GROUP:] for p in pre[n]], axis=-1) + bx_ref[...]))
        a = jnp.exp2(r * ((-LRU_C * math.log2(math.e)) * softplus))
        one_m_a2 = 1.0 - a * a
        root = jnp.where(one_m_a2 > 0.0, one_m_a2 * lax.rsqrt(one_m_a2), 0.0)
        h = _linear_scan(a, root * (gi * u[n]), h_last)
        h_last = h[ts - 1:ts]
        if n == tiles[-1]:
            hprev_ref[...] = h[ts - SUBLANES:ts]
        ylru_ref[rows[n], :] = _rms_plain(h * _gelu_tanh(z_gate[n])).astype(BF16)

    for n in tiles:
        ang_t = invf_ref[...] * pos_ref[:, rows[n]].astype(F32)
        cos_t = jnp.cos(ang_t)
        sin_t = jnp.sin(ang_t)
        t_out = n * ts // TS_IN
        cols = slice(n * ts % TS_IN, n * ts % TS_IN + ts)
        for hd in range(MLA_HEADS):
            r0 = hd * HEAD_SLAB
            pe = r0 + QK_NOPE_DIM
            x1 = q_t[n][pe:pe + half]
            x2 = q_t[n][pe + half:pe + 2 * half]
            qt_ref[t_out, r0:pe, cols] = q_t[n][r0:pe].astype(BF16)
            qt_ref[t_out, pe:pe + half, cols] = (x1 * cos_t - x2 * sin_t).astype(BF16)
            qt_ref[t_out, pe + half:pe + 2 * half, cols] = (x2 * cos_t + x1 * sin_t).astype(BF16)
            qt_ref[t_out, pe + 2 * half:r0 + HEAD_SLAB, cols] = jnp.zeros(
                (HEAD_SLAB - QK_NOPE_DIM - 2 * half, ts), BF16)
            v0 = hd * VT_ROWS
            vt_ref[t_out, v0:v0 + V_HEAD_DIM, cols] = (
                v_t[n][hd * V_HEAD_DIM:(hd + 1) * V_HEAD_DIM].astype(BF16))
            vt_ref[t_out, v0 + V_HEAD_DIM:v0 + VT_ROWS, cols] = jnp.ones((BF16_ROWS, ts), BF16)
        pad = jnp.zeros((LANES - 2 * half, ts), F32)
        cos = jnp.concatenate([cos_t, cos_t, pad], axis=0).T
        sin = jnp.concatenate([sin_t, sin_t, pad], axis=0).T
        kpe = _rope(z_kpe[n], cos, sin).astype(BF16)
        for hd in range(MLA_HEADS):
            c0 = hd * HEAD_SLAB
            k_ref[rows[n], c0:c0 + LANES] = kn[n][:, hd * LANES:(hd + 1) * LANES].astype(BF16)
            k_ref[rows[n], c0 + LANES:c0 + HEAD_SLAB] = kpe


def _in_proj(x, pos_row, invf, win, convw, convb, wg, ba, bx, lam, wuqt, wkn, wvt):
    bsz, s, _ = x.shape
    ts = TS_IN
    step = TS_IN * IN_TILES_PER_STEP
    tok = lambda w: pl.BlockSpec((None, step, w), lambda b, i: (b, i, 0))
    in_specs = [
        tok(D_MODEL), pl.BlockSpec((None, 1, step), lambda b, i: (b, 0, i)),
        _const_spec((QK_ROPE_DIM // 2, 1)),
        _const_spec((D_MODEL, Z_WIDTH)), _const_spec((CONV_WIDTH, D_LRU)), _const_spec((1, D_LRU)),
        _const_spec((D_LRU // LRU_GROUP, LRU_GROUP, 2 * LRU_GROUP)),
        _const_spec((1, D_LRU)), _const_spec((1, D_LRU)), _const_spec((1, D_LRU)),
        _const_spec((MLA_HEADS * HEAD_SLAB, Q_LORA_RANK)),
        _const_spec((KV_LORA_RANK, MLA_HEADS * QK_NOPE_DIM)),
        _const_spec((D_MLA_OUT, KV_LORA_RANK)),
    ]
    tile_t = lambda w: pl.BlockSpec((None, IN_TILES_PER_STEP, w, ts), lambda b, i: (b, i, 0, 0))
    out_shape = (
        jax.ShapeDtypeStruct((bsz, s // ts, MLA_HEADS * HEAD_SLAB, ts), BF16),
        jax.ShapeDtypeStruct((bsz, s, MLA_HEADS * HEAD_SLAB), BF16),
        jax.ShapeDtypeStruct((bsz, s // ts, MLA_HEADS * VT_ROWS, ts), BF16),
        jax.ShapeDtypeStruct((bsz, s, D_LRU), BF16),
    )
    out_specs = (tile_t(MLA_HEADS * HEAD_SLAB), tok(MLA_HEADS * HEAD_SLAB),
                 tile_t(MLA_HEADS * VT_ROWS), tok(D_LRU))
    return pl.pallas_call(
        _in_proj_kernel,
        out_shape=out_shape,
        grid=(bsz, s // step),
        in_specs=in_specs,
        out_specs=out_specs,
        scratch_shapes=[pltpu.VMEM((SUBLANES, D_LRU), F32), pltpu.VMEM((SUBLANES, D_LRU), F32)],
        compiler_params=pltpu.CompilerParams(dimension_semantics=("parallel", "arbitrary"),
                                             vmem_limit_bytes=VMEM_LIMIT),
        name="in_proj",
    )(x, pos_row, invf, win, convw, convb, wg, ba, bx, lam, wuqt, wkn, wvt)


def _mla_attn_kernel(qt_ref, k_ref, vt_ref, o_ref):
    n_local, _, tq = qt_ref.shape
    nq = vt_ref.shape[0]
    i = pl.program_id(1)
    hq = tq // 2

    def scores(ql, j, hd, diagonal):
        slab = slice(hd * HEAD_SLAB, (hd + 1) * HEAD_SLAB)
        if not diagonal:
            st = _dot(k_ref[j * tq:(j + 1) * tq, slab], qt_ref[ql, slab, :])
            return [(0, 0, False, st[:, :hq]), (1, 0, False, st[:, hq:])]
        return [(0, 0, True, _dot(k_ref[j * tq:j * tq + hq, slab], qt_ref[ql, slab, 0:hq])),
                (1, 0, True, _dot(k_ref[j * tq:(j + 1) * tq, slab], qt_ref[ql, slab, hq:tq]))]

    def finish(ql, acc):
        outs = []
        for hd in range(MLA_HEADS):
            a = jnp.concatenate(acc[hd], axis=1)
            outs.append(a[:V_HEAD_DIM] * (1.0 / a[V_HEAD_DIM:V_HEAD_DIM + 1]))
        out_t = jnp.concatenate(outs, axis=0)
        ms = jnp.mean(out_t * out_t, axis=0, keepdims=True)
        o_ref[ql * tq:(ql + 1) * tq, :] = (out_t * lax.rsqrt(ms + RMS_EPS)).astype(BF16).T

    def attend(first_tile):
        units = [(ql, j, hd, j == first_tile + ql)
                 for ql in range(n_local) for j in range(first_tile + ql + 1)
                 for hd in range(MLA_HEADS)]
        ahead = [scores(*u) for u in units[:SCORE_LOOKAHEAD]]
        m = acc = None
        for n, (ql, j, hd, diagonal) in enumerate(units):
            if j == 0 and hd == 0:
                m = [[jnp.full((1, hq), NEG_INF, F32)] * 2 for _ in range(MLA_HEADS)]
                acc = [[None, None] for _ in range(MLA_HEADS)]
            pieces = ahead.pop(0)
            if n + SCORE_LOOKAHEAD < len(units):
                ahead.append(scores(*units[n + SCORE_LOOKAHEAD]))
            vrows = slice(hd * VT_ROWS, (hd + 1) * VT_ROWS)
            for half, k0, masked, st in pieces:
                nk = st.shape[0]
                if masked:
                    kid = lax.broadcasted_iota(jnp.int32, (nk, hq), 0) + k0
                    qid = lax.broadcasted_iota(jnp.int32, (nk, hq), 1) + half * hq
                    st = jnp.where(qid >= kid, st, NEG_INF)
                m_old = m[hd][half]
                m_new = jnp.maximum(m_old, jnp.max(st, axis=0, keepdims=True))
                p = jnp.exp2(st - m_new).astype(BF16)
                pv = _dot(vt_ref[j, vrows, k0:k0 + nk], p)
                old = acc[hd][half]
                acc[hd][half] = pv if old is None else jnp.exp2(m_old - m_new) * old + pv
                m[hd][half] = m_new
            if diagonal and hd == MLA_HEADS - 1:
                finish(ql, acc)

    for n in range(nq // n_local):
        @pl.when(i == n)
        def _(n=n):
            attend(n * n_local)


def _mla_attn(qt, k, vt):
    bsz, nq, _, tq = qt.shape
    s = k.shape[1]
    per = Q_TILES_PER_STEP
    return pl.pallas_call(
        _mla_attn_kernel,
        out_shape=jax.ShapeDtypeStruct((bsz, s, D_MLA_OUT), BF16),
        grid=(bsz, nq // per),
        in_specs=[
            pl.BlockSpec((None, per, MLA_HEADS * HEAD_SLAB, tq), lambda b, i: (b, i, 0, 0)),
            pl.BlockSpec((None, s, MLA_HEADS * HEAD_SLAB), lambda b, i: (b, 0, 0)),
            pl.BlockSpec((None, nq, MLA_HEADS * VT_ROWS, tq), lambda b, i: (b, 0, 0, 0)),
        ],
        out_specs=pl.BlockSpec((None, per * tq, D_MLA_OUT), lambda b, i: (b, i, 0)),
        compiler_params=pltpu.CompilerParams(dimension_semantics=("parallel", "arbitrary"),
                                             vmem_limit_bytes=VMEM_LIMIT),
        name="mla_attn",
    )(qt, k, vt)


def _mix_mem_kernel(ylru_ref, ymla_ref, x_ref, mem_ref, wout_ref, gpost_ref, gpre_ref, wmq_ref,
                    gmem_ref, wmk_ref, wmv_ref, wmo_ref, gpm_ref, h_ref, o_scr, mk_ref, mv_ref):
    first = pl.program_id(1) == 0
    for build_mem in (True, False):
        @pl.when(first == build_mem)
        def _(build_mem=build_mem):
            subs = [slice(r, r + ROW_SUB) for r in range(0, x_ref.shape[0], ROW_SUB)]
            yo = [_dot(ylru_ref[r, :], wout_ref[0:D_LRU, :])
                  + _dot(ymla_ref[r, :], wout_ref[D_LRU:, :]) for r in subs]
            h1 = [x_ref[r, :] + _rms(y, gpost_ref[...]) for r, y in zip(subs, yo)]

            xn = [_rms(h, gpre_ref[...]).astype(BF16) for h in h1]
            mq = [(_dot(v, wmq_ref[...]) * (1.0 / math.sqrt(MEM_HEAD_DIM))).astype(BF16)
                  for v in xn]
            if build_mem:
                mn = _rms(mem_ref[...], gmem_ref[...]).astype(BF16)
                mk = _dot(mn, wmk_ref[...]).astype(BF16)
                mv = _dot(mn, wmv_ref[...]).astype(BF16)
                mk_ref[...] = mk
                mv_ref[...] = mv
            else:
                mk, mv = mk_ref, mv_ref
            for r, q in zip(subs, mq):
                for hd in range(MEM_HEADS):
                    sl = slice(hd * MEM_HEAD_DIM, (hd + 1) * MEM_HEAD_DIM)
                    s = _dot_nt(q[:, sl], mk[:, sl])
                    m = jnp.max(s, axis=-1, keepdims=True)
                    p = jnp.exp(s - m)
                    l = jnp.sum(p, axis=-1, keepdims=True)
                    o_scr[r, sl] = (_dot(p.astype(BF16), mv[:, sl]) / l).astype(BF16)
            mo = [_dot(o_scr[r, :], wmo_ref[...]) for r in subs]
            for r, h, o in zip(subs, h1, mo):
                h_ref[r, :] = h + _rms(o, gpm_ref[...])


def _mix_mem(ylru, ymla, x, mem, wout, gpost, gpre, wmq, gmem, wmk, wmv, wmo, gpm):
    bsz, s, _ = x.shape
    tm = TM_MIX
    tok = lambda w: pl.BlockSpec((None, tm, w), lambda b, i: (b, i, 0))
    memspec = pl.BlockSpec((None, N_MEM, D_MODEL), lambda b, i: (b, 0, 0))
    sq = pl.BlockSpec((D_MODEL, D_MODEL), lambda b, i: (0, 0), pipeline_mode=pl.Buffered(1))
    vec = _const_spec((1, D_MODEL))
    return pl.pallas_call(
        _mix_mem_kernel,
        out_shape=jax.ShapeDtypeStruct((bsz, s, D_MODEL), F32),
        grid=(bsz, s // tm),
        in_specs=[tok(D_LRU), tok(D_MLA_OUT), tok(D_MODEL), memspec, sq, vec, vec, sq, vec, sq, sq,
                  sq, vec],
        out_specs=tok(D_MODEL),
        scratch_shapes=[pltpu.VMEM((tm, D_MODEL), BF16), pltpu.VMEM((N_MEM, D_MODEL), BF16),
                        pltpu.VMEM((N_MEM, D_MODEL), BF16)],
        compiler_params=pltpu.CompilerParams(dimension_semantics=("parallel", "arbitrary"),
                                             vmem_limit_bytes=VMEM_LIMIT),
        name="mix_mem",
    )(ylru, ymla, x, mem, wout, gpost, gpre, wmq, gmem, wmk, wmv, wmo, gpm)


def _ffn_kernel(h_ref, gpre_ref, wg_ref, wu_ref, wd_ref, gpost_ref, o_ref):
    subs = [slice(r, r + ROW_SUB) for r in range(0, h_ref.shape[0], ROW_SUB)]
    xn = [_rms(h_ref[r, :], gpre_ref[...]).astype(BF16) for r in subs]
    f = [None] * len(subs)
    for lo, hi in FF_CHUNKS:
        for n, v in enumerate(xn):
            act = (jax.nn.silu(_dot(v, wg_ref[:, lo:hi])) * _dot(v, wu_ref[:, lo:hi])).astype(BF16)
            part = _dot(act, wd_ref[lo:hi, :])
            f[n] = part if f[n] is None else f[n] + part
    for r, fr in zip(subs, f):
        o_ref[r, :] = h_ref[r, :] + _rms(fr, gpost_ref[...])


def _ffn(h, gpre, wg, wu, wd, gpost):
    m, _ = h.shape
    tm = TM_FFN
    single = pl.Buffered(1)
    tok = pl.BlockSpec((tm, D_MODEL), lambda i: (i, 0))
    vec = _const_spec((1, D_MODEL))
    return pl.pallas_call(
        _ffn_kernel,
        out_shape=jax.ShapeDtypeStruct((m, D_MODEL), F32),
        grid=(m // tm,),
        in_specs=[
            tok, vec,
            pl.BlockSpec((D_MODEL, D_FF), lambda i: (0, 0), pipeline_mode=single),
            pl.BlockSpec((D_MODEL, D_FF), lambda i: (0, 0), pipeline_mode=single),
            pl.BlockSpec((D_FF, D_MODEL), lambda i: (0, 0), pipeline_mode=single),
            vec,
        ],
        out_specs=tok,
        compiler_params=pltpu.CompilerParams(dimension_semantics=("parallel",),
                                             vmem_limit_bytes=VMEM_LIMIT),
        name="ffn",
    )(h, gpre, wg, wu, wd, gpost)


def _block_diag(w):
    eye = jnp.eye(LRU_BLOCKS, dtype=w.dtype)
    return jnp.einsum('hij,hg->higj', w, eye).reshape(D_LRU, D_LRU)


def _gate_weights(wa, wx):
    da, dx = _block_diag(wa) * -math.log2(math.e), _block_diag(wx) * -math.log2(math.e)
    groups = []
    for j in range(D_LRU // LRU_GROUP):
        sl = slice(j * LRU_GROUP, (j + 1) * LRU_GROUP)
        groups.append(jnp.concatenate([da[sl, sl], dx[sl, sl]], axis=1))
    return jnp.stack(groups).astype(BF16)


def _pad_in_proj(w_in, gain):
    w = gain[:, None] * w_in
    pad = jnp.zeros((D_MODEL, Z_WIDTH - w.shape[1]), w.dtype)
    return jnp.concatenate([w, pad], axis=1).astype(BF16)


def _q_weights_t(w_uq, gain):
    dh = QK_NOPE_DIM + QK_ROPE_DIM
    scale = math.log2(math.e) / math.sqrt(dh)
    w = (scale * gain[:, None] * w_uq).reshape(Q_LORA_RANK, MLA_HEADS, dh)
    pad = jnp.zeros((Q_LORA_RANK, MLA_HEADS, HEAD_SLAB - dh), w.dtype)
    w = jnp.concatenate([w, pad], axis=-1).reshape(Q_LORA_RANK, MLA_HEADS * HEAD_SLAB)
    return w.T.astype(BF16)


def _kv_weights(w_ukv, gain):
    w = (gain[:, None] * w_ukv).reshape(KV_LORA_RANK, MLA_HEADS, QK_NOPE_DIM + V_HEAD_DIM)
    kn = w[:, :, :QK_NOPE_DIM].reshape(KV_LORA_RANK, MLA_HEADS * QK_NOPE_DIM)
    vv = w[:, :, QK_NOPE_DIM:].reshape(KV_LORA_RANK, MLA_HEADS * V_HEAD_DIM)
    return kn.astype(BF16), vv.T.astype(BF16)


def _rope_freq_column():
    inv_freq = ROPE_THETA ** (-jnp.arange(0, QK_ROPE_DIM, 2, dtype=F32) / QK_ROPE_DIM)
    return inv_freq.reshape(QK_ROPE_DIM // 2, 1)


def kernel(x, mem, positions, g_pre_mix, w_in, conv_w, conv_b, lru_wa, lru_ba, lru_wx, lru_bx, lru_lambda, g_q_lat, w_uq, g_kv_lat, w_ukv, g_lru_out, g_mla_out, w_out, g_post_mix, g_pre_mem, g_mem_kv, w_mq, w_mk, w_mv, w_mo, g_post_mem, g_pre_ffn, w_gate, w_up, w_down, g_post_ffn):
    bsz, s, d = x.shape
    depth = w_in.shape[0]
    assert d == D_MODEL and s % TS_IN == 0 and s % TM_MIX == 0
    assert (bsz * s) % TM_FFN == 0 and mem.shape == (bsz, N_MEM, D_MODEL)
    pos_row = positions.reshape(bsz, 1, s)
    invf = _rope_freq_column()
    row = lambda a: a.reshape(1, -1)

    h = x
    for l in range(depth):
        wkn, wvt = _kv_weights(w_ukv[l], g_kv_lat[l])
        neg_log2e = -math.log2(math.e)
        qt, k, vt, ylru = _in_proj(
            h, pos_row, invf, _pad_in_proj(w_in[l], g_pre_mix[l]), conv_w[l], row(conv_b[l]),
            _gate_weights(lru_wa[l], lru_wx[l]), row(neg_log2e * lru_ba[l]),
            row(neg_log2e * lru_bx[l]), row(lru_lambda[l]),
            _q_weights_t(w_uq[l], g_q_lat[l]), wkn, wvt)
        ymla = _mla_attn(qt, k, vt)
        wout = (jnp.concatenate([g_lru_out[l], g_mla_out[l]])[:, None] * w_out[l]).astype(BF16)
        h = _mix_mem(ylru, ymla, h, mem, wout, row(g_post_mix[l]), row(g_pre_mem[l]),
                     w_mq[l].astype(BF16), row(g_mem_kv[l]), w_mk[l].astype(BF16),
                     w_mv[l].astype(BF16), w_mo[l].astype(BF16), row(g_post_mem[l]))
        h = _ffn(h.reshape(bsz * s, d), row(g_pre_ffn[l]), w_gate[l].astype(BF16),
                 w_up[l].astype(BF16), w_down[l].astype(BF16), row(g_post_ffn[l])).reshape(bsz, s, d)
    return h
```

```python
import math

import jax
import jax.numpy as jnp
from jax import lax
from jax.experimental import pallas as pl
from jax.experimental.pallas import tpu as pltpu

F32 = jnp.float32
BF16 = jnp.bfloat16

D_MODEL = 1024
N_MEM = 256
MEM_HEADS = 4
MEM_HEAD_DIM = D_MODEL // MEM_HEADS
D_LRU = 512
LRU_BLOCKS = 8
LRU_BLOCK_DIM = D_LRU // LRU_BLOCKS
CONV_WIDTH = 4
LRU_C = 8.0
MLA_HEADS = 4
QK_NOPE_DIM = 128
QK_ROPE_DIM = 64
V_HEAD_DIM = 128
Q_LORA_RANK = 384
KV_LORA_RANK = 256
D_MLA_OUT = MLA_HEADS * V_HEAD_DIM
ROPE_THETA = 10000.0
D_FF = 2816
RMS_EPS = 1e-6
NEG_INF = -1e30

LANES = 128
SUBLANES = 8
HEAD_SLAB = 2 * LANES
BF16_ROWS = 16
VT_ROWS = V_HEAD_DIM + BF16_ROWS
Z_LRU_X = 0
Z_LRU_GATE = Z_LRU_X + D_LRU
Z_CQ = Z_LRU_GATE + D_LRU
Z_CKV = Z_CQ + Q_LORA_RANK
Z_KPE = Z_CKV + KV_LORA_RANK
Z_WIDTH = Z_KPE + LANES
MXU_TILE = 256
LRU_GROUP = MXU_TILE

TS_IN = 512
IN_BATCHES = 8
IN_STEPS = 128
IN_TILES_PER_STEP = 2
IN_SUB = 512
TM_MIX = 1024
Q_TILES_PER_STEP = 4
SCORE_LOOKAHEAD = 2
ROW_SUB = 256
TM_FFN = 1024
FF_CHUNKS = ((0, 1536), (1536, D_FF))
VMEM_LIMIT = 56 * 1024 * 1024


def _rms(x, g):
    ms = jnp.mean(x * x, axis=-1, keepdims=True)
    return x * lax.rsqrt(ms + RMS_EPS) * g


def _rms_plain(x):
    ms = jnp.mean(x * x, axis=-1, keepdims=True)
    return x * lax.rsqrt(ms + RMS_EPS)


def _gelu_tanh(x):
    k1 = -2.0 * math.sqrt(2.0 / math.pi) * math.log2(math.e)
    k2 = k1 * 0.044715
    return x / (1.0 + jnp.exp2(x * (k1 + k2 * (x * x))))


def _dot(a, b):
    return jnp.dot(a, b, preferred_element_type=F32)


def _dot_nt(a, b):
    return lax.dot_general(a, b, (((1,), (1,)), ((), ())), preferred_element_type=F32)


def _const_spec(shape):
    return pl.BlockSpec(shape, lambda *_: (0,) * len(shape))


def _shift_rows(x, prev, d):
    r = pltpu.roll(x, d, 0)
    row = lax.broadcasted_iota(jnp.int32, prev.shape, 0)
    head = jnp.where(row < d, pltpu.roll(prev, d, 0), r[0:SUBLANES])
    return jnp.concatenate([head, r[SUBLANES:]], axis=0)


def _linear_scan(a, u, h0):
    t, c = a.shape
    groups = t // SUBLANES
    a = a.reshape(groups, SUBLANES, c)
    u = u.reshape(groups, SUBLANES, c)
    row = lax.broadcasted_iota(jnp.int32, a.shape, 1)
    for s in (1, 2, 4):
        a_s = pltpu.roll(a, s, 1)
        u_s = pltpu.roll(u, s, 1)
        m = row >= s
        u = jnp.where(m, a * u_s + u, u)
        a = jnp.where(m, a * a_s, a)
    hs = []
    h = h0
    for g in range(groups):
        hg = a[g] * h + u[g]
        hs.append(hg)
        h = hg[SUBLANES - 1:SUBLANES]
    return jnp.concatenate(hs, axis=0)


def _rope(t, cos, sin):
    half = QK_ROPE_DIM // 2
    return t * cos + pltpu.roll(t, half, 1) * sin - pltpu.roll(t, LANES - half, 1) * sin


def _in_proj_kernel_old(x_ref, pos_ref, invf_ref, win_ref, convw_ref, convb_ref, wg_ref,
                    ba_ref, bx_ref, lam_ref, wuqt_ref, wkn_ref, wvt_ref,
                    qt_ref, k_ref, vt_ref, ylru_ref, xprev_ref, hprev_ref):
    ts = IN_SUB
    tiles = range(x_ref.shape[0] // ts)
    rows = [slice(n * ts, (n + 1) * ts) for n in tiles]
    half = QK_ROPE_DIM // 2

    @pl.when(pl.program_id(1) == 0)
    def _():
        xprev_ref[...] = jnp.zeros_like(xprev_ref)
        hprev_ref[...] = jnp.zeros_like(hprev_ref)

    xn = [_rms_plain(x_ref[r, :]).astype(BF16) for r in rows]
    z_tiles = [{} for _ in tiles]

    def z(n, lo, hi):
        parts = []
        for j in range(lo // MXU_TILE, -(-hi // MXU_TILE)):
            if j not in z_tiles[n]:
                z_tiles[n][j] = _dot(xn[n], win_ref[:, j * MXU_TILE:(j + 1) * MXU_TILE])
            a, b = max(lo, j * MXU_TILE), min(hi, (j + 1) * MXU_TILE)
            parts.append(z_tiles[n][j][:, a - j * MXU_TILE:b - j * MXU_TILE])
        return parts[0] if len(parts) == 1 else jnp.concatenate(parts, axis=-1)

    xl = [z(n, Z_LRU_X, Z_LRU_GATE) for n in tiles]
    u = []
    prev = xprev_ref[...]
    for n in tiles:
        un = convb_ref[...] + convw_ref[CONV_WIDTH - 1:CONV_WIDTH, :] * xl[n]
        for d in range(1, CONV_WIDTH):
            k = CONV_WIDTH - 1 - d
            un = un + convw_ref[k:k + 1, :] * _shift_rows(xl[n], prev, d)
        u.append(un)
        prev = xl[n][ts - SUBLANES:ts]
    xprev_ref[...] = prev
    pre = [[_dot(u[n][:, j * LRU_GROUP:(j + 1) * LRU_GROUP].astype(BF16), wg_ref[j])
            for j in range(D_LRU // LRU_GROUP)] for n in tiles]

    z_gate = [z(n, Z_LRU_GATE, Z_CQ) for n in tiles]
    z_cq = [z(n, Z_CQ, Z_CKV) for n in tiles]
    z_ckv = [z(n, Z_CKV, Z_KPE) for n in tiles]
    z_kpe = [z(n, Z_KPE, Z_WIDTH) for n in tiles]
    cqn = [_rms_plain(v) for v in z_cq]
    ckvn = [_rms_plain(v) for v in z_ckv]
    q_t = [_dot(wuqt_ref[...], v.T.astype(BF16)) for v in cqn]
    v_t = [_dot(wvt_ref[...], v.T.astype(BF16)) for v in ckvn]
    kn = [_dot(v.astype(BF16), wkn_ref[...]) for v in ckvn]

    nlam = -lam_ref[...]
    softplus = jnp.maximum(nlam, 0.0) + jnp.log1p(jnp.exp(-jnp.abs(nlam)))
    h_last = hprev_ref[SUBLANES - 1:SUBLANES, :]
    for n in tiles:
        r = 1.0 / (1.0 + jnp.exp2(
            jnp.concatenate([p[:, :LRU_GROUP] for p in pre[n]], axis=-1) + ba_ref[...]))
        gi = 1.0 / (1.0 + jnp.exp2(
            jnp.concatenate([p[:, LRU_GROUP:] for p in pre[n]], axis=-1) + bx_ref[...]))
        a = jnp.exp2(r * ((-LRU_C * math.log2(math.e)) * softplus))
        one_m_a2 = 1.0 - a * a
        root = jnp.where(one_m_a2 > 0.0, one_m_a2 * lax.rsqrt(one_m_a2), 0.0)
        h = _linear_scan(a, root * (gi * u[n]), h_last)
        h_last = h[ts - 1:ts]
        if n == tiles[-1]:
            hprev_ref[...] = h[ts - SUBLANES:ts]
        ylru_ref[rows[n], :] = _rms_plain(h * _gelu_tanh(z_gate[n])).astype(BF16)

    for n in tiles:
        ang_t = invf_ref[...] * pos_ref[:, rows[n]].astype(F32)
        cos_t = jnp.cos(ang_t)
        sin_t = jnp.sin(ang_t)
        t_out = n * ts // TS_IN
        cols = slice(n * ts % TS_IN, n * ts % TS_IN + ts)
        for hd in range(MLA_HEADS):
            r0 = hd * HEAD_SLAB
            pe = r0 + QK_NOPE_DIM
            x1 = q_t[n][pe:pe + half]
            x2 = q_t[n][pe + half:pe + 2 * half]
            qt_ref[t_out, r0:pe, cols] = q_t[n][r0:pe].astype(BF16)
            qt_ref[t_out, pe:pe + half, cols] = (x1 * cos_t - x2 * sin_t).astype(BF16)
            qt_ref[t_out, pe + half:pe + 2 * half, cols] = (x2 * cos_t + x1 * sin_t).astype(BF16)
            qt_ref[t_out, pe + 2 * half:r0 + HEAD_SLAB, cols] = jnp.zeros(
                (HEAD_SLAB - QK_NOPE_DIM - 2 * half, ts), BF16)
            v0 = hd * VT_ROWS
            vt_ref[t_out, v0:v0 + V_HEAD_DIM, cols] = (
                v_t[n][hd * V_HEAD_DIM:(hd + 1) * V_HEAD_DIM].astype(BF16))
            vt_ref[t_out, v0 + V_HEAD_DIM:v0 + VT_ROWS, cols] = jnp.ones((BF16_ROWS, ts), BF16)
        pad = jnp.zeros((LANES - 2 * half, ts), F32)
        cos = jnp.concatenate([cos_t, cos_t, pad], axis=0).T
        sin = jnp.concatenate([sin_t, sin_t, pad], axis=0).T
        kpe = _rope(z_kpe[n], cos, sin).astype(BF16)
        for hd in range(MLA_HEADS):
            c0 = hd * HEAD_SLAB
            k_ref[rows[n], c0:c0 + LANES] = kn[n][:, hd * LANES:(hd + 1) * LANES].astype(BF16)
            k_ref[rows[n], c0 + LANES:c0 + HEAD_SLAB] = kpe


def _in_proj_old(x, pos_row, invf, win, convw, convb, wg, ba, bx, lam, wuqt, wkn, wvt):
    bsz, s, _ = x.shape
    ts = TS_IN
    step = TS_IN * IN_TILES_PER_STEP
    tok = lambda w: pl.BlockSpec((None, step, w), lambda b, i: (b, i, 0))
    in_specs = [
        tok(D_MODEL), pl.BlockSpec((None, 1, step), lambda b, i: (b, 0, i)),
        _const_spec((QK_ROPE_DIM // 2, 1)),
        _const_spec((D_MODEL, Z_WIDTH)), _const_spec((CONV_WIDTH, D_LRU)), _const_spec((1, D_LRU)),
        _const_spec((D_LRU // LRU_GROUP, LRU_GROUP, 2 * LRU_GROUP)),
        _const_spec((1, D_LRU)), _const_spec((1, D_LRU)), _const_spec((1, D_LRU)),
        _const_spec((MLA_HEADS * HEAD_SLAB, Q_LORA_RANK)),
        _const_spec((KV_LORA_RANK, MLA_HEADS * QK_NOPE_DIM)),
        _const_spec((D_MLA_OUT, KV_LORA_RANK)),
    ]
    tile_t = lambda w: pl.BlockSpec((None, IN_TILES_PER_STEP, w, ts), lambda b, i: (b, i, 0, 0))
    out_shape = (
        jax.ShapeDtypeStruct((bsz, s // ts, MLA_HEADS * HEAD_SLAB, ts), BF16),
        jax.ShapeDtypeStruct((bsz, s, MLA_HEADS * HEAD_SLAB), BF16),
        jax.ShapeDtypeStruct((bsz, s // ts, MLA_HEADS * VT_ROWS, ts), BF16),
        jax.ShapeDtypeStruct((bsz, s, D_LRU), BF16),
    )
    out_specs = (tile_t(MLA_HEADS * HEAD_SLAB), tok(MLA_HEADS * HEAD_SLAB),
                 tile_t(MLA_HEADS * VT_ROWS), tok(D_LRU))
    return pl.pallas_call(
        _in_proj_kernel,
        out_shape=out_shape,
        grid=(bsz, s // step),
        in_specs=in_specs,
        out_specs=out_specs,
        scratch_shapes=[pltpu.VMEM((SUBLANES, D_LRU), F32), pltpu.VMEM((SUBLANES, D_LRU), F32)],
        compiler_params=pltpu.CompilerParams(dimension_semantics=("parallel", "arbitrary"),
                                             vmem_limit_bytes=VMEM_LIMIT),
        name="in_proj",
    )(x, pos_row, invf, win, convw, convb, wg, ba, bx, lam, wuqt, wkn, wvt)


def _in_proj_kernel(x_ref, pos_ref, invf_ref, win_ref, convw_ref, convb_ref, wg_ref,
                    ba_ref, bx_ref, lam_ref, wuqt_ref, wkn_ref, wvt_ref,
                    qt_ref, k_ref, vt_ref, ylru_ref, xs_ref, hs_ref, xprev_ref, hprev_ref):
    nb, tt, _ = x_ref.shape
    rows = nb * tt
    half = QK_ROPE_DIM // 2
    lane_tiles = D_LRU // LANES

    @pl.when(pl.program_id(1) == 0)
    def _():
        xprev_ref[...] = jnp.zeros_like(xprev_ref)
        hprev_ref[...] = jnp.zeros_like(hprev_ref)

    xn = _rms_plain(x_ref[...].reshape(rows, D_MODEL)).astype(BF16)
    z_tiles = {}

    def z(lo, hi):
        parts = []
        for j in range(lo // MXU_TILE, -(-hi // MXU_TILE)):
            if j not in z_tiles:
                z_tiles[j] = _dot(xn, win_ref[:, j * MXU_TILE:(j + 1) * MXU_TILE])
            a, b = max(lo, j * MXU_TILE), min(hi, (j + 1) * MXU_TILE)
            parts.append(z_tiles[j][:, a - j * MXU_TILE:b - j * MXU_TILE])
        return parts[0] if len(parts) == 1 else jnp.concatenate(parts, axis=-1)

    xl = z(Z_LRU_X, Z_LRU_GATE)
    for c in range(lane_tiles):
        for b in range(nb):
            xs_ref[c, pl.ds(b, tt, stride=nb), :] = xl[b * tt:(b + 1) * tt, c * LANES:(c + 1) * LANES]
    x_tm = jnp.concatenate([xs_ref[c] for c in range(lane_tiles)], axis=-1)
    ext = jnp.concatenate([xprev_ref[...], x_tm], axis=0)
    xprev_ref[...] = x_tm[rows - (CONV_WIDTH - 1) * nb:]
    u = convb_ref[...]
    for k in range(CONV_WIDTH):
        u = u + convw_ref[k:k + 1, :] * ext[k * nb:k * nb + rows]
    pre = [_dot(u[:, j * LRU_GROUP:(j + 1) * LRU_GROUP].astype(BF16), wg_ref[j])
           for j in range(D_LRU // LRU_GROUP)]

    z_gate = z(Z_LRU_GATE, Z_CQ)
    z_cq = z(Z_CQ, Z_CKV)
    z_ckv = z(Z_CKV, Z_KPE)
    z_kpe = z(Z_KPE, Z_WIDTH)
    ckvn = _rms_plain(z_ckv)
    q_t = _dot(wuqt_ref[...], _rms_plain(z_cq).T.astype(BF16))
    v_t = _dot(wvt_ref[...], ckvn.T.astype(BF16))
    kn = _dot(ckvn.astype(BF16), wkn_ref[...])

    nlam = -lam_ref[...]
    softplus = jnp.maximum(nlam, 0.0) + jnp.log1p(jnp.exp(-jnp.abs(nlam)))
    r = 1.0 / (1.0 + jnp.exp2(jnp.concatenate([p[:, :LRU_GROUP] for p in pre], axis=-1) + ba_ref[...]))
    gi = 1.0 / (1.0 + jnp.exp2(jnp.concatenate([p[:, LRU_GROUP:] for p in pre], axis=-1) + bx_ref[...]))
    a = jnp.exp2(r * ((-LRU_C * math.log2(math.e)) * softplus))
    one_m_a2 = 1.0 - a * a
    root = jnp.where(one_m_a2 > 0.0, one_m_a2 * lax.rsqrt(one_m_a2), 0.0)
    v_in = root * (gi * u)
    h = hprev_ref[...]
    hs = []
    for t in range(tt):
        h = a[t * nb:(t + 1) * nb] * h + v_in[t * nb:(t + 1) * nb]
        hs.append(h)
    hprev_ref[...] = h
    h_tm = jnp.concatenate(hs, axis=0)
    for c in range(lane_tiles):
        hs_ref[c] = h_tm[:, c * LANES:(c + 1) * LANES]
    h_bt = jnp.concatenate(
        [jnp.concatenate([hs_ref[c, pl.ds(b, tt, stride=nb), :] for c in range(lane_tiles)], axis=-1)
         for b in range(nb)], axis=0)
    ylru_ref[...] = _rms_plain(h_bt * _gelu_tanh(z_gate)).astype(BF16).reshape(nb, tt, D_LRU)

    ang_t = invf_ref[...] * pos_ref[...].astype(F32)
    cos_t = jnp.cos(ang_t)
    sin_t = jnp.sin(ang_t)
    q_slabs, v_slabs = [], []
    for hd in range(MLA_HEADS):
        r0 = hd * HEAD_SLAB
        pe = r0 + QK_NOPE_DIM
        x1 = q_t[pe:pe + half]
        x2 = q_t[pe + half:pe + 2 * half]
        q_slabs += [q_t[r0:pe].astype(BF16),
                    (x1 * cos_t - x2 * sin_t).astype(BF16), (x2 * cos_t + x1 * sin_t).astype(BF16),
                    jnp.zeros((HEAD_SLAB - QK_NOPE_DIM - 2 * half, rows), BF16)]
        v_slabs += [v_t[hd * V_HEAD_DIM:(hd + 1) * V_HEAD_DIM].astype(BF16),
                    jnp.ones((BF16_ROWS, rows), BF16)]
    q_rot = jnp.concatenate(q_slabs, axis=0)
    v_aug = jnp.concatenate(v_slabs, axis=0)
    for b in range(nb):
        qt_ref[b] = q_rot[:, b * tt:(b + 1) * tt]
        vt_ref[b] = v_aug[:, b * tt:(b + 1) * tt]
    pad = jnp.zeros((LANES - 2 * half, rows), F32)
    cos = jnp.concatenate([cos_t, cos_t, pad], axis=0).T
    sin = jnp.concatenate([sin_t, sin_t, pad], axis=0).T
    kpe = _rope(z_kpe, cos, sin).astype(BF16)
    k_slabs = []
    for hd in range(MLA_HEADS):
        k_slabs += [kn[:, hd * LANES:(hd + 1) * LANES].astype(BF16), kpe]
    k_ref[...] = jnp.concatenate(k_slabs, axis=-1).reshape(nb, tt, MLA_HEADS * HEAD_SLAB)


def _in_proj(x, pos_rows, invf, win, convw, convb, wg, ba, bx, lam, wuqt, wkn, wvt):
    bsz, s, _ = x.shape
    nb, tt = IN_BATCHES, IN_STEPS
    per_tile = TS_IN // tt
    tok = lambda w: pl.BlockSpec((nb, tt, w), lambda g, i: (g, i, 0))
    in_specs = [
        tok(D_MODEL), pl.BlockSpec((None, None, 1, nb * tt), lambda g, i: (g, i, 0, 0)),
        _const_spec((QK_ROPE_DIM // 2, 1)),
        _const_spec((D_MODEL, Z_WIDTH)), _const_spec((CONV_WIDTH, D_LRU)), _const_spec((1, D_LRU)),
        _const_spec((D_LRU // LRU_GROUP, LRU_GROUP, 2 * LRU_GROUP)),
        _const_spec((1, D_LRU)), _const_spec((1, D_LRU)), _const_spec((1, D_LRU)),
        _const_spec((MLA_HEADS * HEAD_SLAB, Q_LORA_RANK)),
        _const_spec((KV_LORA_RANK, MLA_HEADS * QK_NOPE_DIM)),
        _const_spec((D_MLA_OUT, KV_LORA_RANK)),
    ]
    tile_t = lambda w: pl.BlockSpec((nb, None, w, tt), lambda g, i: (g, i // per_tile, 0, i % per_tile))
    out_shape = (
        jax.ShapeDtypeStruct((bsz, s // TS_IN, MLA_HEADS * HEAD_SLAB, TS_IN), BF16),
        jax.ShapeDtypeStruct((bsz, s, MLA_HEADS * HEAD_SLAB), BF16),
        jax.ShapeDtypeStruct((bsz, s // TS_IN, MLA_HEADS * VT_ROWS, TS_IN), BF16),
        jax.ShapeDtypeStruct((bsz, s, D_LRU), BF16),
    )
    out_specs = (tile_t(MLA_HEADS * HEAD_SLAB), tok(MLA_HEADS * HEAD_SLAB),
                 tile_t(MLA_HEADS * VT_ROWS), tok(D_LRU))
    return pl.pallas_call(
        _in_proj_kernel,
        out_shape=out_shape,
        grid=(bsz // nb, s // tt),
        in_specs=in_specs,
        out_specs=out_specs,
        scratch_shapes=[pltpu.VMEM((D_LRU // LANES, nb * tt, LANES), F32),
                        pltpu.VMEM((D_LRU // LANES, nb * tt, LANES), F32),
                        pltpu.VMEM(((CONV_WIDTH - 1) * nb, D_LRU), F32),
                        pltpu.VMEM((nb, D_LRU), F32)],
        compiler_params=pltpu.CompilerParams(dimension_semantics=("parallel", "arbitrary"),
                                             vmem_limit_bytes=VMEM_LIMIT),
        name="in_proj",
    )(x, pos_rows, invf, win, convw, convb, wg, ba, bx, lam, wuqt, wkn, wvt)


def _mla_attn_kernel(qt_ref, k_ref, vt_ref, o_ref):
    n_local, _, tq = qt_ref.shape
    nq = vt_ref.shape[0]
    i = pl.program_id(1)
    hq = tq // 2

    def scores(ql, j, hd, diagonal):
        slab = slice(hd * HEAD_SLAB, (hd + 1) * HEAD_SLAB)
        if not diagonal:
            st = _dot(k_ref[j * tq:(j + 1) * tq, slab], qt_ref[ql, slab, :])
            return [(0, 0, False, st[:, :hq]), (1, 0, False, st[:, hq:])]
        return [(0, 0, True, _dot(k_ref[j * tq:j * tq + hq, slab], qt_ref[ql, slab, 0:hq])),
                (1, 0, True, _dot(k_ref[j * tq:(j + 1) * tq, slab], qt_ref[ql, slab, hq:tq]))]

    def finish(ql, acc):
        outs = []
        for hd in range(MLA_HEADS):
            a = jnp.concatenate(acc[hd], axis=1)
            outs.append(a[:V_HEAD_DIM] * (1.0 / a[V_HEAD_DIM:V_HEAD_DIM + 1]))
        out_t = jnp.concatenate(outs, axis=0)
        ms = jnp.mean(out_t * out_t, axis=0, keepdims=True)
        o_ref[ql * tq:(ql + 1) * tq, :] = (out_t * lax.rsqrt(ms + RMS_EPS)).astype(BF16).T

    def attend(first_tile):
        units = [(ql, j, hd, j == first_tile + ql)
                 for ql in range(n_local) for j in range(first_tile + ql + 1)
                 for hd in range(MLA_HEADS)]
        ahead = [scores(*u) for u in units[:SCORE_LOOKAHEAD]]
        m = acc = None
        for n, (ql, j, hd, diagonal) in enumerate(units):
            if j == 0 and hd == 0:
                m = [[jnp.full((1, hq), NEG_INF, F32)] * 2 for _ in range(MLA_HEADS)]
                acc = [[None, None] for _ in range(MLA_HEADS)]
            pieces = ahead.pop(0)
            if n + SCORE_LOOKAHEAD < len(units):
                ahead.append(scores(*units[n + SCORE_LOOKAHEAD]))
            vrows = slice(hd * VT_ROWS, (hd + 1) * VT_ROWS)
            for half, k0, masked, st in pieces:
                nk = st.shape[0]
                if masked:
                    kid = lax.broadcasted_iota(jnp.int32, (nk, hq), 0) + k0
                    qid = lax.broadcasted_iota(jnp.int32, (nk, hq), 1) + half * hq
                    st = jnp.where(qid >= kid, st, NEG_INF)
                m_old = m[hd][half]
                m_new = jnp.maximum(m_old, jnp.max(st, axis=0, keepdims=True))
                p = jnp.exp2(st - m_new).astype(BF16)
                pv = _dot(vt_ref[j, vrows, k0:k0 + nk], p)
                old = acc[hd][half]
                acc[hd][half] = pv if old is None else jnp.exp2(m_old - m_new) * old + pv
                m[hd][half] = m_new
            if diagonal and hd == MLA_HEADS - 1:
                finish(ql, acc)

    for n in range(nq // n_local):
        @pl.when(i == n)
        def _(n=n):
            attend(n * n_local)


def _mla_attn(qt, k, vt):
    bsz, nq, _, tq = qt.shape
    s = k.shape[1]
    per = Q_TILES_PER_STEP
    return pl.pallas_call(
        _mla_attn_kernel,
        out_shape=jax.ShapeDtypeStruct((bsz, s, D_MLA_OUT), BF16),
        grid=(bsz, nq // per),
        in_specs=[
            pl.BlockSpec((None, per, MLA_HEADS * HEAD_SLAB, tq), lambda b, i: (b, i, 0, 0)),
            pl.BlockSpec((None, s, MLA_HEADS * HEAD_SLAB), lambda b, i: (b, 0, 0)),
            pl.BlockSpec((None, nq, MLA_HEADS * VT_ROWS, tq), lambda b, i: (b, 0, 0, 0)),
        ],
        out_specs=pl.BlockSpec((None, per * tq, D_MLA_OUT), lambda b, i: (b, i, 0)),
        compiler_params=pltpu.CompilerParams(dimension_semantics=("parallel", "arbitrary"),
                                             vmem_limit_bytes=VMEM_LIMIT),
        name="mla_attn",
    )(qt, k, vt)


def _mix_mem_kernel(ylru_ref, ymla_ref, x_ref, mem_ref, wout_ref, gpost_ref, gpre_ref, wmq_ref,
                    gmem_ref, wmk_ref, wmv_ref, wmo_ref, gpm_ref, h_ref, o_scr, mk_ref, mv_ref):
    first = pl.program_id(1) == 0
    for build_mem in (True, False):
        @pl.when(first == build_mem)
        def _(build_mem=build_mem):
            subs = [slice(r, r + ROW_SUB) for r in range(0, x_ref.shape[0], ROW_SUB)]
            yo = [_dot(ylru_ref[r, :], wout_ref[0:D_LRU, :])
                  + _dot(ymla_ref[r, :], wout_ref[D_LRU:, :]) for r in subs]
            h1 = [x_ref[r, :] + _rms(y, gpost_ref[...]) for r, y in zip(subs, yo)]

            xn = [_rms(h, gpre_ref[...]).astype(BF16) for h in h1]
            mq = [(_dot(v, wmq_ref[...]) * (1.0 / math.sqrt(MEM_HEAD_DIM))).astype(BF16)
                  for v in xn]
            if build_mem:
                mn = _rms(mem_ref[...], gmem_ref[...]).astype(BF16)
                mk = _dot(mn, wmk_ref[...]).astype(BF16)
                mv = _dot(mn, wmv_ref[...]).astype(BF16)
                mk_ref[...] = mk
                mv_ref[...] = mv
            else:
                mk, mv = mk_ref, mv_ref
            for r, q in zip(subs, mq):
                for hd in range(MEM_HEADS):
                    sl = slice(hd * MEM_HEAD_DIM, (hd + 1) * MEM_HEAD_DIM)
                    s = _dot_nt(q[:, sl], mk[:, sl])
                    m = jnp.max(s, axis=-1, keepdims=True)
                    p = jnp.exp(s - m)
                    l = jnp.sum(p, axis=-1, keepdims=True)
                    o_scr[r, sl] = (_dot(p.astype(BF16), mv[:, sl]) / l).astype(BF16)
            mo = [_dot(o_scr[r, :], wmo_ref[...]) for r in subs]
            for r, h, o in zip(subs, h1, mo):
                h_ref[r, :] = h + _rms(o, gpm_ref[...])


def _mix_mem(ylru, ymla, x, mem, wout, gpost, gpre, wmq, gmem, wmk, wmv, wmo, gpm):
    bsz, s, _ = x.shape
    tm = TM_MIX
    tok = lambda w: pl.BlockSpec((None, tm, w), lambda b, i: (b, i, 0))
    memspec = pl.BlockSpec((None, N_MEM, D_MODEL), lambda b, i: (b, 0, 0))
    sq = pl.BlockSpec((D_MODEL, D_MODEL), lambda b, i: (0, 0), pipeline_mode=pl.Buffered(1))
    vec = _const_spec((1, D_MODEL))
    return pl.pallas_call(
        _mix_mem_kernel,
        out_shape=jax.ShapeDtypeStruct((bsz, s, D_MODEL), F32),
        grid=(bsz, s // tm),
        in_specs=[tok(D_LRU), tok(D_MLA_OUT), tok(D_MODEL), memspec, sq, vec, vec, sq, vec, sq, sq,
                  sq, vec],
        out_specs=tok(D_MODEL),
        scratch_shapes=[pltpu.VMEM((tm, D_MODEL), BF16), pltpu.VMEM((N_MEM, D_MODEL), BF16),
                        pltpu.VMEM((N_MEM, D_MODEL), BF16)],
        compiler_params=pltpu.CompilerParams(dimension_semantics=("parallel", "arbitrary"),
                                             vmem_limit_bytes=VMEM_LIMIT),
        name="mix_mem",
    )(ylru, ymla, x, mem, wout, gpost, gpre, wmq, gmem, wmk, wmv, wmo, gpm)


def _ffn_kernel(h_ref, gpre_ref, wg_ref, wu_ref, wd_ref, gpost_ref, o_ref):
    subs = [slice(r, r + ROW_SUB) for r in range(0, h_ref.shape[0], ROW_SUB)]
    xn = [_rms(h_ref[r, :], gpre_ref[...]).astype(BF16) for r in subs]
    f = [None] * len(subs)
    for lo, hi in FF_CHUNKS:
        for n, v in enumerate(xn):
            act = (jax.nn.silu(_dot(v, wg_ref[:, lo:hi])) * _dot(v, wu_ref[:, lo:hi])).astype(BF16)
            part = _dot(act, wd_ref[lo:hi, :])
            f[n] = part if f[n] is None else f[n] + part
    for r, fr in zip(subs, f):
        o_ref[r, :] = h_ref[r, :] + _rms(fr, gpost_ref[...])


def _ffn(h, gpre, wg, wu, wd, gpost):
    m, _ = h.shape
    tm = TM_FFN
    single = pl.Buffered(1)
    tok = pl.BlockSpec((tm, D_MODEL), lambda i: (i, 0))
    vec = _const_spec((1, D_MODEL))
    return pl.pallas_call(
        _ffn_kernel,
        out_shape=jax.ShapeDtypeStruct((m, D_MODEL), F32),
        grid=(m // tm,),
        in_specs=[
            tok, vec,
            pl.BlockSpec((D_MODEL, D_FF), lambda i: (0, 0), pipeline_mode=single),
            pl.BlockSpec((D_MODEL, D_FF), lambda i: (0, 0), pipeline_mode=single),
            pl.BlockSpec((D_FF, D_MODEL), lambda i: (0, 0), pipeline_mode=single),
            vec,
        ],
        out_specs=tok,
        compiler_params=pltpu.CompilerParams(dimension_semantics=("parallel",),
                                             vmem_limit_bytes=VMEM_LIMIT),
        name="ffn",
    )(h, gpre, wg, wu, wd, gpost)


def _block_diag(w):
    eye = jnp.eye(LRU_BLOCKS, dtype=w.dtype)
    return jnp.einsum('hij,hg->higj', w, eye).reshape(D_LRU, D_LRU)


def _gate_weights(wa, wx):
    da, dx = _block_diag(wa) * -math.log2(math.e), _block_diag(wx) * -math.log2(math.e)
    groups = []
    for j in range(D_LRU // LRU_GROUP):
        sl = slice(j * LRU_GROUP, (j + 1) * LRU_GROUP)
        groups.append(jnp.concatenate([da[sl, sl], dx[sl, sl]], axis=1))
    return jnp.stack(groups).astype(BF16)


def _pad_in_proj(w_in, gain):
    w = gain[:, None] * w_in
    pad = jnp.zeros((D_MODEL, Z_WIDTH - w.shape[1]), w.dtype)
    return jnp.concatenate([w, pad], axis=1).astype(BF16)


def _q_weights_t(w_uq, gain):
    dh = QK_NOPE_DIM + QK_ROPE_DIM
    scale = math.log2(math.e) / math.sqrt(dh)
    w = (scale * gain[:, None] * w_uq).reshape(Q_LORA_RANK, MLA_HEADS, dh)
    pad = jnp.zeros((Q_LORA_RANK, MLA_HEADS, HEAD_SLAB - dh), w.dtype)
    w = jnp.concatenate([w, pad], axis=-1).reshape(Q_LORA_RANK, MLA_HEADS * HEAD_SLAB)
    return w.T.astype(BF16)


def _kv_weights(w_ukv, gain):
    w = (gain[:, None] * w_ukv).reshape(KV_LORA_RANK, MLA_HEADS, QK_NOPE_DIM + V_HEAD_DIM)
    kn = w[:, :, :QK_NOPE_DIM].reshape(KV_LORA_RANK, MLA_HEADS * QK_NOPE_DIM)
    vv = w[:, :, QK_NOPE_DIM:].reshape(KV_LORA_RANK, MLA_HEADS * V_HEAD_DIM)
    return kn.astype(BF16), vv.T.astype(BF16)


def _rope_freq_column():
    inv_freq = ROPE_THETA ** (-jnp.arange(0, QK_ROPE_DIM, 2, dtype=F32) / QK_ROPE_DIM)
    return inv_freq.reshape(QK_ROPE_DIM // 2, 1)


def kernel(x, mem, positions, g_pre_mix, w_in, conv_w, conv_b, lru_wa, lru_ba, lru_wx, lru_bx, lru_lambda, g_q_lat, w_uq, g_kv_lat, w_ukv, g_lru_out, g_mla_out, w_out, g_post_mix, g_pre_mem, g_mem_kv, w_mq, w_mk, w_mv, w_mo, g_post_mem, g_pre_ffn, w_gate, w_up, w_down, g_post_ffn):
    bsz, s, d = x.shape
    depth = w_in.shape[0]
    assert d == D_MODEL and s % TS_IN == 0 and s % TM_MIX == 0
    assert (bsz * s) % TM_FFN == 0 and mem.shape == (bsz, N_MEM, D_MODEL)
    pos_rows = positions.reshape(bsz // IN_BATCHES, IN_BATCHES, s // IN_STEPS, IN_STEPS)
    pos_rows = pos_rows.transpose(0, 2, 1, 3).reshape(bsz // IN_BATCHES, s // IN_STEPS, 1, -1)
    invf = _rope_freq_column()
    row = lambda a: a.reshape(1, -1)

    h = x
    for l in range(depth):
        wkn, wvt = _kv_weights(w_ukv[l], g_kv_lat[l])
        neg_log2e = -math.log2(math.e)
        qt, k, vt, ylru = _in_proj(
            h, pos_rows, invf, _pad_in_proj(w_in[l], g_pre_mix[l]), conv_w[l], row(conv_b[l]),
            _gate_weights(lru_wa[l], lru_wx[l]), row(neg_log2e * lru_ba[l]),
            row(neg_log2e * lru_bx[l]), row(lru_lambda[l]),
            _q_weights_t(w_uq[l], g_q_lat[l]), wkn, wvt)
        ymla = _mla_attn(qt, k, vt)
        wout = (jnp.concatenate([g_lru_out[l], g_mla_out[l]])[:, None] * w_out[l]).astype(BF16)
        h = _mix_mem(ylru, ymla, h, mem, wout, row(g_post_mix[l]), row(g_pre_mem[l]),
                     w_mq[l].astype(BF16), row(g_mem_kv[l]), w_mk[l].astype(BF16),
                     w_mv[l].astype(BF16), w_mo[l].astype(BF16), row(g_post_mem[l]))
        h = _ffn(h.reshape(bsz * s, d), row(g_pre_ffn[l]), w_gate[l].astype(BF16),
                 w_up[l].astype(BF16), w_down[l].astype(BF16), row(g_post_ffn[l])).reshape(bsz, s, d)
    return h
```

```python
import math

import jax
import jax.numpy as jnp
from jax import lax
from jax.experimental import pallas as pl
from jax.experimental.pallas import tpu as pltpu

F32 = jnp.float32
BF16 = jnp.bfloat16

D_MODEL = 1024
N_MEM = 256
MEM_HEADS = 4
MEM_HEAD_DIM = D_MODEL // MEM_HEADS
D_LRU = 512
LRU_BLOCKS = 8
LRU_BLOCK_DIM = D_LRU // LRU_BLOCKS
CONV_WIDTH = 4
LRU_C = 8.0
MLA_HEADS = 4
QK_NOPE_DIM = 128
QK_ROPE_DIM = 64
V_HEAD_DIM = 128
Q_LORA_RANK = 384
KV_LORA_RANK = 256
D_MLA_OUT = MLA_HEADS * V_HEAD_DIM
ROPE_THETA = 10000.0
D_FF = 2816
RMS_EPS = 1e-6
NEG_INF = -1e30

LANES = 128
SUBLANES = 8
HEAD_SLAB = 2 * LANES
BF16_ROWS = 16
VT_ROWS = V_HEAD_DIM + BF16_ROWS
Z_LRU_X = 0
Z_LRU_GATE = Z_LRU_X + D_LRU
Z_CQ = Z_LRU_GATE + D_LRU
Z_CKV = Z_CQ + Q_LORA_RANK
Z_KPE = Z_CKV + KV_LORA_RANK
Z_WIDTH = Z_KPE + LANES
MXU_TILE = 256
LRU_GROUP = MXU_TILE

TS_IN = 512
IN_TILES_PER_STEP = 2
Q_TILES_PER_STEP = 4
SCORE_LOOKAHEAD = 2
TM_MIX = 1024
TM_FFN = 1024
ROW_SUB = 256
FF_CHUNKS = ((0, 1536), (1536, D_FF))
VMEM_LIMIT = 56 * 1024 * 1024


def _rms(x, g):
    ms = jnp.mean(x * x, axis=-1, keepdims=True)
    return x * lax.rsqrt(ms + RMS_EPS) * g


def _rms_plain(x):
    ms = jnp.mean(x * x, axis=-1, keepdims=True)
    return x * lax.rsqrt(ms + RMS_EPS)


def _gelu_tanh(x):
    k1 = -2.0 * math.sqrt(2.0 / math.pi) * math.log2(math.e)
    k2 = k1 * 0.044715
    return x / (1.0 + jnp.exp2(x * (k1 + k2 * (x * x))))


def _dot(a, b):
    return jnp.dot(a, b, preferred_element_type=F32)


def _dot_nt(a, b):
    return lax.dot_general(a, b, (((1,), (1,)), ((), ())), preferred_element_type=F32)


def _const_spec(shape):
    return pl.BlockSpec(shape, lambda *_: (0,) * len(shape))


def _shift_rows(x, prev, d):
    r = pltpu.roll(x, d, 0)
    row = lax.broadcasted_iota(jnp.int32, prev.shape, 0)
    head = jnp.where(row < d, pltpu.roll(prev, d, 0), r[0:SUBLANES])
    return jnp.concatenate([head, r[SUBLANES:]], axis=0)


def _linear_scan(a, u, h0):
    t, c = a.shape
    groups = t // SUBLANES
    a = a.reshape(groups, SUBLANES, c)
    u = u.reshape(groups, SUBLANES, c)
    row = lax.broadcasted_iota(jnp.int32, a.shape, 1)
    for s in (1, 2, 4):
        a_s = pltpu.roll(a, s, 1)
        u_s = pltpu.roll(u, s, 1)
        m = row >= s
        u = jnp.where(m, a * u_s + u, u)
        a = jnp.where(m, a * a_s, a)
    hs = []
    h = h0
    for g in range(groups):
        hg = a[g] * h + u[g]
        hs.append(hg)
        h = hg[SUBLANES - 1:SUBLANES]
    return jnp.concatenate(hs, axis=0)


def _rope(t, cos, sin):
    half = QK_ROPE_DIM // 2
    return t * cos + pltpu.roll(t, half, 1) * sin - pltpu.roll(t, LANES - half, 1) * sin


def _in_proj_kernel(x_ref, pos_ref, invf_ref, win_ref, convw_ref, convb_ref, wg_ref,
                    ba_ref, bx_ref, lam_ref, wuqt_ref, wkn_ref, wvt_ref,
                    qt_ref, k_ref, vt_ref, ylru_ref, xprev_ref, hprev_ref):
    ts = TS_IN
    tiles = range(x_ref.shape[0] // ts)
    rows = [slice(n * ts, (n + 1) * ts) for n in tiles]
    half = QK_ROPE_DIM // 2

    @pl.when(pl.program_id(1) == 0)
    def _():
        xprev_ref[...] = jnp.zeros_like(xprev_ref)
        hprev_ref[...] = jnp.zeros_like(hprev_ref)

    xn = [_rms_plain(x_ref[r, :]).astype(BF16) for r in rows]
    z_tiles = [{} for _ in tiles]

    def z(n, lo, hi):
        parts = []
        for j in range(lo // MXU_TILE, -(-hi // MXU_TILE)):
            if j not in z_tiles[n]:
                z_tiles[n][j] = _dot(xn[n], win_ref[:, j * MXU_TILE:(j + 1) * MXU_TILE])
            a, b = max(lo, j * MXU_TILE), min(hi, (j + 1) * MXU_TILE)
            parts.append(z_tiles[n][j][:, a - j * MXU_TILE:b - j * MXU_TILE])
        return parts[0] if len(parts) == 1 else jnp.concatenate(parts, axis=-1)

    xl = [z(n, Z_LRU_X, Z_LRU_GATE) for n in tiles]
    u = []
    prev = xprev_ref[...]
    for n in tiles:
        un = convb_ref[...] + convw_ref[CONV_WIDTH - 1:CONV_WIDTH, :] * xl[n]
        for d in range(1, CONV_WIDTH):
            k = CONV_WIDTH - 1 - d
            un = un + convw_ref[k:k + 1, :] * _shift_rows(xl[n], prev, d)
        u.append(un)
        prev = xl[n][ts - SUBLANES:ts]
    xprev_ref[...] = prev
    pre = [[_dot(u[n][:, j * LRU_GROUP:(j + 1) * LRU_GROUP].astype(BF16), wg_ref[j])
            for j in range(D_LRU // LRU_GROUP)] for n in tiles]

    z_gate = [z(n, Z_LRU_GATE, Z_CQ) for n in tiles]
    z_cq = [z(n, Z_CQ, Z_CKV) for n in tiles]
    z_ckv = [z(n, Z_CKV, Z_KPE) for n in tiles]
    z_kpe = [z(n, Z_KPE, Z_WIDTH) for n in tiles]
    cqn = [_rms_plain(v) for v in z_cq]
    ckvn = [_rms_plain(v) for v in z_ckv]
    q_t = [_dot(wuqt_ref[...], v.T.astype(BF16)) for v in cqn]
    v_t = [_dot(wvt_ref[...], v.T.astype(BF16)) for v in ckvn]
    kn = [_dot(v.astype(BF16), wkn_ref[...]) for v in ckvn]

    nlam = -lam_ref[...]
    softplus = jnp.maximum(nlam, 0.0) + jnp.log1p(jnp.exp(-jnp.abs(nlam)))
    h_last = hprev_ref[SUBLANES - 1:SUBLANES, :]
    for n in tiles:
        r = 1.0 / (1.0 + jnp.exp2(
            jnp.concatenate([p[:, :LRU_GROUP] for p in pre[n]], axis=-1) + ba_ref[...]))
        gi = 1.0 / (1.0 + jnp.exp2(
            jnp.concatenate([p[:, LRU_GROUP:] for p in pre[n]], axis=-1) + bx_ref[...]))
        a = jnp.exp2(r * ((-LRU_C * math.log2(math.e)) * softplus))
        one_m_a2 = 1.0 - a * a
        root = jnp.where(one_m_a2 > 0.0, one_m_a2 * lax.rsqrt(one_m_a2), 0.0)
        h = _linear_scan(a, root * (gi * u[n]), h_last)
        h_last = h[ts - 1:ts]
        if n == tiles[-1]:
            hprev_ref[...] = h[ts - SUBLANES:ts]
        ylru_ref[rows[n], :] = _rms_plain(h * _gelu_tanh(z_gate[n])).astype(BF16)

    for n in tiles:
        ang_t = invf_ref[...] * pos_ref[:, rows[n]].astype(F32)
        cos_t = jnp.cos(ang_t)
        sin_t = jnp.sin(ang_t)
        for hd in range(MLA_HEADS):
            r0 = hd * HEAD_SLAB
            pe = r0 + QK_NOPE_DIM
            x1 = q_t[n][pe:pe + half]
            x2 = q_t[n][pe + half:pe + 2 * half]
            qt_ref[n, r0:pe, :] = q_t[n][r0:pe].astype(BF16)
            qt_ref[n, pe:pe + half, :] = (x1 * cos_t - x2 * sin_t).astype(BF16)
            qt_ref[n, pe + half:pe + 2 * half, :] = (x2 * cos_t + x1 * sin_t).astype(BF16)
            qt_ref[n, pe + 2 * half:r0 + HEAD_SLAB, :] = jnp.zeros(
                (HEAD_SLAB - QK_NOPE_DIM - 2 * half, ts), BF16)
            v0 = hd * VT_ROWS
            vt_ref[n, v0:v0 + V_HEAD_DIM, :] = v_t[n][hd * V_HEAD_DIM:(hd + 1) * V_HEAD_DIM].astype(BF16)
            vt_ref[n, v0 + V_HEAD_DIM:v0 + VT_ROWS, :] = jnp.ones((BF16_ROWS, ts), BF16)
        pad = jnp.zeros((LANES - 2 * half, ts), F32)
        cos = jnp.concatenate([cos_t, cos_t, pad], axis=0).T
        sin = jnp.concatenate([sin_t, sin_t, pad], axis=0).T
        kpe = _rope(z_kpe[n], cos, sin).astype(BF16)
        for hd in range(MLA_HEADS):
            c0 = hd * HEAD_SLAB
            k_ref[rows[n], c0:c0 + LANES] = kn[n][:, hd * LANES:(hd + 1) * LANES].astype(BF16)
            k_ref[rows[n], c0 + LANES:c0 + HEAD_SLAB] = kpe


def _in_proj(x, pos_row, invf, win, convw, convb, wg, ba, bx, lam, wuqt, wkn, wvt):
    bsz, s, _ = x.shape
    ts = TS_IN
    step = TS_IN * IN_TILES_PER_STEP
    tok = lambda w: pl.BlockSpec((None, step, w), lambda b, i: (b, i, 0))
    in_specs = [
        tok(D_MODEL), pl.BlockSpec((None, 1, step), lambda b, i: (b, 0, i)),
        _const_spec((QK_ROPE_DIM // 2, 1)),
        _const_spec((D_MODEL, Z_WIDTH)), _const_spec((CONV_WIDTH, D_LRU)), _const_spec((1, D_LRU)),
        _const_spec((D_LRU // LRU_GROUP, LRU_GROUP, 2 * LRU_GROUP)),
        _const_spec((1, D_LRU)), _const_spec((1, D_LRU)), _const_spec((1, D_LRU)),
        _const_spec((MLA_HEADS * HEAD_SLAB, Q_LORA_RANK)),
        _const_spec((KV_LORA_RANK, MLA_HEADS * QK_NOPE_DIM)),
        _const_spec((D_MLA_OUT, KV_LORA_RANK)),
    ]
    tile_t = lambda w: pl.BlockSpec((None, IN_TILES_PER_STEP, w, ts), lambda b, i: (b, i, 0, 0))
    out_shape = (
        jax.ShapeDtypeStruct((bsz, s // ts, MLA_HEADS * HEAD_SLAB, ts), BF16),
        jax.ShapeDtypeStruct((bsz, s, MLA_HEADS * HEAD_SLAB), BF16),
        jax.ShapeDtypeStruct((bsz, s // ts, MLA_HEADS * VT_ROWS, ts), BF16),
        jax.ShapeDtypeStruct((bsz, s, D_LRU), BF16),
    )
    out_specs = (tile_t(MLA_HEADS * HEAD_SLAB), tok(MLA_HEADS * HEAD_SLAB),
                 tile_t(MLA_HEADS * VT_ROWS), tok(D_LRU))
    return pl.pallas_call(
        _in_proj_kernel,
        out_shape=out_shape,
        grid=(bsz, s // step),
        in_specs=in_specs,
        out_specs=out_specs,
        scratch_shapes=[pltpu.VMEM((SUBLANES, D_LRU), F32), pltpu.VMEM((SUBLANES, D_LRU), F32)],
        compiler_params=pltpu.CompilerParams(dimension_semantics=("parallel", "arbitrary"),
                                             vmem_limit_bytes=VMEM_LIMIT),
        name="in_proj",
    )(x, pos_row, invf, win, convw, convb, wg, ba, bx, lam, wuqt, wkn, wvt)


def _mla_attn_kernel(qt_ref, k_ref, vt_ref, o_ref):
    n_local, _, tq = qt_ref.shape
    nq = vt_ref.shape[0]
    i = pl.program_id(1)
    hq = tq // 2

    def scores(ql, j, hd, diagonal):
        slab = slice(hd * HEAD_SLAB, (hd + 1) * HEAD_SLAB)
        if not diagonal:
            st = _dot(k_ref[j * tq:(j + 1) * tq, slab], qt_ref[ql, slab, :])
            return [(0, 0, False, st[:, :hq]), (1, 0, False, st[:, hq:])]
        return [(0, 0, True, _dot(k_ref[j * tq:j * tq + hq, slab], qt_ref[ql, slab, 0:hq])),
                (1, 0, True, _dot(k_ref[j * tq:(j + 1) * tq, slab], qt_ref[ql, slab, hq:tq]))]

    def finish(ql, acc):
        outs = []
        for hd in range(MLA_HEADS):
            a = jnp.concatenate(acc[hd], axis=1)
            outs.append(a[:V_HEAD_DIM] * (1.0 / a[V_HEAD_DIM:V_HEAD_DIM + 1]))
        out_t = jnp.concatenate(outs, axis=0)
        ms = jnp.mean(out_t * out_t, axis=0, keepdims=True)
        o_ref[ql * tq:(ql + 1) * tq, :] = (out_t * lax.rsqrt(ms + RMS_EPS)).astype(BF16).T

    def attend(first_tile):
        units = [(ql, j, hd, j == first_tile + ql)
                 for ql in range(n_local) for j in range(first_tile + ql + 1)
                 for hd in range(MLA_HEADS)]
        ahead = [scores(*u) for u in units[:SCORE_LOOKAHEAD]]
        m = acc = None
        for n, (ql, j, hd, diagonal) in enumerate(units):
            if j == 0 and hd == 0:
                m = [[jnp.full((1, hq), NEG_INF, F32)] * 2 for _ in range(MLA_HEADS)]
                acc = [[None, None] for _ in range(MLA_HEADS)]
            pieces = ahead.pop(0)
            if n + SCORE_LOOKAHEAD < len(units):
                ahead.append(scores(*units[n + SCORE_LOOKAHEAD]))
            vrows = slice(hd * VT_ROWS, (hd + 1) * VT_ROWS)
            for half, k0, masked, st in pieces:
                nk = st.shape[0]
                if masked:
                    kid = lax.broadcasted_iota(jnp.int32, (nk, hq), 0) + k0
                    qid = lax.broadcasted_iota(jnp.int32, (nk, hq), 1) + half * hq
                    st = jnp.where(qid >= kid, st, NEG_INF)
                m_old = m[hd][half]
                m_new = jnp.maximum(m_old, jnp.max(st, axis=0, keepdims=True))
                p = jnp.exp2(st - m_new).astype(BF16)
                pv = _dot(vt_ref[j, vrows, k0:k0 + nk], p)
                old = acc[hd][half]
                acc[hd][half] = pv if old is None else jnp.exp2(m_old - m_new) * old + pv
                m[hd][half] = m_new
            if diagonal and hd == MLA_HEADS - 1:
                finish(ql, acc)

    for n in range(nq // n_local):
        @pl.when(i == n)
        def _(n=n):
            attend(n * n_local)


def _mla_attn(qt, k, vt):
    bsz, nq, _, tq = qt.shape
    s = k.shape[1]
    per = Q_TILES_PER_STEP
    return pl.pallas_call(
        _mla_attn_kernel,
        out_shape=jax.ShapeDtypeStruct((bsz, s, D_MLA_OUT), BF16),
        grid=(bsz, nq // per),
        in_specs=[
            pl.BlockSpec((None, per, MLA_HEADS * HEAD_SLAB, tq), lambda b, i: (b, i, 0, 0)),
            pl.BlockSpec((None, s, MLA_HEADS * HEAD_SLAB), lambda b, i: (b, 0, 0)),
            pl.BlockSpec((None, nq, MLA_HEADS * VT_ROWS, tq), lambda b, i: (b, 0, 0, 0)),
        ],
        out_specs=pl.BlockSpec((None, per * tq, D_MLA_OUT), lambda b, i: (b, i, 0)),
        compiler_params=pltpu.CompilerParams(dimension_semantics=("parallel", "arbitrary"),
                                             vmem_limit_bytes=VMEM_LIMIT),
        name="mla_attn",
    )(qt, k, vt)


def _mix_mem_kernel(ylru_ref, ymla_ref, x_ref, mem_ref, wout_ref, gpost_ref, gpre_ref, wmq_ref,
                    gmem_ref, wmk_ref, wmv_ref, wmo_ref, gpm_ref, h_ref, o_scr, mk_ref, mv_ref):
    first = pl.program_id(1) == 0
    for build_mem in (True, False):
        @pl.when(first == build_mem)
        def _(build_mem=build_mem):
            subs = [slice(r, r + ROW_SUB) for r in range(0, x_ref.shape[0], ROW_SUB)]
            yo = [_dot(ylru_ref[r, :], wout_ref[0:D_LRU, :])
                  + _dot(ymla_ref[r, :], wout_ref[D_LRU:, :]) for r in subs]
            h1 = [x_ref[r, :] + _rms(y, gpost_ref[...]) for r, y in zip(subs, yo)]

            xn = [_rms(h, gpre_ref[...]).astype(BF16) for h in h1]
            mq = [(_dot(v, wmq_ref[...]) * (1.0 / math.sqrt(MEM_HEAD_DIM))).astype(BF16)
                  for v in xn]
            if build_mem:
                mn = _rms(mem_ref[...], gmem_ref[...]).astype(BF16)
                mk = _dot(mn, wmk_ref[...]).astype(BF16)
                mv = _dot(mn, wmv_ref[...]).astype(BF16)
                mk_ref[...] = mk
                mv_ref[...] = mv
            else:
                mk, mv = mk_ref, mv_ref
            for r, q in zip(subs, mq):
                for hd in range(MEM_HEADS):
                    sl = slice(hd * MEM_HEAD_DIM, (hd + 1) * MEM_HEAD_DIM)
                    s = _dot_nt(q[:, sl], mk[:, sl])
                    m = jnp.max(s, axis=-1, keepdims=True)
                    p = jnp.exp(s - m)
                    l = jnp.sum(p, axis=-1, keepdims=True)
                    o_scr[r, sl] = (_dot(p.astype(BF16), mv[:, sl]) / l).astype(BF16)
            mo = [_dot(o_scr[r, :], wmo_ref[...]) for r in subs]
            for r, h, o in zip(subs, h1, mo):
                h_ref[r, :] = h + _rms(o, gpm_ref[...])


def _mix_mem(ylru, ymla, x, mem, wout, gpost, gpre, wmq, gmem, wmk, wmv, wmo, gpm):
    bsz, s, _ = x.shape
    tm = TM_MIX
    tok = lambda w: pl.BlockSpec((None, tm, w), lambda b, i: (b, i, 0))
    memspec = pl.BlockSpec((None, N_MEM, D_MODEL), lambda b, i: (b, 0, 0))
    sq = pl.BlockSpec((D_MODEL, D_MODEL), lambda b, i: (0, 0), pipeline_mode=pl.Buffered(1))
    vec = _const_spec((1, D_MODEL))
    return pl.pallas_call(
        _mix_mem_kernel,
        out_shape=jax.ShapeDtypeStruct((bsz, s, D_MODEL), F32),
        grid=(bsz, s // tm),
        in_specs=[tok(D_LRU), tok(D_MLA_OUT), tok(D_MODEL), memspec, sq, vec, vec, sq, vec, sq, sq,
                  sq, vec],
        out_specs=tok(D_MODEL),
        scratch_shapes=[pltpu.VMEM((tm, D_MODEL), BF16), pltpu.VMEM((N_MEM, D_MODEL), BF16),
                        pltpu.VMEM((N_MEM, D_MODEL), BF16)],
        compiler_params=pltpu.CompilerParams(dimension_semantics=("parallel", "arbitrary"),
                                             vmem_limit_bytes=VMEM_LIMIT),
        name="mix_mem",
    )(ylru, ymla, x, mem, wout, gpost, gpre, wmq, gmem, wmk, wmv, wmo, gpm)


def _ffn_kernel(h_ref, gpre_ref, wg_ref, wu_ref, wd_ref, gpost_ref, o_ref):
    subs = [slice(r, r + ROW_SUB) for r in range(0, h_ref.shape[0], ROW_SUB)]
    xn = [_rms(h_ref[r, :], gpre_ref[...]).astype(BF16) for r in subs]
    f = [None] * len(subs)
    for lo, hi in FF_CHUNKS:
        for n, v in enumerate(xn):
            act = (jax.nn.silu(_dot(v, wg_ref[:, lo:hi])) * _dot(v, wu_ref[:, lo:hi])).astype(BF16)
            part = _dot(act, wd_ref[lo:hi, :])
            f[n] = part if f[n] is None else f[n] + part
    for r, fr in zip(subs, f):
        o_ref[r, :] = h_ref[r, :] + _rms(fr, gpost_ref[...])


def _ffn(h, gpre, wg, wu, wd, gpost):
    m, _ = h.shape
    tm = TM_FFN
    single = pl.Buffered(1)
    tok = pl.BlockSpec((tm, D_MODEL), lambda i: (i, 0))
    vec = _const_spec((1, D_MODEL))
    return pl.pallas_call(
        _ffn_kernel,
        out_shape=jax.ShapeDtypeStruct((m, D_MODEL), F32),
        grid=(m // tm,),
        in_specs=[
            tok, vec,
            pl.BlockSpec((D_MODEL, D_FF), lambda i: (0, 0), pipeline_mode=single),
            pl.BlockSpec((D_MODEL, D_FF), lambda i: (0, 0), pipeline_mode=single),
            pl.BlockSpec((D_FF, D_MODEL), lambda i: (0, 0), pipeline_mode=single),
            vec,
        ],
        out_specs=tok,
        compiler_params=pltpu.CompilerParams(dimension_semantics=("parallel",),
                                             vmem_limit_bytes=VMEM_LIMIT),
        name="ffn",
    )(h, gpre, wg, wu, wd, gpost)


def _block_diag(w):
    eye = jnp.eye(LRU_BLOCKS, dtype=w.dtype)
    return jnp.einsum('hij,hg->higj', w, eye).reshape(D_LRU, D_LRU)


def _gate_weights(wa, wx):
    da, dx = _block_diag(wa) * -math.log2(math.e), _block_diag(wx) * -math.log2(math.e)
    groups = []
    for j in range(D_LRU // LRU_GROUP):
        sl = slice(j * LRU_GROUP, (j + 1) * LRU_GROUP)
        groups.append(jnp.concatenate([da[sl, sl], dx[sl, sl]], axis=1))
    return jnp.stack(groups).astype(BF16)


def _pad_in_proj(w_in, gain):
    w = gain[:, None] * w_in
    pad = jnp.zeros((D_MODEL, Z_WIDTH - w.shape[1]), w.dtype)
    return jnp.concatenate([w, pad], axis=1).astype(BF16)


def _q_weights_t(w_uq, gain):
    dh = QK_NOPE_DIM + QK_ROPE_DIM
    scale = math.log2(math.e) / math.sqrt(dh)
    w = (scale * gain[:, None] * w_uq).reshape(Q_LORA_RANK, MLA_HEADS, dh)
    pad = jnp.zeros((Q_LORA_RANK, MLA_HEADS, HEAD_SLAB - dh), w.dtype)
    w = jnp.concatenate([w, pad], axis=-1).reshape(Q_LORA_RANK, MLA_HEADS * HEAD_SLAB)
    return w.T.astype(BF16)


def _kv_weights(w_ukv, gain):
    w = (gain[:, None] * w_ukv).reshape(KV_LORA_RANK, MLA_HEADS, QK_NOPE_DIM + V_HEAD_DIM)
    kn = w[:, :, :QK_NOPE_DIM].reshape(KV_LORA_RANK, MLA_HEADS * QK_NOPE_DIM)
    vv = w[:, :, QK_NOPE_DIM:].reshape(KV_LORA_RANK, MLA_HEADS * V_HEAD_DIM)
    return kn.astype(BF16), vv.T.astype(BF16)


def _rope_freq_column():
    inv_freq = ROPE_THETA ** (-jnp.arange(0, QK_ROPE_DIM, 2, dtype=F32) / QK_ROPE_DIM)
    return inv_freq.reshape(QK_ROPE_DIM // 2, 1)


def kernel(x, mem, positions, g_pre_mix, w_in, conv_w, conv_b, lru_wa, lru_ba, lru_wx, lru_bx, lru_lambda, g_q_lat, w_uq, g_kv_lat, w_ukv, g_lru_out, g_mla_out, w_out, g_post_mix, g_pre_mem, g_mem_kv, w_mq, w_mk, w_mv, w_mo, g_post_mem, g_pre_ffn, w_gate, w_up, w_down, g_post_ffn):
    bsz, s, d = x.shape
    depth = w_in.shape[0]
    assert d == D_MODEL and s % (TS_IN * max(IN_TILES_PER_STEP, Q_TILES_PER_STEP)) == 0
    assert s % TM_MIX == 0 and (bsz * s) % TM_FFN == 0 and mem.shape == (bsz, N_MEM, D_MODEL)
    pos_row = positions.reshape(bsz, 1, s)
    invf = _rope_freq_column()
    row = lambda a: a.reshape(1, -1)

    h = x
    for l in range(depth):
        wkn, wvt = _kv_weights(w_ukv[l], g_kv_lat[l])
        neg_log2e = -math.log2(math.e)
        qt, k, vt, ylru = _in_proj(
            h, pos_row, invf, _pad_in_proj(w_in[l], g_pre_mix[l]), conv_w[l], row(conv_b[l]),
            _gate_weights(lru_wa[l], lru_wx[l]), row(neg_log2e * lru_ba[l]),
            row(neg_log2e * lru_bx[l]), row(lru_lambda[l]),
            _q_weights_t(w_uq[l], g_q_lat[l]), wkn, wvt)
        ymla = _mla_attn(qt, k, vt)
        wout = (jnp.concatenate([g_lru_out[l], g_mla_out[l]])[:, None] * w_out[l]).astype(BF16)
        h = _mix_mem(ylru, ymla, h, mem, wout, row(g_post_mix[l]), row(g_pre_mem[l]),
                     w_mq[l].astype(BF16), row(g_mem_kv[l]), w_mk[l].astype(BF16),
                     w_mv[l].astype(BF16), w_mo[l].astype(BF16), row(g_post_mem[l]))
        h = _ffn(h.reshape(bsz * s, d), row(g_pre_ffn[l]), w_gate[l].astype(BF16),
                 w_up[l].astype(BF16), w_down[l].astype(BF16), row(g_post_ffn[l])).reshape(bsz, s, d)
    return h
```

```python
import math

import jax
import jax.numpy as jnp
from jax import lax
from jax.experimental import pallas as pl
from jax.experimental.pallas import tpu as pltpu

F32 = jnp.float32
BF16 = jnp.bfloat16

D_MODEL = 1024
N_MEM = 256
MEM_HEADS = 4
MEM_HEAD_DIM = D_MODEL // MEM_HEADS
D_LRU = 512
LRU_BLOCKS = 8
LRU_BLOCK_DIM = D_LRU // LRU_BLOCKS
CONV_WIDTH = 4
LRU_C = 8.0
MLA_HEADS = 4
QK_NOPE_DIM = 128
QK_ROPE_DIM = 64
V_HEAD_DIM = 128
Q_LORA_RANK = 384
KV_LORA_RANK = 256
D_MLA_OUT = MLA_HEADS * V_HEAD_DIM
ROPE_THETA = 10000.0
D_FF = 2816
RMS_EPS = 1e-6
NEG_INF = -1e30

LANES = 128
SUBLANES = 8
HEAD_SLAB = 2 * LANES
BF16_ROWS = 16
VT_ROWS = V_HEAD_DIM + BF16_ROWS
Z_LRU_X = 0
Z_LRU_GATE = Z_LRU_X + D_LRU
Z_CQ = Z_LRU_GATE + D_LRU
Z_CKV = Z_CQ + Q_LORA_RANK
Z_KPE = Z_CKV + KV_LORA_RANK
Z_WIDTH = Z_KPE + LANES
MXU_TILE = 256
LRU_GROUP = MXU_TILE

TS_IN = 512
IN_TILES_PER_STEP = 2
Q_TILES_PER_STEP = 4
SCORE_LOOKAHEAD = 2
TM_MIX = 1024
TM_FFN = 1024
ROW_SUB = 256
FF_CHUNKS = ((0, 1536), (1536, D_FF))
VMEM_LIMIT = 56 * 1024 * 1024


def _rms(x, g):
    ms = jnp.mean(x * x, axis=-1, keepdims=True)
    return x * lax.rsqrt(ms + RMS_EPS) * g


def _rms_plain(x):
    ms = jnp.mean(x * x, axis=-1, keepdims=True)
    return x * lax.rsqrt(ms + RMS_EPS)


def _gelu_tanh(x):
    k1 = -2.0 * math.sqrt(2.0 / math.pi) * math.log2(math.e)
    k2 = k1 * 0.044715
    return x / (1.0 + jnp.exp2(x * (k1 + k2 * (x * x))))


def _dot(a, b):
    return jnp.dot(a, b, preferred_element_type=F32)


def _dot_nt(a, b):
    return lax.dot_general(a, b, (((1,), (1,)), ((), ())), preferred_element_type=F32)


def _const_spec(shape):
    return pl.BlockSpec(shape, lambda *_: (0,) * len(shape))


def _shift_rows(x, prev, d):
    r = pltpu.roll(x, d, 0)
    row = lax.broadcasted_iota(jnp.int32, prev.shape, 0)
    head = jnp.where(row < d, pltpu.roll(prev, d, 0), r[0:SUBLANES])
    return jnp.concatenate([head, r[SUBLANES:]], axis=0)


def _linear_scan(a, u, h0, a_scr, u_scr):
    t, c = a.shape
    seg = t // SUBLANES
    lane_tiles = c // LANES
    for ct in range(lane_tiles):
        cols = slice(ct * LANES, (ct + 1) * LANES)
        for s in range(SUBLANES):
            a_scr[ct, pl.ds(s, seg, stride=SUBLANES), :] = a[s * seg:(s + 1) * seg, cols]
            u_scr[ct, pl.ds(s, seg, stride=SUBLANES), :] = u[s * seg:(s + 1) * seg, cols]
    h = jnp.zeros((SUBLANES, c), F32)
    p = jnp.ones((SUBLANES, c), F32)
    for j in range(seg):
        rows = slice(j * SUBLANES, (j + 1) * SUBLANES)
        aj = jnp.concatenate([a_scr[ct, rows, :] for ct in range(lane_tiles)], axis=-1)
        uj = jnp.concatenate([u_scr[ct, rows, :] for ct in range(lane_tiles)], axis=-1)
        h = aj * h + uj
        p = aj * p
        for ct in range(lane_tiles):
            u_scr[ct, rows, :] = h[:, ct * LANES:(ct + 1) * LANES]
            a_scr[ct, rows, :] = p[:, ct * LANES:(ct + 1) * LANES]
    carries = [h0]
    for s in range(1, SUBLANES):
        carries.append(h[s - 1:s] + p[s - 1:s] * carries[-1])
    carry = jnp.concatenate(carries, axis=0)
    for ct in range(lane_tiles):
        fixed = (u_scr[ct].reshape(seg, SUBLANES, LANES)
                 + a_scr[ct].reshape(seg, SUBLANES, LANES) * carry[None, :, ct * LANES:(ct + 1) * LANES])
        u_scr[ct] = fixed.reshape(t, LANES)
    return jnp.concatenate(
        [jnp.concatenate([u_scr[ct, pl.ds(s, seg, stride=SUBLANES), :] for ct in range(lane_tiles)],
                         axis=-1) for s in range(SUBLANES)], axis=0)


def _rope(t, cos, sin):
    half = QK_ROPE_DIM // 2
    return t * cos + pltpu.roll(t, half, 1) * sin - pltpu.roll(t, LANES - half, 1) * sin


def _in_proj_kernel(x_ref, pos_ref, invf_ref, win_ref, convw_ref, convb_ref, wg_ref,
                    ba_ref, bx_ref, lam_ref, wuqt_ref, wkn_ref, wvt_ref,
                    qt_ref, k_ref, vt_ref, ylru_ref, xprev_ref, hprev_ref, asm_ref, usm_ref):
    ts = TS_IN
    tiles = range(x_ref.shape[0] // ts)
    rows = [slice(n * ts, (n + 1) * ts) for n in tiles]
    half = QK_ROPE_DIM // 2

    @pl.when(pl.program_id(1) == 0)
    def _():
        xprev_ref[...] = jnp.zeros_like(xprev_ref)
        hprev_ref[...] = jnp.zeros_like(hprev_ref)

    xn = [_rms_plain(x_ref[r, :]).astype(BF16) for r in rows]
    z_tiles = [{} for _ in tiles]

    def z(n, lo, hi):
        parts = []
        for j in range(lo // MXU_TILE, -(-hi // MXU_TILE)):
            if j not in z_tiles[n]:
                z_tiles[n][j] = _dot(xn[n], win_ref[:, j * MXU_TILE:(j + 1) * MXU_TILE])
            a, b = max(lo, j * MXU_TILE), min(hi, (j + 1) * MXU_TILE)
            parts.append(z_tiles[n][j][:, a - j * MXU_TILE:b - j * MXU_TILE])
        return parts[0] if len(parts) == 1 else jnp.concatenate(parts, axis=-1)

    xl = [z(n, Z_LRU_X, Z_LRU_GATE) for n in tiles]
    u = []
    prev = xprev_ref[...]
    for n in tiles:
        un = convb_ref[...] + convw_ref[CONV_WIDTH - 1:CONV_WIDTH, :] * xl[n]
        for d in range(1, CONV_WIDTH):
            k = CONV_WIDTH - 1 - d
            un = un + convw_ref[k:k + 1, :] * _shift_rows(xl[n], prev, d)
        u.append(un)
        prev = xl[n][ts - SUBLANES:ts]
    xprev_ref[...] = prev
    pre = [[_dot(u[n][:, j * LRU_GROUP:(j + 1) * LRU_GROUP].astype(BF16), wg_ref[j])
            for j in range(D_LRU // LRU_GROUP)] for n in tiles]

    z_gate = [z(n, Z_LRU_GATE, Z_CQ) for n in tiles]
    z_cq = [z(n, Z_CQ, Z_CKV) for n in tiles]
    z_ckv = [z(n, Z_CKV, Z_KPE) for n in tiles]
    z_kpe = [z(n, Z_KPE, Z_WIDTH) for n in tiles]
    cqn = [_rms_plain(v) for v in z_cq]
    ckvn = [_rms_plain(v) for v in z_ckv]
    q_t = [_dot(wuqt_ref[...], v.T.astype(BF16)) for v in cqn]
    v_t = [_dot(wvt_ref[...], v.T.astype(BF16)) for v in ckvn]
    kn = [_dot(v.astype(BF16), wkn_ref[...]) for v in ckvn]

    nlam = -lam_ref[...]
    softplus = jnp.maximum(nlam, 0.0) + jnp.log1p(jnp.exp(-jnp.abs(nlam)))
    h_last = hprev_ref[SUBLANES - 1:SUBLANES, :]
    for n in tiles:
        r = 1.0 / (1.0 + jnp.exp2(
            jnp.concatenate([p[:, :LRU_GROUP] for p in pre[n]], axis=-1) + ba_ref[...]))
        gi = 1.0 / (1.0 + jnp.exp2(
            jnp.concatenate([p[:, LRU_GROUP:] for p in pre[n]], axis=-1) + bx_ref[...]))
        a = jnp.exp2(r * ((-LRU_C * math.log2(math.e)) * softplus))
        one_m_a2 = 1.0 - a * a
        root = jnp.where(one_m_a2 > 0.0, one_m_a2 * lax.rsqrt(one_m_a2), 0.0)
        h = _linear_scan(a, root * (gi * u[n]), h_last, asm_ref, usm_ref)
        h_last = h[ts - 1:ts]
        if n == tiles[-1]:
            hprev_ref[...] = h[ts - SUBLANES:ts]
        ylru_ref[rows[n], :] = _rms_plain(h * _gelu_tanh(z_gate[n])).astype(BF16)

    for n in tiles:
        ang_t = invf_ref[...] * pos_ref[:, rows[n]].astype(F32)
        cos_t = jnp.cos(ang_t)
        sin_t = jnp.sin(ang_t)
        for hd in range(MLA_HEADS):
            r0 = hd * HEAD_SLAB
            pe = r0 + QK_NOPE_DIM
            x1 = q_t[n][pe:pe + half]
            x2 = q_t[n][pe + half:pe + 2 * half]
            qt_ref[n, r0:pe, :] = q_t[n][r0:pe].astype(BF16)
            qt_ref[n, pe:pe + half, :] = (x1 * cos_t - x2 * sin_t).astype(BF16)
            qt_ref[n, pe + half:pe + 2 * half, :] = (x2 * cos_t + x1 * sin_t).astype(BF16)
            qt_ref[n, pe + 2 * half:r0 + HEAD_SLAB, :] = jnp.zeros(
                (HEAD_SLAB - QK_NOPE_DIM - 2 * half, ts), BF16)
            v0 = hd * VT_ROWS
            vt_ref[n, v0:v0 + V_HEAD_DIM, :] = v_t[n][hd * V_HEAD_DIM:(hd + 1) * V_HEAD_DIM].astype(BF16)
            vt_ref[n, v0 + V_HEAD_DIM:v0 + VT_ROWS, :] = jnp.ones((BF16_ROWS, ts), BF16)
        pad = jnp.zeros((LANES - 2 * half, ts), F32)
        cos = jnp.concatenate([cos_t, cos_t, pad], axis=0).T
        sin = jnp.concatenate([sin_t, sin_t, pad], axis=0).T
        kpe = _rope(z_kpe[n], cos, sin).astype(BF16)
        for hd in range(MLA_HEADS):
            c0 = hd * HEAD_SLAB
            k_ref[rows[n], c0:c0 + LANES] = kn[n][:, hd * LANES:(hd + 1) * LANES].astype(BF16)
            k_ref[rows[n], c0 + LANES:c0 + HEAD_SLAB] = kpe


def _in_proj(x, pos_row, invf, win, convw, convb, wg, ba, bx, lam, wuqt, wkn, wvt):
    bsz, s, _ = x.shape
    ts = TS_IN
    step = TS_IN * IN_TILES_PER_STEP
    tok = lambda w: pl.BlockSpec((None, step, w), lambda b, i: (b, i, 0))
    in_specs = [
        tok(D_MODEL), pl.BlockSpec((None, 1, step), lambda b, i: (b, 0, i)),
        _const_spec((QK_ROPE_DIM // 2, 1)),
        _const_spec((D_MODEL, Z_WIDTH)), _const_spec((CONV_WIDTH, D_LRU)), _const_spec((1, D_LRU)),
        _const_spec((D_LRU // LRU_GROUP, LRU_GROUP, 2 * LRU_GROUP)),
        _const_spec((1, D_LRU)), _const_spec((1, D_LRU)), _const_spec((1, D_LRU)),
        _const_spec((MLA_HEADS * HEAD_SLAB, Q_LORA_RANK)),
        _const_spec((KV_LORA_RANK, MLA_HEADS * QK_NOPE_DIM)),
        _const_spec((D_MLA_OUT, KV_LORA_RANK)),
    ]
    tile_t = lambda w: pl.BlockSpec((None, IN_TILES_PER_STEP, w, ts), lambda b, i: (b, i, 0, 0))
    out_shape = (
        jax.ShapeDtypeStruct((bsz, s // ts, MLA_HEADS * HEAD_SLAB, ts), BF16),
        jax.ShapeDtypeStruct((bsz, s, MLA_HEADS * HEAD_SLAB), BF16),
        jax.ShapeDtypeStruct((bsz, s // ts, MLA_HEADS * VT_ROWS, ts), BF16),
        jax.ShapeDtypeStruct((bsz, s, D_LRU), BF16),
    )
    out_specs = (tile_t(MLA_HEADS * HEAD_SLAB), tok(MLA_HEADS * HEAD_SLAB),
                 tile_t(MLA_HEADS * VT_ROWS), tok(D_LRU))
    return pl.pallas_call(
        _in_proj_kernel,
        out_shape=out_shape,
        grid=(bsz, s // step),
        in_specs=in_specs,
        out_specs=out_specs,
        scratch_shapes=[pltpu.VMEM((SUBLANES, D_LRU), F32), pltpu.VMEM((SUBLANES, D_LRU), F32),
                        pltpu.VMEM((D_LRU // LANES, ts, LANES), F32),
                        pltpu.VMEM((D_LRU // LANES, ts, LANES), F32)],
        compiler_params=pltpu.CompilerParams(dimension_semantics=("parallel", "arbitrary"),
                                             vmem_limit_bytes=VMEM_LIMIT),
        name="in_proj",
    )(x, pos_row, invf, win, convw, convb, wg, ba, bx, lam, wuqt, wkn, wvt)


def _mla_attn_kernel(qt_ref, k_ref, vt_ref, o_ref):
    n_local, _, tq = qt_ref.shape
    nq = vt_ref.shape[0]
    i = pl.program_id(1)
    hq = tq // 2

    def scores(ql, j, hd, diagonal):
        slab = slice(hd * HEAD_SLAB, (hd + 1) * HEAD_SLAB)
        if not diagonal:
            st = _dot(k_ref[j * tq:(j + 1) * tq, slab], qt_ref[ql, slab, :])
            return [(0, 0, False, st[:, :hq]), (1, 0, False, st[:, hq:])]
        return [(0, 0, True, _dot(k_ref[j * tq:j * tq + hq, slab], qt_ref[ql, slab, 0:hq])),
                (1, 0, True, _dot(k_ref[j * tq:(j + 1) * tq, slab], qt_ref[ql, slab, hq:tq]))]

    def finish(ql, acc):
        outs = []
        for hd in range(MLA_HEADS):
            a = jnp.concatenate(acc[hd], axis=1)
            outs.append(a[:V_HEAD_DIM] * (1.0 / a[V_HEAD_DIM:V_HEAD_DIM + 1]))
        out_t = jnp.concatenate(outs, axis=0)
        ms = jnp.mean(out_t * out_t, axis=0, keepdims=True)
        o_ref[ql * tq:(ql + 1) * tq, :] = (out_t * lax.rsqrt(ms + RMS_EPS)).astype(BF16).T

    def attend(first_tile):
        units = [(ql, j, hd, j == first_tile + ql)
                 for ql in range(n_local) for j in range(first_tile + ql + 1)
                 for hd in range(MLA_HEADS)]
        ahead = [scores(*u) for u in units[:SCORE_LOOKAHEAD]]
        m = acc = None
        for n, (ql, j, hd, diagonal) in enumerate(units):
            if j == 0 and hd == 0:
                m = [[jnp.full((1, hq), NEG_INF, F32)] * 2 for _ in range(MLA_HEADS)]
                acc = [[None, None] for _ in range(MLA_HEADS)]
            pieces = ahead.pop(0)
            if n + SCORE_LOOKAHEAD < len(units):
                ahead.append(scores(*units[n + SCORE_LOOKAHEAD]))
            vrows = slice(hd * VT_ROWS, (hd + 1) * VT_ROWS)
            for half, k0, masked, st in pieces:
                nk = st.shape[0]
                if masked:
                    kid = lax.broadcasted_iota(jnp.int32, (nk, hq), 0) + k0
                    qid = lax.broadcasted_iota(jnp.int32, (nk, hq), 1) + half * hq
                    st = jnp.where(qid >= kid, st, NEG_INF)
                m_old = m[hd][half]
                m_new = jnp.maximum(m_old, jnp.max(st, axis=0, keepdims=True))
                p = jnp.exp2(st - m_new).astype(BF16)
                pv = _dot(vt_ref[j, vrows, k0:k0 + nk], p)
                old = acc[hd][half]
                acc[hd][half] = pv if old is None else jnp.exp2(m_old - m_new) * old + pv
                m[hd][half] = m_new
            if diagonal and hd == MLA_HEADS - 1:
                finish(ql, acc)

    for n in range(nq // n_local):
        @pl.when(i == n)
        def _(n=n):
            attend(n * n_local)


def _mla_attn(qt, k, vt):
    bsz, nq, _, tq = qt.shape
    s = k.shape[1]
    per = Q_TILES_PER_STEP
    return pl.pallas_call(
        _mla_attn_kernel,
        out_shape=jax.ShapeDtypeStruct((bsz, s, D_MLA_OUT), BF16),
        grid=(bsz, nq // per),
        in_specs=[
            pl.BlockSpec((None, per, MLA_HEADS * HEAD_SLAB, tq), lambda b, i: (b, i, 0, 0)),
            pl.BlockSpec((None, s, MLA_HEADS * HEAD_SLAB), lambda b, i: (b, 0, 0)),
            pl.BlockSpec((None, nq, MLA_HEADS * VT_ROWS, tq), lambda b, i: (b, 0, 0, 0)),
        ],
        out_specs=pl.BlockSpec((None, per * tq, D_MLA_OUT), lambda b, i: (b, i, 0)),
        compiler_params=pltpu.CompilerParams(dimension_semantics=("parallel", "arbitrary"),
                                             vmem_limit_bytes=VMEM_LIMIT),
        name="mla_attn",
    )(qt, k, vt)


def _mix_mem_kernel(ylru_ref, ymla_ref, x_ref, mem_ref, wout_ref, gpost_ref, gpre_ref, wmq_ref,
                    gmem_ref, wmk_ref, wmv_ref, wmo_ref, gpm_ref, h_ref, o_scr, mk_ref, mv_ref):
    first = pl.program_id(1) == 0
    for build_mem in (True, False):
        @pl.when(first == build_mem)
        def _(build_mem=build_mem):
            subs = [slice(r, r + ROW_SUB) for r in range(0, x_ref.shape[0], ROW_SUB)]
            yo = [_dot(ylru_ref[r, :], wout_ref[0:D_LRU, :])
                  + _dot(ymla_ref[r, :], wout_ref[D_LRU:, :]) for r in subs]
            h1 = [x_ref[r, :] + _rms(y, gpost_ref[...]) for r, y in zip(subs, yo)]

            xn = [_rms(h, gpre_ref[...]).astype(BF16) for h in h1]
            mq = [(_dot(v, wmq_ref[...]) * (1.0 / math.sqrt(MEM_HEAD_DIM))).astype(BF16)
                  for v in xn]
            if build_mem:
                mn = _rms(mem_ref[...], gmem_ref[...]).astype(BF16)
                mk = _dot(mn, wmk_ref[...]).astype(BF16)
                mv = _dot(mn, wmv_ref[...]).astype(BF16)
                mk_ref[...] = mk
                mv_ref[...] = mv
            else:
                mk, mv = mk_ref, mv_ref
            for r, q in zip(subs, mq):
                for hd in range(MEM_HEADS):
                    sl = slice(hd * MEM_HEAD_DIM, (hd + 1) * MEM_HEAD_DIM)
                    s = _dot_nt(q[:, sl], mk[:, sl])
                    m = jnp.max(s, axis=-1, keepdims=True)
                    p = jnp.exp(s - m)
                    l = jnp.sum(p, axis=-1, keepdims=True)
                    o_scr[r, sl] = (_dot(p.astype(BF16), mv[:, sl]) / l).astype(BF16)
            mo = [_dot(o_scr[r, :], wmo_ref[...]) for r in subs]
            for r, h, o in zip(subs, h1, mo):
                h_ref[r, :] = h + _rms(o, gpm_ref[...])


def _mix_mem(ylru, ymla, x, mem, wout, gpost, gpre, wmq, gmem, wmk, wmv, wmo, gpm):
    bsz, s, _ = x.shape
    tm = TM_MIX
    tok = lambda w: pl.BlockSpec((None, tm, w), lambda b, i: (b, i, 0))
    memspec = pl.BlockSpec((None, N_MEM, D_MODEL), lambda b, i: (b, 0, 0))
    sq = pl.BlockSpec((D_MODEL, D_MODEL), lambda b, i: (0, 0), pipeline_mode=pl.Buffered(1))
    vec = _const_spec((1, D_MODEL))
    return pl.pallas_call(
        _mix_mem_kernel,
        out_shape=jax.ShapeDtypeStruct((bsz, s, D_MODEL), F32),
        grid=(bsz, s // tm),
        in_specs=[tok(D_LRU), tok(D_MLA_OUT), tok(D_MODEL), memspec, sq, vec, vec, sq, vec, sq, sq,
                  sq, vec],
        out_specs=tok(D_MODEL),
        scratch_shapes=[pltpu.VMEM((tm, D_MODEL), BF16), pltpu.VMEM((N_MEM, D_MODEL), BF16),
                        pltpu.VMEM((N_MEM, D_MODEL), BF16)],
        compiler_params=pltpu.CompilerParams(dimension_semantics=("parallel", "arbitrary"),
                                             vmem_limit_bytes=VMEM_LIMIT),
        name="mix_mem",
    )(ylru, ymla, x, mem, wout, gpost, gpre, wmq, gmem, wmk, wmv, wmo, gpm)


def _ffn_kernel(h_ref, gpre_ref, wg_ref, wu_ref, wd_ref, gpost_ref, o_ref):
    subs = [slice(r, r + ROW_SUB) for r in range(0, h_ref.shape[0], ROW_SUB)]
    xn = [_rms(h_ref[r, :], gpre_ref[...]).astype(BF16) for r in subs]
    f = [None] * len(subs)
    for lo, hi in FF_CHUNKS:
        for n, v in enumerate(xn):
            act = (jax.nn.silu(_dot(v, wg_ref[:, lo:hi])) * _dot(v, wu_ref[:, lo:hi])).astype(BF16)
            part = _dot(act, wd_ref[lo:hi, :])
            f[n] = part if f[n] is None else f[n] + part
    for r, fr in zip(subs, f):
        o_ref[r, :] = h_ref[r, :] + _rms(fr, gpost_ref[...])


def _ffn(h, gpre, wg, wu, wd, gpost):
    m, _ = h.shape
    tm = TM_FFN
    single = pl.Buffered(1)
    tok = pl.BlockSpec((tm, D_MODEL), lambda i: (i, 0))
    vec = _const_spec((1, D_MODEL))
    return pl.pallas_call(
        _ffn_kernel,
        out_shape=jax.ShapeDtypeStruct((m, D_MODEL), F32),
        grid=(m // tm,),
        in_specs=[
            tok, vec,
            pl.BlockSpec((D_MODEL, D_FF), lambda i: (0, 0), pipeline_mode=single),
            pl.BlockSpec((D_MODEL, D_FF), lambda i: (0, 0), pipeline_mode=single),
            pl.BlockSpec((D_FF, D_MODEL), lambda i: (0, 0), pipeline_mode=single),
            vec,
        ],
        out_specs=tok,
        compiler_params=pltpu.CompilerParams(dimension_semantics=("parallel",),
                                             vmem_limit_bytes=VMEM_LIMIT),
        name="ffn",
    )(h, gpre, wg, wu, wd, gpost)


def _block_diag(w):
    eye = jnp.eye(LRU_BLOCKS, dtype=w.dtype)
    return jnp.einsum('hij,hg->higj', w, eye).reshape(D_LRU, D_LRU)


def _gate_weights(wa, wx):
    da, dx = _block_diag(wa) * -math.log2(math.e), _block_diag(wx) * -math.log2(math.e)
    groups = []
    for j in range(D_LRU // LRU_GROUP):
        sl = slice(j * LRU_GROUP, (j + 1) * LRU_GROUP)
        groups.append(jnp.concatenate([da[sl, sl], dx[sl, sl]], axis=1))
    return jnp.stack(groups).astype(BF16)


def _pad_in_proj(w_in, gain):
    w = gain[:, None] * w_in
    pad = jnp.zeros((D_MODEL, Z_WIDTH - w.shape[1]), w.dtype)
    return jnp.concatenate([w, pad], axis=1).astype(BF16)


def _q_weights_t(w_uq, gain):
    dh = QK_NOPE_DIM + QK_ROPE_DIM
    scale = math.log2(math.e) / math.sqrt(dh)
    w = (scale * gain[:, None] * w_uq).reshape(Q_LORA_RANK, MLA_HEADS, dh)
    pad = jnp.zeros((Q_LORA_RANK, MLA_HEADS, HEAD_SLAB - dh), w.dtype)
    w = jnp.concatenate([w, pad], axis=-1).reshape(Q_LORA_RANK, MLA_HEADS * HEAD_SLAB)
    return w.T.astype(BF16)


def _kv_weights(w_ukv, gain):
    w = (gain[:, None] * w_ukv).reshape(KV_LORA_RANK, MLA_HEADS, QK_NOPE_DIM + V_HEAD_DIM)
    kn = w[:, :, :QK_NOPE_DIM].reshape(KV_LORA_RANK, MLA_HEADS * QK_NOPE_DIM)
    vv = w[:, :, QK_NOPE_DIM:].reshape(KV_LORA_RANK, MLA_HEADS * V_HEAD_DIM)
    return kn.astype(BF16), vv.T.astype(BF16)


def _rope_freq_column():
    inv_freq = ROPE_THETA ** (-jnp.arange(0, QK_ROPE_DIM, 2, dtype=F32) / QK_ROPE_DIM)
    return inv_freq.reshape(QK_ROPE_DIM // 2, 1)


def kernel(x, mem, positions, g_pre_mix, w_in, conv_w, conv_b, lru_wa, lru_ba, lru_wx, lru_bx, lru_lambda, g_q_lat, w_uq, g_kv_lat, w_ukv, g_lru_out, g_mla_out, w_out, g_post_mix, g_pre_mem, g_mem_kv, w_mq, w_mk, w_mv, w_mo, g_post_mem, g_pre_ffn, w_gate, w_up, w_down, g_post_ffn):
    bsz, s, d = x.shape
    depth = w_in.shape[0]
    assert d == D_MODEL and s % (TS_IN * max(IN_TILES_PER_STEP, Q_TILES_PER_STEP)) == 0
    assert s % TM_MIX == 0 and (bsz * s) % TM_FFN == 0 and mem.shape == (bsz, N_MEM, D_MODEL)
    pos_row = positions.reshape(bsz, 1, s)
    invf = _rope_freq_column()
    row = lambda a: a.reshape(1, -1)

    h = x
    for l in range(depth):
        wkn, wvt = _kv_weights(w_ukv[l], g_kv_lat[l])
        neg_log2e = -math.log2(math.e)
        qt, k, vt, ylru = _in_proj(
            h, pos_row, invf, _pad_in_proj(w_in[l], g_pre_mix[l]), conv_w[l], row(conv_b[l]),
            _gate_weights(lru_wa[l], lru_wx[l]), row(neg_log2e * lru_ba[l]),
            row(neg_log2e * lru_bx[l]), row(lru_lambda[l]),
            _q_weights_t(w_uq[l], g_q_lat[l]), wkn, wvt)
        ymla = _mla_attn(qt, k, vt)
        wout = (jnp.concatenate([g_lru_out[l], g_mla_out[l]])[:, None] * w_out[l]).astype(BF16)
        h = _mix_mem(ylru, ymla, h, mem, wout, row(g_post_mix[l]), row(g_pre_mem[l]),
                     w_mq[l].astype(BF16), row(g_mem_kv[l]), w_mk[l].astype(BF16),
                     w_mv[l].astype(BF16), w_mo[l].astype(BF16), row(g_post_mem[l]))
        h = _ffn(h.reshape(bsz * s, d), row(g_pre_ffn[l]), w_gate[l].astype(BF16),
                 w_up[l].astype(BF16), w_down[l].astype(BF16), row(g_post_ffn[l])).reshape(bsz, s, d)
    return h
```

```python
import math

import jax
import jax.numpy as jnp
from jax import lax
from jax.experimental import pallas as pl
from jax.experimental.pallas import tpu as pltpu

F32 = jnp.float32
BF16 = jnp.bfloat16

D_MODEL = 1024
N_MEM = 256
MEM_HEADS = 4
MEM_HEAD_DIM = D_MODEL // MEM_HEADS
D_LRU = 512
LRU_BLOCKS = 8
LRU_BLOCK_DIM = D_LRU // LRU_BLOCKS
CONV_WIDTH = 4
LRU_C = 8.0
MLA_HEADS = 4
QK_NOPE_DIM = 128
QK_ROPE_DIM = 64
V_HEAD_DIM = 128
Q_LORA_RANK = 384
KV_LORA_RANK = 256
D_MLA_OUT = MLA_HEADS * V_HEAD_DIM
ROPE_THETA = 10000.0
D_FF = 2816
RMS_EPS = 1e-6
NEG_INF = -1e30

LANES = 128
SUBLANES = 8
HEAD_SLAB = 2 * LANES
BF16_ROWS = 16
VT_ROWS = V_HEAD_DIM + BF16_ROWS
Z_LRU_X = 0
Z_LRU_GATE = Z_LRU_X + D_LRU
Z_CQ = Z_LRU_GATE + D_LRU
Z_CKV = Z_CQ + Q_LORA_RANK
Z_KPE = Z_CKV + KV_LORA_RANK
Z_WIDTH = Z_KPE + LANES
MXU_TILE = 256
LRU_GROUP = MXU_TILE

TS_IN = 512
IN_TILES_PER_STEP = 2
Q_TILES_PER_STEP = 4
SCORE_LOOKAHEAD = 2
TM_MIX = 1024
TM_MEM = 1024
ROW_SUB = 256
FF_CHUNKS = ((0, 1536), (1536, D_FF))
VMEM_LIMIT = 60 * 1024 * 1024


def _rms(x, g):
    ms = jnp.mean(x * x, axis=-1, keepdims=True)
    return x * lax.rsqrt(ms + RMS_EPS) * g


def _rms_plain(x):
    ms = jnp.mean(x * x, axis=-1, keepdims=True)
    return x * lax.rsqrt(ms + RMS_EPS)


def _gelu_tanh(x):
    k1 = -2.0 * math.sqrt(2.0 / math.pi) * math.log2(math.e)
    k2 = k1 * 0.044715
    return x / (1.0 + jnp.exp2(x * (k1 + k2 * (x * x))))


def _dot(a, b):
    return jnp.dot(a, b, preferred_element_type=F32)


def _dot_nt(a, b):
    return lax.dot_general(a, b, (((1,), (1,)), ((), ())), preferred_element_type=F32)


def _const_spec(shape):
    return pl.BlockSpec(shape, lambda *_: (0,) * len(shape))


def _shift_rows(x, prev, d):
    r = pltpu.roll(x, d, 0)
    row = lax.broadcasted_iota(jnp.int32, prev.shape, 0)
    head = jnp.where(row < d, pltpu.roll(prev, d, 0), r[0:SUBLANES])
    return jnp.concatenate([head, r[SUBLANES:]], axis=0)


def _linear_scan(a, u, h0):
    t, c = a.shape
    groups = t // SUBLANES
    a = a.reshape(groups, SUBLANES, c)
    u = u.reshape(groups, SUBLANES, c)
    row = lax.broadcasted_iota(jnp.int32, a.shape, 1)
    for s in (1, 2, 4):
        a_s = pltpu.roll(a, s, 1)
        u_s = pltpu.roll(u, s, 1)
        m = row >= s
        u = jnp.where(m, a * u_s + u, u)
        a = jnp.where(m, a * a_s, a)
    hs = []
    h = h0
    for g in range(groups):
        hg = a[g] * h + u[g]
        hs.append(hg)
        h = hg[SUBLANES - 1:SUBLANES]
    return jnp.concatenate(hs, axis=0)


def _rope(t, cos, sin):
    half = QK_ROPE_DIM // 2
    return t * cos + pltpu.roll(t, half, 1) * sin - pltpu.roll(t, LANES - half, 1) * sin


def _in_proj_kernel(x_ref, pos_ref, invf_ref, win_ref, convw_ref, convb_ref, wg_ref,
                    ba_ref, bx_ref, lam_ref, wuqt_ref, wkn_ref, wvt_ref,
                    qt_ref, k_ref, vt_ref, ylru_ref, xprev_ref, hprev_ref):
    ts = TS_IN
    tiles = range(x_ref.shape[0] // ts)
    rows = [slice(n * ts, (n + 1) * ts) for n in tiles]
    half = QK_ROPE_DIM // 2

    @pl.when(pl.program_id(1) == 0)
    def _():
        xprev_ref[...] = jnp.zeros_like(xprev_ref)
        hprev_ref[...] = jnp.zeros_like(hprev_ref)

    xn = [_rms_plain(x_ref[r, :]).astype(BF16) for r in rows]
    z_tiles = [{} for _ in tiles]

    def z(n, lo, hi):
        parts = []
        for j in range(lo // MXU_TILE, -(-hi // MXU_TILE)):
            if j not in z_tiles[n]:
                z_tiles[n][j] = _dot(xn[n], win_ref[:, j * MXU_TILE:(j + 1) * MXU_TILE])
            a, b = max(lo, j * MXU_TILE), min(hi, (j + 1) * MXU_TILE)
            parts.append(z_tiles[n][j][:, a - j * MXU_TILE:b - j * MXU_TILE])
        return parts[0] if len(parts) == 1 else jnp.concatenate(parts, axis=-1)

    xl = [z(n, Z_LRU_X, Z_LRU_GATE) for n in tiles]
    u = []
    prev = xprev_ref[...]
    for n in tiles:
        un = convb_ref[...] + convw_ref[CONV_WIDTH - 1:CONV_WIDTH, :] * xl[n]
        for d in range(1, CONV_WIDTH):
            k = CONV_WIDTH - 1 - d
            un = un + convw_ref[k:k + 1, :] * _shift_rows(xl[n], prev, d)
        u.append(un)
        prev = xl[n][ts - SUBLANES:ts]
    xprev_ref[...] = prev
    pre = [[_dot(u[n][:, j * LRU_GROUP:(j + 1) * LRU_GROUP].astype(BF16), wg_ref[j])
            for j in range(D_LRU // LRU_GROUP)] for n in tiles]

    z_gate = [z(n, Z_LRU_GATE, Z_CQ) for n in tiles]
    z_cq = [z(n, Z_CQ, Z_CKV) for n in tiles]
    z_ckv = [z(n, Z_CKV, Z_KPE) for n in tiles]
    z_kpe = [z(n, Z_KPE, Z_WIDTH) for n in tiles]
    cqn = [_rms_plain(v) for v in z_cq]
    ckvn = [_rms_plain(v) for v in z_ckv]
    q_t = [_dot(wuqt_ref[...], v.T.astype(BF16)) for v in cqn]
    v_t = [_dot(wvt_ref[...], v.T.astype(BF16)) for v in ckvn]
    kn = [_dot(v.astype(BF16), wkn_ref[...]) for v in ckvn]

    nlam = -lam_ref[...]
    softplus = jnp.maximum(nlam, 0.0) + jnp.log1p(jnp.exp(-jnp.abs(nlam)))
    h_last = hprev_ref[SUBLANES - 1:SUBLANES, :]
    for n in tiles:
        r = 1.0 / (1.0 + jnp.exp2(
            jnp.concatenate([p[:, :LRU_GROUP] for p in pre[n]], axis=-1) + ba_ref[...]))
        gi = 1.0 / (1.0 + jnp.exp2(
            jnp.concatenate([p[:, LRU_GROUP:] for p in pre[n]], axis=-1) + bx_ref[...]))
        a = jnp.exp2(r * ((-LRU_C * math.log2(math.e)) * softplus))
        one_m_a2 = 1.0 - a * a
        root = jnp.where(one_m_a2 > 0.0, one_m_a2 * lax.rsqrt(one_m_a2), 0.0)
        h = _linear_scan(a, root * (gi * u[n]), h_last)
        h_last = h[ts - 1:ts]
        if n == tiles[-1]:
            hprev_ref[...] = h[ts - SUBLANES:ts]
        ylru_ref[rows[n], :] = _rms_plain(h * _gelu_tanh(z_gate[n])).astype(BF16)

    for n in tiles:
        ang_t = invf_ref[...] * pos_ref[:, rows[n]].astype(F32)
        cos_t = jnp.cos(ang_t)
        sin_t = jnp.sin(ang_t)
        for hd in range(MLA_HEADS):
            r0 = hd * HEAD_SLAB
            pe = r0 + QK_NOPE_DIM
            x1 = q_t[n][pe:pe + half]
            x2 = q_t[n][pe + half:pe + 2 * half]
            qt_ref[n, r0:pe, :] = q_t[n][r0:pe].astype(BF16)
            qt_ref[n, pe:pe + half, :] = (x1 * cos_t - x2 * sin_t).astype(BF16)
            qt_ref[n, pe + half:pe + 2 * half, :] = (x2 * cos_t + x1 * sin_t).astype(BF16)
            qt_ref[n, pe + 2 * half:r0 + HEAD_SLAB, :] = jnp.zeros(
                (HEAD_SLAB - QK_NOPE_DIM - 2 * half, ts), BF16)
            v0 = hd * VT_ROWS
            vt_ref[n, v0:v0 + V_HEAD_DIM, :] = v_t[n][hd * V_HEAD_DIM:(hd + 1) * V_HEAD_DIM].astype(BF16)
            vt_ref[n, v0 + V_HEAD_DIM:v0 + VT_ROWS, :] = jnp.ones((BF16_ROWS, ts), BF16)
        pad = jnp.zeros((LANES - 2 * half, ts), F32)
        cos = jnp.concatenate([cos_t, cos_t, pad], axis=0).T
        sin = jnp.concatenate([sin_t, sin_t, pad], axis=0).T
        kpe = _rope(z_kpe[n], cos, sin).astype(BF16)
        for hd in range(MLA_HEADS):
            c0 = hd * HEAD_SLAB
            k_ref[rows[n], c0:c0 + LANES] = kn[n][:, hd * LANES:(hd + 1) * LANES].astype(BF16)
            k_ref[rows[n], c0 + LANES:c0 + HEAD_SLAB] = kpe


def _in_proj(x, pos_row, invf, win, convw, convb, wg, ba, bx, lam, wuqt, wkn, wvt):
    bsz, s, _ = x.shape
    ts = TS_IN
    step = TS_IN * IN_TILES_PER_STEP
    tok = lambda w: pl.BlockSpec((None, step, w), lambda b, i: (b, i, 0))
    in_specs = [
        tok(D_MODEL), pl.BlockSpec((None, 1, step), lambda b, i: (b, 0, i)),
        _const_spec((QK_ROPE_DIM // 2, 1)),
        _const_spec((D_MODEL, Z_WIDTH)), _const_spec((CONV_WIDTH, D_LRU)), _const_spec((1, D_LRU)),
        _const_spec((D_LRU // LRU_GROUP, LRU_GROUP, 2 * LRU_GROUP)),
        _const_spec((1, D_LRU)), _const_spec((1, D_LRU)), _const_spec((1, D_LRU)),
        _const_spec((MLA_HEADS * HEAD_SLAB, Q_LORA_RANK)),
        _const_spec((KV_LORA_RANK, MLA_HEADS * QK_NOPE_DIM)),
        _const_spec((D_MLA_OUT, KV_LORA_RANK)),
    ]
    tile_t = lambda w: pl.BlockSpec((None, IN_TILES_PER_STEP, w, ts), lambda b, i: (b, i, 0, 0))
    out_shape = (
        jax.ShapeDtypeStruct((bsz, s // ts, MLA_HEADS * HEAD_SLAB, ts), BF16),
        jax.ShapeDtypeStruct((bsz, s, MLA_HEADS * HEAD_SLAB), BF16),
        jax.ShapeDtypeStruct((bsz, s // ts, MLA_HEADS * VT_ROWS, ts), BF16),
        jax.ShapeDtypeStruct((bsz, s, D_LRU), BF16),
    )
    out_specs = (tile_t(MLA_HEADS * HEAD_SLAB), tok(MLA_HEADS * HEAD_SLAB),
                 tile_t(MLA_HEADS * VT_ROWS), tok(D_LRU))
    return pl.pallas_call(
        _in_proj_kernel,
        out_shape=out_shape,
        grid=(bsz, s // step),
        in_specs=in_specs,
        out_specs=out_specs,
        scratch_shapes=[pltpu.VMEM((SUBLANES, D_LRU), F32), pltpu.VMEM((SUBLANES, D_LRU), F32)],
        compiler_params=pltpu.CompilerParams(dimension_semantics=("parallel", "arbitrary"),
                                             vmem_limit_bytes=VMEM_LIMIT),
        name="in_proj",
    )(x, pos_row, invf, win, convw, convb, wg, ba, bx, lam, wuqt, wkn, wvt)


def _mla_attn_kernel(qt_ref, k_ref, vt_ref, o_ref):
    n_local, _, tq = qt_ref.shape
    nq = vt_ref.shape[0]
    i = pl.program_id(1)
    hq = tq // 2

    def scores(ql, j, hd, diagonal):
        slab = slice(hd * HEAD_SLAB, (hd + 1) * HEAD_SLAB)
        if not diagonal:
            st = _dot(k_ref[j * tq:(j + 1) * tq, slab], qt_ref[ql, slab, :])
            return [(0, 0, False, st[:, :hq]), (1, 0, False, st[:, hq:])]
        return [(0, 0, True, _dot(k_ref[j * tq:j * tq + hq, slab], qt_ref[ql, slab, 0:hq])),
                (1, 0, True, _dot(k_ref[j * tq:(j + 1) * tq, slab], qt_ref[ql, slab, hq:tq]))]

    def finish(ql, acc):
        outs = []
        for hd in range(MLA_HEADS):
            a = jnp.concatenate(acc[hd], axis=1)
            outs.append(a[:V_HEAD_DIM] * (1.0 / a[V_HEAD_DIM:V_HEAD_DIM + 1]))
        out_t = jnp.concatenate(outs, axis=0)
        ms = jnp.mean(out_t * out_t, axis=0, keepdims=True)
        o_ref[ql * tq:(ql + 1) * tq, :] = (out_t * lax.rsqrt(ms + RMS_EPS)).astype(BF16).T

    def attend(first_tile):
        units = [(ql, j, hd, j == first_tile + ql)
                 for ql in range(n_local) for j in range(first_tile + ql + 1)
                 for hd in range(MLA_HEADS)]
        ahead = [scores(*u) for u in units[:SCORE_LOOKAHEAD]]
        m = acc = None
        for n, (ql, j, hd, diagonal) in enumerate(units):
            if j == 0 and hd == 0:
                m = [[jnp.full((1, hq), NEG_INF, F32)] * 2 for _ in range(MLA_HEADS)]
                acc = [[None, None] for _ in range(MLA_HEADS)]
            pieces = ahead.pop(0)
            if n + SCORE_LOOKAHEAD < len(units):
                ahead.append(scores(*units[n + SCORE_LOOKAHEAD]))
            vrows = slice(hd * VT_ROWS, (hd + 1) * VT_ROWS)
            for half, k0, masked, st in pieces:
                nk = st.shape[0]
                if masked:
                    kid = lax.broadcasted_iota(jnp.int32, (nk, hq), 0) + k0
                    qid = lax.broadcasted_iota(jnp.int32, (nk, hq), 1) + half * hq
                    st = jnp.where(qid >= kid, st, NEG_INF)
                m_old = m[hd][half]
                m_new = jnp.maximum(m_old, jnp.max(st, axis=0, keepdims=True))
                p = jnp.exp2(st - m_new).astype(BF16)
                pv = _dot(vt_ref[j, vrows, k0:k0 + nk], p)
                old = acc[hd][half]
                acc[hd][half] = pv if old is None else jnp.exp2(m_old - m_new) * old + pv
                m[hd][half] = m_new
            if diagonal and hd == MLA_HEADS - 1:
                finish(ql, acc)

    for n in range(nq // n_local):
        @pl.when(i == n)
        def _(n=n):
            attend(n * n_local)


def _mla_attn(qt, k, vt):
    bsz, nq, _, tq = qt.shape
    s = k.shape[1]
    per = Q_TILES_PER_STEP
    return pl.pallas_call(
        _mla_attn_kernel,
        out_shape=jax.ShapeDtypeStruct((bsz, s, D_MLA_OUT), BF16),
        grid=(bsz, nq // per),
        in_specs=[
            pl.BlockSpec((None, per, MLA_HEADS * HEAD_SLAB, tq), lambda b, i: (b, i, 0, 0)),
            pl.BlockSpec((None, s, MLA_HEADS * HEAD_SLAB), lambda b, i: (b, 0, 0)),
            pl.BlockSpec((None, nq, MLA_HEADS * VT_ROWS, tq), lambda b, i: (b, 0, 0, 0)),
        ],
        out_specs=pl.BlockSpec((None, per * tq, D_MLA_OUT), lambda b, i: (b, i, 0)),
        compiler_params=pltpu.CompilerParams(dimension_semantics=("parallel", "arbitrary"),
                                             vmem_limit_bytes=VMEM_LIMIT),
        name="mla_attn",
    )(qt, k, vt)


def _mem_kv_kernel(mem_ref, gmem_ref, wmk_ref, wmv_ref, mk_ref, mv_ref):
    mn = _rms(mem_ref[...], gmem_ref[...]).astype(BF16)
    mk_ref[...] = _dot(mn, wmk_ref[...]).astype(BF16)
    mv_ref[...] = _dot(mn, wmv_ref[...]).astype(BF16)


def _mem_kv(mem, gmem, wmk, wmv):
    m, _ = mem.shape
    tm = TM_MEM
    tok = pl.BlockSpec((tm, D_MODEL), lambda i: (i, 0))
    sq = _const_spec((D_MODEL, D_MODEL))
    out = jax.ShapeDtypeStruct((m, D_MODEL), BF16)
    return pl.pallas_call(
        _mem_kv_kernel,
        out_shape=(out, out),
        grid=(m // tm,),
        in_specs=[tok, _const_spec((1, D_MODEL)), sq, sq],
        out_specs=(tok, tok),
        compiler_params=pltpu.CompilerParams(dimension_semantics=("parallel",),
                                             vmem_limit_bytes=VMEM_LIMIT),
        name="mem_kv",
    )(mem, gmem, wmk, wmv)


def _mix_ffn_kernel(ylru_ref, ymla_ref, x_ref, mk_ref, mv_ref, wout_ref, gpost_ref, gpre_ref,
                    wmq_ref, wmo_ref, gpm_ref, gpf_ref, wg_ref, wu_ref, wd_ref, gqf_ref,
                    h_ref, o_scr):
    subs = [slice(r, r + ROW_SUB) for r in range(0, x_ref.shape[0], ROW_SUB)]
    yo = [_dot(ylru_ref[r, :], wout_ref[0:D_LRU, :])
          + _dot(ymla_ref[r, :], wout_ref[D_LRU:, :]) for r in subs]
    h1 = [x_ref[r, :] + _rms(y, gpost_ref[...]) for r, y in zip(subs, yo)]

    xn = [_rms(h, gpre_ref[...]).astype(BF16) for h in h1]
    mq = [(_dot(v, wmq_ref[...]) * (1.0 / math.sqrt(MEM_HEAD_DIM))).astype(BF16) for v in xn]
    for r, q in zip(subs, mq):
        for hd in range(MEM_HEADS):
            sl = slice(hd * MEM_HEAD_DIM, (hd + 1) * MEM_HEAD_DIM)
            s = _dot_nt(q[:, sl], mk_ref[:, sl])
            m = jnp.max(s, axis=-1, keepdims=True)
            p = jnp.exp(s - m)
            l = jnp.sum(p, axis=-1, keepdims=True)
            o_scr[r, sl] = (_dot(p.astype(BF16), mv_ref[:, sl]) / l).astype(BF16)
    mo = [_dot(o_scr[r, :], wmo_ref[...]) for r in subs]
    for r, h, o in zip(subs, h1, mo):
        h_ref[r, :] = h + _rms(o, gpm_ref[...])

    xn = [_rms(h_ref[r, :], gpf_ref[...]).astype(BF16) for r in subs]
    f = [None] * len(subs)
    for lo, hi in FF_CHUNKS:
        for n, v in enumerate(xn):
            act = (jax.nn.silu(_dot(v, wg_ref[:, lo:hi])) * _dot(v, wu_ref[:, lo:hi])).astype(BF16)
            part = _dot(act, wd_ref[lo:hi, :])
            f[n] = part if f[n] is None else f[n] + part
    for r, fr in zip(subs, f):
        h_ref[r, :] = h_ref[r, :] + _rms(fr, gqf_ref[...])


def _mix_ffn(ylru, ymla, x, mk, mv, wout, gpost, gpre, wmq, wmo, gpm, gpf, wg, wu, wd, gqf):
    bsz, s, _ = x.shape
    tm = TM_MIX
    single = pl.Buffered(1)
    tok = lambda w: pl.BlockSpec((None, tm, w), lambda b, i: (b, i, 0))
    memspec = pl.BlockSpec((None, N_MEM, D_MODEL), lambda b, i: (b, 0, 0))
    sq = pl.BlockSpec((D_MODEL, D_MODEL), lambda b, i: (0, 0), pipeline_mode=single)
    wide = pl.BlockSpec((D_MODEL, D_FF), lambda b, i: (0, 0), pipeline_mode=single)
    tall = pl.BlockSpec((D_FF, D_MODEL), lambda b, i: (0, 0), pipeline_mode=single)
    vec = _const_spec((1, D_MODEL))
    return pl.pallas_call(
        _mix_ffn_kernel,
        out_shape=jax.ShapeDtypeStruct((bsz, s, D_MODEL), F32),
        grid=(bsz, s // tm),
        in_specs=[tok(D_LRU), tok(D_MLA_OUT), tok(D_MODEL), memspec, memspec, sq, vec, vec, sq,
                  sq, vec, vec, wide, wide, tall, vec],
        out_specs=tok(D_MODEL),
        scratch_shapes=[pltpu.VMEM((tm, D_MODEL), BF16)],
        compiler_params=pltpu.CompilerParams(dimension_semantics=("parallel", "arbitrary"),
                                             vmem_limit_bytes=VMEM_LIMIT),
        name="mix_ffn",
    )(ylru, ymla, x, mk, mv, wout, gpost, gpre, wmq, wmo, gpm, gpf, wg, wu, wd, gqf)


def _block_diag(w):
    eye = jnp.eye(LRU_BLOCKS, dtype=w.dtype)
    return jnp.einsum('hij,hg->higj', w, eye).reshape(D_LRU, D_LRU)


def _gate_weights(wa, wx):
    da, dx = _block_diag(wa) * -math.log2(math.e), _block_diag(wx) * -math.log2(math.e)
    groups = []
    for j in range(D_LRU // LRU_GROUP):
        sl = slice(j * LRU_GROUP, (j + 1) * LRU_GROUP)
        groups.append(jnp.concatenate([da[sl, sl], dx[sl, sl]], axis=1))
    return jnp.stack(groups).astype(BF16)


def _pad_in_proj(w_in, gain):
    w = gain[:, None] * w_in
    pad = jnp.zeros((D_MODEL, Z_WIDTH - w.shape[1]), w.dtype)
    return jnp.concatenate([w, pad], axis=1).astype(BF16)


def _q_weights_t(w_uq, gain):
    dh = QK_NOPE_DIM + QK_ROPE_DIM
    scale = math.log2(math.e) / math.sqrt(dh)
    w = (scale * gain[:, None] * w_uq).reshape(Q_LORA_RANK, MLA_HEADS, dh)
    pad = jnp.zeros((Q_LORA_RANK, MLA_HEADS, HEAD_SLAB - dh), w.dtype)
    w = jnp.concatenate([w, pad], axis=-1).reshape(Q_LORA_RANK, MLA_HEADS * HEAD_SLAB)
    return w.T.astype(BF16)


def _kv_weights(w_ukv, gain):
    w = (gain[:, None] * w_ukv).reshape(KV_LORA_RANK, MLA_HEADS, QK_NOPE_DIM + V_HEAD_DIM)
    kn = w[:, :, :QK_NOPE_DIM].reshape(KV_LORA_RANK, MLA_HEADS * QK_NOPE_DIM)
    vv = w[:, :, QK_NOPE_DIM:].reshape(KV_LORA_RANK, MLA_HEADS * V_HEAD_DIM)
    return kn.astype(BF16), vv.T.astype(BF16)


def _rope_freq_column():
    inv_freq = ROPE_THETA ** (-jnp.arange(0, QK_ROPE_DIM, 2, dtype=F32) / QK_ROPE_DIM)
    return inv_freq.reshape(QK_ROPE_DIM // 2, 1)


def kernel(x, mem, positions, g_pre_mix, w_in, conv_w, conv_b, lru_wa, lru_ba, lru_wx, lru_bx, lru_lambda, g_q_lat, w_uq, g_kv_lat, w_ukv, g_lru_out, g_mla_out, w_out, g_post_mix, g_pre_mem, g_mem_kv, w_mq, w_mk, w_mv, w_mo, g_post_mem, g_pre_ffn, w_gate, w_up, w_down, g_post_ffn):
    bsz, s, d = x.shape
    depth = w_in.shape[0]
    assert d == D_MODEL and s % (TS_IN * max(IN_TILES_PER_STEP, Q_TILES_PER_STEP)) == 0
    assert s % TM_MIX == 0 and (bsz * N_MEM) % TM_MEM == 0 and mem.shape == (bsz, N_MEM, D_MODEL)
    pos_row = positions.reshape(bsz, 1, s)
    invf = _rope_freq_column()
    row = lambda a: a.reshape(1, -1)

    h = x
    for l in range(depth):
        wkn, wvt = _kv_weights(w_ukv[l], g_kv_lat[l])
        neg_log2e = -math.log2(math.e)
        qt, k, vt, ylru = _in_proj(
            h, pos_row, invf, _pad_in_proj(w_in[l], g_pre_mix[l]), conv_w[l], row(conv_b[l]),
            _gate_weights(lru_wa[l], lru_wx[l]), row(neg_log2e * lru_ba[l]),
            row(neg_log2e * lru_bx[l]), row(lru_lambda[l]),
            _q_weights_t(w_uq[l], g_q_lat[l]), wkn, wvt)
        ymla = _mla_attn(qt, k, vt)
        wout = (jnp.concatenate([g_lru_out[l], g_mla_out[l]])[:, None] * w_out[l]).astype(BF16)
        mk, mv = _mem_kv(mem.reshape(bsz * N_MEM, d), row(g_mem_kv[l]), w_mk[l].astype(BF16),
                         w_mv[l].astype(BF16))
        h = _mix_ffn(ylru, ymla, h, mk.reshape(bsz, N_MEM, d), mv.reshape(bsz, N_MEM, d), wout,
                     row(g_post_mix[l]), row(g_pre_mem[l]), w_mq[l].astype(BF16),
                     w_mo[l].astype(BF16), row(g_post_mem[l]), row(g_pre_ffn[l]),
                     w_gate[l].astype(BF16), w_up[l].astype(BF16), w_down[l].astype(BF16),
                     row(g_post_ffn[l]))
    return h
```

```python
import math

import jax
import jax.numpy as jnp
from jax import lax
from jax.experimental import pallas as pl
from jax.experimental.pallas import tpu as pltpu

F32 = jnp.float32
BF16 = jnp.bfloat16

D_MODEL = 1024
N_MEM = 256
MEM_HEADS = 4
MEM_HEAD_DIM = D_MODEL // MEM_HEADS
D_LRU = 512
LRU_BLOCKS = 8
LRU_BLOCK_DIM = D_LRU // LRU_BLOCKS
CONV_WIDTH = 4
LRU_C = 8.0
MLA_HEADS = 4
QK_NOPE_DIM = 128
QK_ROPE_DIM = 64
V_HEAD_DIM = 128
Q_LORA_RANK = 384
KV_LORA_RANK = 256
D_MLA_OUT = MLA_HEADS * V_HEAD_DIM
ROPE_THETA = 10000.0
D_FF = 2816
RMS_EPS = 1e-6
NEG_INF = -1e30

LANES = 128
SUBLANES = 8
HEAD_SLAB = 2 * LANES
BF16_ROWS = 16
VT_ROWS = V_HEAD_DIM + BF16_ROWS
Z_LRU_X = 0
Z_LRU_GATE = Z_LRU_X + D_LRU
Z_CQ = Z_LRU_GATE + D_LRU
Z_CKV = Z_CQ + Q_LORA_RANK
Z_KPE = Z_CKV + KV_LORA_RANK
Z_WIDTH = Z_KPE + LANES
MXU_TILE = 256
LRU_GROUP = MXU_TILE

TS_IN = 512
IN_TILES_PER_STEP = 4
Q_TILES_PER_STEP = 4
SCORE_LOOKAHEAD = 2
TM_MIX = 1024
TM_MEM = 1024
ROW_SUB = 256
FF_CHUNKS = ((0, 1536), (1536, D_FF))
VMEM_LIMIT = 60 * 1024 * 1024


def _rms(x, g):
    ms = jnp.mean(x * x, axis=-1, keepdims=True)
    return x * lax.rsqrt(ms + RMS_EPS) * g


def _rms_plain(x):
    ms = jnp.mean(x * x, axis=-1, keepdims=True)
    return x * lax.rsqrt(ms + RMS_EPS)


def _gelu_tanh(x):
    k1 = -2.0 * math.sqrt(2.0 / math.pi) * math.log2(math.e)
    k2 = k1 * 0.044715
    return x / (1.0 + jnp.exp2(x * (k1 + k2 * (x * x))))


def _dot(a, b):
    return jnp.dot(a, b, preferred_element_type=F32)


def _dot_nt(a, b):
    return lax.dot_general(a, b, (((1,), (1,)), ((), ())), preferred_element_type=F32)


def _const_spec(shape):
    return pl.BlockSpec(shape, lambda *_: (0,) * len(shape))


def _shift_rows(x, prev, d):
    r = pltpu.roll(x, d, 0)
    row = lax.broadcasted_iota(jnp.int32, prev.shape, 0)
    head = jnp.where(row < d, pltpu.roll(prev, d, 0), r[0:SUBLANES])
    return jnp.concatenate([head, r[SUBLANES:]], axis=0)


def _linear_scan(a, u, h0):
    t, c = a.shape
    groups = t // SUBLANES
    a = a.reshape(groups, SUBLANES, c)
    u = u.reshape(groups, SUBLANES, c)
    row = lax.broadcasted_iota(jnp.int32, a.shape, 1)
    for s in (1, 2, 4):
        a_s = pltpu.roll(a, s, 1)
        u_s = pltpu.roll(u, s, 1)
        m = row >= s
        u = jnp.where(m, a * u_s + u, u)
        a = jnp.where(m, a * a_s, a)
    hs = []
    h = h0
    for g in range(groups):
        hg = a[g] * h + u[g]
        hs.append(hg)
        h = hg[SUBLANES - 1:SUBLANES]
    return jnp.concatenate(hs, axis=0)


def _rope(t, cos, sin):
    half = QK_ROPE_DIM // 2
    return t * cos + pltpu.roll(t, half, 1) * sin - pltpu.roll(t, LANES - half, 1) * sin


def _in_proj_kernel(x_ref, pos_ref, invf_ref, win_ref, convw_ref, convb_ref, wg_ref,
                    ba_ref, bx_ref, lam_ref, wuqt_ref, wkn_ref, wvt_ref,
                    qt_ref, k_ref, vt_ref, ylru_ref, xprev_ref, hprev_ref):
    ts = TS_IN
    tiles = range(x_ref.shape[0] // ts)
    rows = [slice(n * ts, (n + 1) * ts) for n in tiles]
    half = QK_ROPE_DIM // 2

    @pl.when(pl.program_id(1) == 0)
    def _():
        xprev_ref[...] = jnp.zeros_like(xprev_ref)
        hprev_ref[...] = jnp.zeros_like(hprev_ref)

    xn = [_rms_plain(x_ref[r, :]).astype(BF16) for r in rows]
    z_tiles = [{} for _ in tiles]

    def z(n, lo, hi):
        parts = []
        for j in range(lo // MXU_TILE, -(-hi // MXU_TILE)):
            if j not in z_tiles[n]:
                z_tiles[n][j] = _dot(xn[n], win_ref[:, j * MXU_TILE:(j + 1) * MXU_TILE])
            a, b = max(lo, j * MXU_TILE), min(hi, (j + 1) * MXU_TILE)
            parts.append(z_tiles[n][j][:, a - j * MXU_TILE:b - j * MXU_TILE])
        return parts[0] if len(parts) == 1 else jnp.concatenate(parts, axis=-1)

    xl = [z(n, Z_LRU_X, Z_LRU_GATE) for n in tiles]
    u = []
    prev = xprev_ref[...]
    for n in tiles:
        un = convb_ref[...] + convw_ref[CONV_WIDTH - 1:CONV_WIDTH, :] * xl[n]
        for d in range(1, CONV_WIDTH):
            k = CONV_WIDTH - 1 - d
            un = un + convw_ref[k:k + 1, :] * _shift_rows(xl[n], prev, d)
        u.append(un)
        prev = xl[n][ts - SUBLANES:ts]
    xprev_ref[...] = prev
    pre = [[_dot(u[n][:, j * LRU_GROUP:(j + 1) * LRU_GROUP].astype(BF16), wg_ref[j])
            for j in range(D_LRU // LRU_GROUP)] for n in tiles]

    z_gate = [z(n, Z_LRU_GATE, Z_CQ) for n in tiles]
    z_cq = [z(n, Z_CQ, Z_CKV) for n in tiles]
    z_ckv = [z(n, Z_CKV, Z_KPE) for n in tiles]
    z_kpe = [z(n, Z_KPE, Z_WIDTH) for n in tiles]
    cqn = [_rms_plain(v) for v in z_cq]
    ckvn = [_rms_plain(v) for v in z_ckv]
    q_t = [_dot(wuqt_ref[...], v.T.astype(BF16)) for v in cqn]
    v_t = [_dot(wvt_ref[...], v.T.astype(BF16)) for v in ckvn]
    kn = [_dot(v.astype(BF16), wkn_ref[...]) for v in ckvn]

    nlam = -lam_ref[...]
    softplus = jnp.maximum(nlam, 0.0) + jnp.log1p(jnp.exp(-jnp.abs(nlam)))
    h_last = hprev_ref[SUBLANES - 1:SUBLANES, :]
    for n in tiles:
        r = 1.0 / (1.0 + jnp.exp2(
            jnp.concatenate([p[:, :LRU_GROUP] for p in pre[n]], axis=-1) + ba_ref[...]))
        gi = 1.0 / (1.0 + jnp.exp2(
            jnp.concatenate([p[:, LRU_GROUP:] for p in pre[n]], axis=-1) + bx_ref[...]))
        a = jnp.exp2(r * ((-LRU_C * math.log2(math.e)) * softplus))
        one_m_a2 = 1.0 - a * a
        root = jnp.where(one_m_a2 > 0.0, one_m_a2 * lax.rsqrt(one_m_a2), 0.0)
        h = _linear_scan(a, root * (gi * u[n]), h_last)
        h_last = h[ts - 1:ts]
        if n == tiles[-1]:
            hprev_ref[...] = h[ts - SUBLANES:ts]
        ylru_ref[rows[n], :] = _rms_plain(h * _gelu_tanh(z_gate[n])).astype(BF16)

    for n in tiles:
        ang_t = invf_ref[...] * pos_ref[:, rows[n]].astype(F32)
        cos_t = jnp.cos(ang_t)
        sin_t = jnp.sin(ang_t)
        for hd in range(MLA_HEADS):
            r0 = hd * HEAD_SLAB
            pe = r0 + QK_NOPE_DIM
            x1 = q_t[n][pe:pe + half]
            x2 = q_t[n][pe + half:pe + 2 * half]
            qt_ref[n, r0:pe, :] = q_t[n][r0:pe].astype(BF16)
            qt_ref[n, pe:pe + half, :] = (x1 * cos_t - x2 * sin_t).astype(BF16)
            qt_ref[n, pe + half:pe + 2 * half, :] = (x2 * cos_t + x1 * sin_t).astype(BF16)
            qt_ref[n, pe + 2 * half:r0 + HEAD_SLAB, :] = jnp.zeros(
                (HEAD_SLAB - QK_NOPE_DIM - 2 * half, ts), BF16)
            v0 = hd * VT_ROWS
            vt_ref[n, v0:v0 + V_HEAD_DIM, :] = v_t[n][hd * V_HEAD_DIM:(hd + 1) * V_HEAD_DIM].astype(BF16)
            vt_ref[n, v0 + V_HEAD_DIM:v0 + VT_ROWS, :] = jnp.ones((BF16_ROWS, ts), BF16)
        pad = jnp.zeros((LANES - 2 * half, ts), F32)
        cos = jnp.concatenate([cos_t, cos_t, pad], axis=0).T
        sin = jnp.concatenate([sin_t, sin_t, pad], axis=0).T
        kpe = _rope(z_kpe[n], cos, sin).astype(BF16)
        for hd in range(MLA_HEADS):
            c0 = hd * HEAD_SLAB
            k_ref[rows[n], c0:c0 + LANES] = kn[n][:, hd * LANES:(hd + 1) * LANES].astype(BF16)
            k_ref[rows[n], c0 + LANES:c0 + HEAD_SLAB] = kpe


def _in_proj(x, pos_row, invf, win, convw, convb, wg, ba, bx, lam, wuqt, wkn, wvt):
    bsz, s, _ = x.shape
    ts = TS_IN
    step = TS_IN * IN_TILES_PER_STEP
    tok = lambda w: pl.BlockSpec((None, step, w), lambda b, i: (b, i, 0))
    in_specs = [
        tok(D_MODEL), pl.BlockSpec((None, 1, step), lambda b, i: (b, 0, i)),
        _const_spec((QK_ROPE_DIM // 2, 1)),
        _const_spec((D_MODEL, Z_WIDTH)), _const_spec((CONV_WIDTH, D_LRU)), _const_spec((1, D_LRU)),
        _const_spec((D_LRU // LRU_GROUP, LRU_GROUP, 2 * LRU_GROUP)),
        _const_spec((1, D_LRU)), _const_spec((1, D_LRU)), _const_spec((1, D_LRU)),
        _const_spec((MLA_HEADS * HEAD_SLAB, Q_LORA_RANK)),
        _const_spec((KV_LORA_RANK, MLA_HEADS * QK_NOPE_DIM)),
        _const_spec((D_MLA_OUT, KV_LORA_RANK)),
    ]
    tile_t = lambda w: pl.BlockSpec((None, IN_TILES_PER_STEP, w, ts), lambda b, i: (b, i, 0, 0))
    out_shape = (
        jax.ShapeDtypeStruct((bsz, s // ts, MLA_HEADS * HEAD_SLAB, ts), BF16),
        jax.ShapeDtypeStruct((bsz, s, MLA_HEADS * HEAD_SLAB), BF16),
        jax.ShapeDtypeStruct((bsz, s // ts, MLA_HEADS * VT_ROWS, ts), BF16),
        jax.ShapeDtypeStruct((bsz, s, D_LRU), BF16),
    )
    out_specs = (tile_t(MLA_HEADS * HEAD_SLAB), tok(MLA_HEADS * HEAD_SLAB),
                 tile_t(MLA_HEADS * VT_ROWS), tok(D_LRU))
    return pl.pallas_call(
        _in_proj_kernel,
        out_shape=out_shape,
        grid=(bsz, s // step),
        in_specs=in_specs,
        out_specs=out_specs,
        scratch_shapes=[pltpu.VMEM((SUBLANES, D_LRU), F32), pltpu.VMEM((SUBLANES, D_LRU), F32)],
        compiler_params=pltpu.CompilerParams(dimension_semantics=("parallel", "arbitrary"),
                                             vmem_limit_bytes=VMEM_LIMIT),
        name="in_proj",
    )(x, pos_row, invf, win, convw, convb, wg, ba, bx, lam, wuqt, wkn, wvt)


def _mla_attn_kernel(qt_ref, k_ref, vt_ref, o_ref):
    n_local, _, tq = qt_ref.shape
    nq = vt_ref.shape[0]
    i = pl.program_id(1)
    hq = tq // 2

    def scores(ql, j, hd, diagonal):
        slab = slice(hd * HEAD_SLAB, (hd + 1) * HEAD_SLAB)
        if not diagonal:
            st = _dot(k_ref[j * tq:(j + 1) * tq, slab], qt_ref[ql, slab, :])
            return [(0, 0, False, st[:, :hq]), (1, 0, False, st[:, hq:])]
        return [(0, 0, True, _dot(k_ref[j * tq:j * tq + hq, slab], qt_ref[ql, slab, 0:hq])),
                (1, 0, True, _dot(k_ref[j * tq:(j + 1) * tq, slab], qt_ref[ql, slab, hq:tq]))]

    def finish(ql, acc):
        outs = []
        for hd in range(MLA_HEADS):
            a = jnp.concatenate(acc[hd], axis=1)
            outs.append(a[:V_HEAD_DIM] * (1.0 / a[V_HEAD_DIM:V_HEAD_DIM + 1]))
        out_t = jnp.concatenate(outs, axis=0)
        ms = jnp.mean(out_t * out_t, axis=0, keepdims=True)
        o_ref[ql * tq:(ql + 1) * tq, :] = (out_t * lax.rsqrt(ms + RMS_EPS)).astype(BF16).T

    def attend(first_tile):
        units = [(ql, j, hd, j == first_tile + ql)
                 for ql in range(n_local) for j in range(first_tile + ql + 1)
                 for hd in range(MLA_HEADS)]
        ahead = [scores(*u) for u in units[:SCORE_LOOKAHEAD]]
        m = acc = None
        for n, (ql, j, hd, diagonal) in enumerate(units):
            if j == 0 and hd == 0:
                m = [[jnp.full((1, hq), NEG_INF, F32)] * 2 for _ in range(MLA_HEADS)]
                acc = [[None, None] for _ in range(MLA_HEADS)]
            pieces = ahead.pop(0)
            if n + SCORE_LOOKAHEAD < len(units):
                ahead.append(scores(*units[n + SCORE_LOOKAHEAD]))
            vrows = slice(hd * VT_ROWS, (hd + 1) * VT_ROWS)
            for half, k0, masked, st in pieces:
                nk = st.shape[0]
                if masked:
                    kid = lax.broadcasted_iota(jnp.int32, (nk, hq), 0) + k0
                    qid = lax.broadcasted_iota(jnp.int32, (nk, hq), 1) + half * hq
                    st = jnp.where(qid >= kid, st, NEG_INF)
                m_old = m[hd][half]
                m_new = jnp.maximum(m_old, jnp.max(st, axis=0, keepdims=True))
                p = jnp.exp2(st - m_new).astype(BF16)
                pv = _dot(vt_ref[j, vrows, k0:k0 + nk], p)
                old = acc[hd][half]
                acc[hd][half] = pv if old is None else jnp.exp2(m_old - m_new) * old + pv
                m[hd][half] = m_new
            if diagonal and hd == MLA_HEADS - 1:
                finish(ql, acc)

    for n in range(nq // n_local):
        @pl.when(i == n)
        def _(n=n):
            attend(n * n_local)


def _mla_attn(qt, k, vt):
    bsz, nq, _, tq = qt.shape
    s = k.shape[1]
    per = Q_TILES_PER_STEP
    return pl.pallas_call(
        _mla_attn_kernel,
        out_shape=jax.ShapeDtypeStruct((bsz, s, D_MLA_OUT), BF16),
        grid=(bsz, nq // per),
        in_specs=[
            pl.BlockSpec((None, per, MLA_HEADS * HEAD_SLAB, tq), lambda b, i: (b, i, 0, 0)),
            pl.BlockSpec((None, s, MLA_HEADS * HEAD_SLAB), lambda b, i: (b, 0, 0)),
            pl.BlockSpec((None, nq, MLA_HEADS * VT_ROWS, tq), lambda b, i: (b, 0, 0, 0)),
        ],
        out_specs=pl.BlockSpec((None, per * tq, D_MLA_OUT), lambda b, i: (b, i, 0)),
        compiler_params=pltpu.CompilerParams(dimension_semantics=("parallel", "arbitrary"),
                                             vmem_limit_bytes=VMEM_LIMIT),
        name="mla_attn",
    )(qt, k, vt)


def _mem_kv_kernel(mem_ref, gmem_ref, wmk_ref, wmv_ref, mk_ref, mv_ref):
    mn = _rms(mem_ref[...], gmem_ref[...]).astype(BF16)
    mk_ref[...] = _dot(mn, wmk_ref[...]).astype(BF16)
    mv_ref[...] = _dot(mn, wmv_ref[...]).astype(BF16)


def _mem_kv(mem, gmem, wmk, wmv):
    m, _ = mem.shape
    tm = TM_MEM
    tok = pl.BlockSpec((tm, D_MODEL), lambda i: (i, 0))
    sq = _const_spec((D_MODEL, D_MODEL))
    out = jax.ShapeDtypeStruct((m, D_MODEL), BF16)
    return pl.pallas_call(
        _mem_kv_kernel,
        out_shape=(out, out),
        grid=(m // tm,),
        in_specs=[tok, _const_spec((1, D_MODEL)), sq, sq],
        out_specs=(tok, tok),
        compiler_params=pltpu.CompilerParams(dimension_semantics=("parallel",),
                                             vmem_limit_bytes=VMEM_LIMIT),
        name="mem_kv",
    )(mem, gmem, wmk, wmv)


def _mix_ffn_kernel(ylru_ref, ymla_ref, x_ref, mk_ref, mv_ref, wout_ref, gpost_ref, gpre_ref,
                    wmq_ref, wmo_ref, gpm_ref, gpf_ref, wg_ref, wu_ref, wd_ref, gqf_ref,
                    h_ref, o_scr):
    subs = [slice(r, r + ROW_SUB) for r in range(0, x_ref.shape[0], ROW_SUB)]
    yo = [_dot(ylru_ref[r, :], wout_ref[0:D_LRU, :])
          + _dot(ymla_ref[r, :], wout_ref[D_LRU:, :]) for r in subs]
    h1 = [x_ref[r, :] + _rms(y, gpost_ref[...]) for r, y in zip(subs, yo)]

    xn = [_rms(h, gpre_ref[...]).astype(BF16) for h in h1]
    mq = [(_dot(v, wmq_ref[...]) * (1.0 / math.sqrt(MEM_HEAD_DIM))).astype(BF16) for v in xn]
    for r, q in zip(subs, mq):
        for hd in range(MEM_HEADS):
            sl = slice(hd * MEM_HEAD_DIM, (hd + 1) * MEM_HEAD_DIM)
            s = _dot_nt(q[:, sl], mk_ref[:, sl])
            m = jnp.max(s, axis=-1, keepdims=True)
            p = jnp.exp(s - m)
            l = jnp.sum(p, axis=-1, keepdims=True)
            o_scr[r, sl] = (_dot(p.astype(BF16), mv_ref[:, sl]) / l).astype(BF16)
    mo = [_dot(o_scr[r, :], wmo_ref[...]) for r in subs]
    for r, h, o in zip(subs, h1, mo):
        h_ref[r, :] = h + _rms(o, gpm_ref[...])

    xn = [_rms(h_ref[r, :], gpf_ref[...]).astype(BF16) for r in subs]
    f = [None] * len(subs)
    for lo, hi in FF_CHUNKS:
        for n, v in enumerate(xn):
            act = (jax.nn.silu(_dot(v, wg_ref[:, lo:hi])) * _dot(v, wu_ref[:, lo:hi])).astype(BF16)
            part = _dot(act, wd_ref[lo:hi, :])
            f[n] = part if f[n] is None else f[n] + part
    for r, fr in zip(subs, f):
        h_ref[r, :] = h_ref[r, :] + _rms(fr, gqf_ref[...])


def _mix_ffn(ylru, ymla, x, mk, mv, wout, gpost, gpre, wmq, wmo, gpm, gpf, wg, wu, wd, gqf):
    bsz, s, _ = x.shape
    tm = TM_MIX
    single = pl.Buffered(1)
    tok = lambda w: pl.BlockSpec((None, tm, w), lambda b, i: (b, i, 0))
    memspec = pl.BlockSpec((None, N_MEM, D_MODEL), lambda b, i: (b, 0, 0))
    sq = pl.BlockSpec((D_MODEL, D_MODEL), lambda b, i: (0, 0), pipeline_mode=single)
    wide = pl.BlockSpec((D_MODEL, D_FF), lambda b, i: (0, 0), pipeline_mode=single)
    tall = pl.BlockSpec((D_FF, D_MODEL), lambda b, i: (0, 0), pipeline_mode=single)
    vec = _const_spec((1, D_MODEL))
    return pl.pallas_call(
        _mix_ffn_kernel,
        out_shape=jax.ShapeDtypeStruct((bsz, s, D_MODEL), F32),
        grid=(bsz, s // tm),
        in_specs=[tok(D_LRU), tok(D_MLA_OUT), tok(D_MODEL), memspec, memspec, sq, vec, vec, sq,
                  sq, vec, vec, wide, wide, tall, vec],
        out_specs=tok(D_MODEL),
        scratch_shapes=[pltpu.VMEM((tm, D_MODEL), BF16)],
        compiler_params=pltpu.CompilerParams(dimension_semantics=("parallel", "arbitrary"),
                                             vmem_limit_bytes=VMEM_LIMIT),
        name="mix_ffn",
    )(ylru, ymla, x, mk, mv, wout, gpost, gpre, wmq, wmo, gpm, gpf, wg, wu, wd, gqf)


def _block_diag(w):
    eye = jnp.eye(LRU_BLOCKS, dtype=w.dtype)
    return jnp.einsum('hij,hg->higj', w, eye).reshape(D_LRU, D_LRU)


def _gate_weights(wa, wx):
    da, dx = _block_diag(wa) * -math.log2(math.e), _block_diag(wx) * -math.log2(math.e)
    groups = []
    for j in range(D_LRU // LRU_GROUP):
        sl = slice(j * LRU_GROUP, (j + 1) * LRU_GROUP)
        groups.append(jnp.concatenate([da[sl, sl], dx[sl, sl]], axis=1))
    return jnp.stack(groups).astype(BF16)


def _pad_in_proj(w_in, gain):
    w = gain[:, None] * w_in
    pad = jnp.zeros((D_MODEL, Z_WIDTH - w.shape[1]), w.dtype)
    return jnp.concatenate([w, pad], axis=1).astype(BF16)


def _q_weights_t(w_uq, gain):
    dh = QK_NOPE_DIM + QK_ROPE_DIM
    scale = math.log2(math.e) / math.sqrt(dh)
    w = (scale * gain[:, None] * w_uq).reshape(Q_LORA_RANK, MLA_HEADS, dh)
    pad = jnp.zeros((Q_LORA_RANK, MLA_HEADS, HEAD_SLAB - dh), w.dtype)
    w = jnp.concatenate([w, pad], axis=-1).reshape(Q_LORA_RANK, MLA_HEADS * HEAD_SLAB)
    return w.T.astype(BF16)


def _kv_weights(w_ukv, gain):
    w = (gain[:, None] * w_ukv).reshape(KV_LORA_RANK, MLA_HEADS, QK_NOPE_DIM + V_HEAD_DIM)
    kn = w[:, :, :QK_NOPE_DIM].reshape(KV_LORA_RANK, MLA_HEADS * QK_NOPE_DIM)
    vv = w[:, :, QK_NOPE_DIM:].reshape(KV_LORA_RANK, MLA_HEADS * V_HEAD_DIM)
    return kn.astype(BF16), vv.T.astype(BF16)


def _rope_freq_column():
    inv_freq = ROPE_THETA ** (-jnp.arange(0, QK_ROPE_DIM, 2, dtype=F32) / QK_ROPE_DIM)
    return inv_freq.reshape(QK_ROPE_DIM // 2, 1)


def kernel(x, mem, positions, g_pre_mix, w_in, conv_w, conv_b, lru_wa, lru_ba, lru_wx, lru_bx, lru_lambda, g_q_lat, w_uq, g_kv_lat, w_ukv, g_lru_out, g_mla_out, w_out, g_post_mix, g_pre_mem, g_mem_kv, w_mq, w_mk, w_mv, w_mo, g_post_mem, g_pre_ffn, w_gate, w_up, w_down, g_post_ffn):
    bsz, s, d = x.shape
    depth = w_in.shape[0]
    assert d == D_MODEL and s % (TS_IN * max(IN_TILES_PER_STEP, Q_TILES_PER_STEP)) == 0
    assert s % TM_MIX == 0 and (bsz * N_MEM) % TM_MEM == 0 and mem.shape == (bsz, N_MEM, D_MODEL)
    pos_row = positions.reshape(bsz, 1, s)
    invf = _rope_freq_column()
    row = lambda a: a.reshape(1, -1)

    h = x
    for l in range(depth):
        wkn, wvt = _kv_weights(w_ukv[l], g_kv_lat[l])
        neg_log2e = -math.log2(math.e)
        qt, k, vt, ylru = _in_proj(
            h, pos_row, invf, _pad_in_proj(w_in[l], g_pre_mix[l]), conv_w[l], row(conv_b[l]),
            _gate_weights(lru_wa[l], lru_wx[l]), row(neg_log2e * lru_ba[l]),
            row(neg_log2e * lru_bx[l]), row(lru_lambda[l]),
            _q_weights_t(w_uq[l], g_q_lat[l]), wkn, wvt)
        ymla = _mla_attn(qt, k, vt)
        wout = (jnp.concatenate([g_lru_out[l], g_mla_out[l]])[:, None] * w_out[l]).astype(BF16)
        mk, mv = _mem_kv(mem.reshape(bsz * N_MEM, d), row(g_mem_kv[l]), w_mk[l].astype(BF16),
                         w_mv[l].astype(BF16))
        h = _mix_ffn(ylru, ymla, h, mk.reshape(bsz, N_MEM, d), mv.reshape(bsz, N_MEM, d), wout,
                     row(g_post_mix[l]), row(g_pre_mem[l]), w_mq[l].astype(BF16),
                     w_mo[l].astype(BF16), row(g_post_mem[l]), row(g_pre_ffn[l]),
                     w_gate[l].astype(BF16), w_up[l].astype(BF16), w_down[l].astype(BF16),
                     row(g_post_ffn[l]))
    return h
```

```python
import math

import jax
import jax.numpy as jnp
from jax import lax
from jax.experimental import pallas as pl
from jax.experimental.pallas import tpu as pltpu

F32 = jnp.float32
BF16 = jnp.bfloat16

D_MODEL = 1024
N_MEM = 256
MEM_HEADS = 4
MEM_HEAD_DIM = D_MODEL // MEM_HEADS
D_LRU = 512
LRU_BLOCKS = 8
LRU_BLOCK_DIM = D_LRU // LRU_BLOCKS
CONV_WIDTH = 4
LRU_C = 8.0
MLA_HEADS = 4
QK_NOPE_DIM = 128
QK_ROPE_DIM = 64
V_HEAD_DIM = 128
Q_LORA_RANK = 384
KV_LORA_RANK = 256
D_MLA_OUT = MLA_HEADS * V_HEAD_DIM
ROPE_THETA = 10000.0
D_FF = 2816
RMS_EPS = 1e-6
NEG_INF = -1e30

LANES = 128
SUBLANES = 8
HEAD_SLAB = 2 * LANES
BF16_ROWS = 16
VT_ROWS = V_HEAD_DIM + BF16_ROWS
Z_LRU_X = 0
Z_LRU_GATE = Z_LRU_X + D_LRU
Z_CQ = Z_LRU_GATE + D_LRU
Z_CKV = Z_CQ + Q_LORA_RANK
Z_KPE = Z_CKV + KV_LORA_RANK
Z_WIDTH = Z_KPE + LANES
MXU_TILE = 256
LRU_GROUP = MXU_TILE

TS_IN = 512
IN_TILES_PER_STEP = 4
Q_TILES_PER_STEP = 4
SCORE_LOOKAHEAD = 2
TM_MIX = 1024
TM_MEM = 1024
ROW_SUB = 256
FF_CHUNKS = ((0, 1536), (1536, D_FF))
VMEM_LIMIT = 62 * 1024 * 1024


def _rms(x, g):
    ms = jnp.mean(x * x, axis=-1, keepdims=True)
    return x * lax.rsqrt(ms + RMS_EPS) * g


def _rms_plain(x):
    ms = jnp.mean(x * x, axis=-1, keepdims=True)
    return x * lax.rsqrt(ms + RMS_EPS)


def _gelu_tanh(x):
    k1 = -2.0 * math.sqrt(2.0 / math.pi) * math.log2(math.e)
    k2 = k1 * 0.044715
    return x / (1.0 + jnp.exp2(x * (k1 + k2 * (x * x))))


def _dot(a, b):
    return jnp.dot(a, b, preferred_element_type=F32)


def _dot_nt(a, b):
    return lax.dot_general(a, b, (((1,), (1,)), ((), ())), preferred_element_type=F32)


def _const_spec(shape):
    return pl.BlockSpec(shape, lambda *_: (0,) * len(shape))


def _shift_rows(x, prev, d):
    r = pltpu.roll(x, d, 0)
    row = lax.broadcasted_iota(jnp.int32, prev.shape, 0)
    head = jnp.where(row < d, pltpu.roll(prev, d, 0), r[0:SUBLANES])
    return jnp.concatenate([head, r[SUBLANES:]], axis=0)


def _linear_scan(a, u, h0):
    t, c = a.shape
    groups = t // SUBLANES
    a = a.reshape(groups, SUBLANES, c)
    u = u.reshape(groups, SUBLANES, c)
    row = lax.broadcasted_iota(jnp.int32, a.shape, 1)
    for s in (1, 2, 4):
        a_s = pltpu.roll(a, s, 1)
        u_s = pltpu.roll(u, s, 1)
        m = row >= s
        u = jnp.where(m, a * u_s + u, u)
        a = jnp.where(m, a * a_s, a)
    hs = []
    h = h0
    for g in range(groups):
        hg = a[g] * h + u[g]
        hs.append(hg)
        h = hg[SUBLANES - 1:SUBLANES]
    return jnp.concatenate(hs, axis=0)


def _rope(t, cos, sin):
    half = QK_ROPE_DIM // 2
    return t * cos + pltpu.roll(t, half, 1) * sin - pltpu.roll(t, LANES - half, 1) * sin


def _in_proj_kernel(x_ref, pos_ref, invf_ref, win_ref, convw_ref, convb_ref, wg_ref,
                    ba_ref, bx_ref, lam_ref, wuqt_ref, wkn_ref, wvt_ref,
                    qt_ref, k_ref, vt_ref, ylru_ref, xprev_ref, hprev_ref):
    ts = TS_IN
    tiles = range(x_ref.shape[0] // ts)
    rows = [slice(n * ts, (n + 1) * ts) for n in tiles]
    half = QK_ROPE_DIM // 2

    @pl.when(pl.program_id(1) == 0)
    def _():
        xprev_ref[...] = jnp.zeros_like(xprev_ref)
        hprev_ref[...] = jnp.zeros_like(hprev_ref)

    xn = [_rms_plain(x_ref[r, :]).astype(BF16) for r in rows]
    z_tiles = [{} for _ in tiles]

    def z(n, lo, hi):
        parts = []
        for j in range(lo // MXU_TILE, -(-hi // MXU_TILE)):
            if j not in z_tiles[n]:
                z_tiles[n][j] = _dot(xn[n], win_ref[:, j * MXU_TILE:(j + 1) * MXU_TILE])
            a, b = max(lo, j * MXU_TILE), min(hi, (j + 1) * MXU_TILE)
            parts.append(z_tiles[n][j][:, a - j * MXU_TILE:b - j * MXU_TILE])
        return parts[0] if len(parts) == 1 else jnp.concatenate(parts, axis=-1)

    xl = [z(n, Z_LRU_X, Z_LRU_GATE) for n in tiles]
    u = []
    prev = xprev_ref[...]
    for n in tiles:
        un = convb_ref[...] + convw_ref[CONV_WIDTH - 1:CONV_WIDTH, :] * xl[n]
        for d in range(1, CONV_WIDTH):
            k = CONV_WIDTH - 1 - d
            un = un + convw_ref[k:k + 1, :] * _shift_rows(xl[n], prev, d)
        u.append(un)
        prev = xl[n][ts - SUBLANES:ts]
    xprev_ref[...] = prev
    pre = [[_dot(u[n][:, j * LRU_GROUP:(j + 1) * LRU_GROUP].astype(BF16), wg_ref[j])
            for j in range(D_LRU // LRU_GROUP)] for n in tiles]

    z_gate = [z(n, Z_LRU_GATE, Z_CQ) for n in tiles]
    z_cq = [z(n, Z_CQ, Z_CKV) for n in tiles]
    z_ckv = [z(n, Z_CKV, Z_KPE) for n in tiles]
    z_kpe = [z(n, Z_KPE, Z_WIDTH) for n in tiles]
    cqn = [_rms_plain(v) for v in z_cq]
    ckvn = [_rms_plain(v) for v in z_ckv]
    q_t = [_dot(wuqt_ref[...], v.T.astype(BF16)) for v in cqn]
    v_t = [_dot(wvt_ref[...], v.T.astype(BF16)) for v in ckvn]
    kn = [_dot(v.astype(BF16), wkn_ref[...]) for v in ckvn]

    nlam = -lam_ref[...]
    softplus = jnp.maximum(nlam, 0.0) + jnp.log1p(jnp.exp(-jnp.abs(nlam)))
    h_last = hprev_ref[SUBLANES - 1:SUBLANES, :]
    for n in tiles:
        r = 1.0 / (1.0 + jnp.exp2(
            jnp.concatenate([p[:, :LRU_GROUP] for p in pre[n]], axis=-1) + ba_ref[...]))
        gi = 1.0 / (1.0 + jnp.exp2(
            jnp.concatenate([p[:, LRU_GROUP:] for p in pre[n]], axis=-1) + bx_ref[...]))
        a = jnp.exp2(r * ((-LRU_C * math.log2(math.e)) * softplus))
        one_m_a2 = 1.0 - a * a
        root = jnp.where(one_m_a2 > 0.0, one_m_a2 * lax.rsqrt(one_m_a2), 0.0)
        h = _linear_scan(a, root * (gi * u[n]), h_last)
        h_last = h[ts - 1:ts]
        if n == tiles[-1]:
            hprev_ref[...] = h[ts - SUBLANES:ts]
        ylru_ref[rows[n], :] = _rms_plain(h * _gelu_tanh(z_gate[n])).astype(BF16)

    for n in tiles:
        ang_t = invf_ref[...] * pos_ref[:, rows[n]].astype(F32)
        cos_t = jnp.cos(ang_t)
        sin_t = jnp.sin(ang_t)
        for hd in range(MLA_HEADS):
            r0 = hd * HEAD_SLAB
            pe = r0 + QK_NOPE_DIM
            x1 = q_t[n][pe:pe + half]
            x2 = q_t[n][pe + half:pe + 2 * half]
            qt_ref[n, r0:pe, :] = q_t[n][r0:pe].astype(BF16)
            qt_ref[n, pe:pe + half, :] = (x1 * cos_t - x2 * sin_t).astype(BF16)
            qt_ref[n, pe + half:pe + 2 * half, :] = (x2 * cos_t + x1 * sin_t).astype(BF16)
            qt_ref[n, pe + 2 * half:r0 + HEAD_SLAB, :] = jnp.zeros(
                (HEAD_SLAB - QK_NOPE_DIM - 2 * half, ts), BF16)
            v0 = hd * VT_ROWS
            vt_ref[n, v0:v0 + V_HEAD_DIM, :] = v_t[n][hd * V_HEAD_DIM:(hd + 1) * V_HEAD_DIM].astype(BF16)
            vt_ref[n, v0 + V_HEAD_DIM:v0 + VT_ROWS, :] = jnp.ones((BF16_ROWS, ts), BF16)
        pad = jnp.zeros((LANES - 2 * half, ts), F32)
        cos = jnp.concatenate([cos_t, cos_t, pad], axis=0).T
        sin = jnp.concatenate([sin_t, sin_t, pad], axis=0).T
        kpe = _rope(z_kpe[n], cos, sin).astype(BF16)
        for hd in range(MLA_HEADS):
            c0 = hd * HEAD_SLAB
            k_ref[rows[n], c0:c0 + LANES] = kn[n][:, hd * LANES:(hd + 1) * LANES].astype(BF16)
            k_ref[rows[n], c0 + LANES:c0 + HEAD_SLAB] = kpe


def _in_proj(x, pos_row, invf, win, convw, convb, wg, ba, bx, lam, wuqt, wkn, wvt):
    bsz, s, _ = x.shape
    ts = TS_IN
    step = TS_IN * IN_TILES_PER_STEP
    tok = lambda w: pl.BlockSpec((None, step, w), lambda b, i: (b, i, 0))
    in_specs = [
        tok(D_MODEL), pl.BlockSpec((None, 1, step), lambda b, i: (b, 0, i)),
        _const_spec((QK_ROPE_DIM // 2, 1)),
        _const_spec((D_MODEL, Z_WIDTH)), _const_spec((CONV_WIDTH, D_LRU)), _const_spec((1, D_LRU)),
        _const_spec((D_LRU // LRU_GROUP, LRU_GROUP, 2 * LRU_GROUP)),
        _const_spec((1, D_LRU)), _const_spec((1, D_LRU)), _const_spec((1, D_LRU)),
        _const_spec((MLA_HEADS * HEAD_SLAB, Q_LORA_RANK)),
        _const_spec((KV_LORA_RANK, MLA_HEADS * QK_NOPE_DIM)),
        _const_spec((D_MLA_OUT, KV_LORA_RANK)),
    ]
    tile_t = lambda w: pl.BlockSpec((None, IN_TILES_PER_STEP, w, ts), lambda b, i: (b, i, 0, 0))
    out_shape = (
        jax.ShapeDtypeStruct((bsz, s // ts, MLA_HEADS * HEAD_SLAB, ts), BF16),
        jax.ShapeDtypeStruct((bsz, s, MLA_HEADS * HEAD_SLAB), BF16),
        jax.ShapeDtypeStruct((bsz, s // ts, MLA_HEADS * VT_ROWS, ts), BF16),
        jax.ShapeDtypeStruct((bsz, s, D_LRU), BF16),
    )
    out_specs = (tile_t(MLA_HEADS * HEAD_SLAB), tok(MLA_HEADS * HEAD_SLAB),
                 tile_t(MLA_HEADS * VT_ROWS), tok(D_LRU))
    return pl.pallas_call(
        _in_proj_kernel,
        out_shape=out_shape,
        grid=(bsz, s // step),
        in_specs=in_specs,
        out_specs=out_specs,
        scratch_shapes=[pltpu.VMEM((SUBLANES, D_LRU), F32), pltpu.VMEM((SUBLANES, D_LRU), F32)],
        compiler_params=pltpu.CompilerParams(dimension_semantics=("parallel", "arbitrary"),
                                             vmem_limit_bytes=VMEM_LIMIT),
        name="in_proj",
    )(x, pos_row, invf, win, convw, convb, wg, ba, bx, lam, wuqt, wkn, wvt)


def _mla_attn_kernel(qt_ref, k_ref, vt_ref, o_ref):
    n_local, _, tq = qt_ref.shape
    nq = vt_ref.shape[0]
    i = pl.program_id(1)
    hq = tq // 2

    def scores(ql, j, hd, diagonal):
        slab = slice(hd * HEAD_SLAB, (hd + 1) * HEAD_SLAB)
        if not diagonal:
            st = _dot(k_ref[j * tq:(j + 1) * tq, slab], qt_ref[ql, slab, :])
            return [(0, 0, False, st[:, :hq]), (1, 0, False, st[:, hq:])]
        return [(0, 0, True, _dot(k_ref[j * tq:j * tq + hq, slab], qt_ref[ql, slab, 0:hq])),
                (1, 0, True, _dot(k_ref[j * tq:(j + 1) * tq, slab], qt_ref[ql, slab, hq:tq]))]

    def finish(ql, acc):
        outs = []
        for hd in range(MLA_HEADS):
            a = jnp.concatenate(acc[hd], axis=1)
            outs.append(a[:V_HEAD_DIM] * (1.0 / a[V_HEAD_DIM:V_HEAD_DIM + 1]))
        out_t = jnp.concatenate(outs, axis=0)
        ms = jnp.mean(out_t * out_t, axis=0, keepdims=True)
        o_ref[ql * tq:(ql + 1) * tq, :] = (out_t * lax.rsqrt(ms + RMS_EPS)).astype(BF16).T

    def attend(first_tile):
        units = [(ql, j, hd, j == first_tile + ql)
                 for ql in range(n_local) for j in range(first_tile + ql + 1)
                 for hd in range(MLA_HEADS)]
        ahead = [scores(*u) for u in units[:SCORE_LOOKAHEAD]]
        m = acc = None
        for n, (ql, j, hd, diagonal) in enumerate(units):
            if j == 0 and hd == 0:
                m = [[jnp.full((1, hq), NEG_INF, F32)] * 2 for _ in range(MLA_HEADS)]
                acc = [[None, None] for _ in range(MLA_HEADS)]
            pieces = ahead.pop(0)
            if n + SCORE_LOOKAHEAD < len(units):
                ahead.append(scores(*units[n + SCORE_LOOKAHEAD]))
            vrows = slice(hd * VT_ROWS, (hd + 1) * VT_ROWS)
            for half, k0, masked, st in pieces:
                nk = st.shape[0]
                if masked:
                    kid = lax.broadcasted_iota(jnp.int32, (nk, hq), 0) + k0
                    qid = lax.broadcasted_iota(jnp.int32, (nk, hq), 1) + half * hq
                    st = jnp.where(qid >= kid, st, NEG_INF)
                m_old = m[hd][half]
                m_new = jnp.maximum(m_old, jnp.max(st, axis=0, keepdims=True))
                p = jnp.exp2(st - m_new).astype(BF16)
                pv = _dot(vt_ref[j, vrows, k0:k0 + nk], p)
                old = acc[hd][half]
                acc[hd][half] = pv if old is None else jnp.exp2(m_old - m_new) * old + pv
                m[hd][half] = m_new
            if diagonal and hd == MLA_HEADS - 1:
                finish(ql, acc)

    for n in range(nq // n_local):
        @pl.when(i == n)
        def _(n=n):
            attend(n * n_local)


def _mla_attn(qt, k, vt):
    bsz, nq, _, tq = qt.shape
    s = k.shape[1]
    per = Q_TILES_PER_STEP
    return pl.pallas_call(
        _mla_attn_kernel,
        out_shape=jax.ShapeDtypeStruct((bsz, s, D_MLA_OUT), BF16),
        grid=(bsz, nq // per),
        in_specs=[
            pl.BlockSpec((None, per, MLA_HEADS * HEAD_SLAB, tq), lambda b, i: (b, i, 0, 0)),
            pl.BlockSpec((None, s, MLA_HEADS * HEAD_SLAB), lambda b, i: (b, 0, 0)),
            pl.BlockSpec((None, nq, MLA_HEADS * VT_ROWS, tq), lambda b, i: (b, 0, 0, 0)),
        ],
        out_specs=pl.BlockSpec((None, per * tq, D_MLA_OUT), lambda b, i: (b, i, 0)),
        compiler_params=pltpu.CompilerParams(dimension_semantics=("parallel", "arbitrary"),
                                             vmem_limit_bytes=VMEM_LIMIT),
        name="mla_attn",
    )(qt, k, vt)


def _mem_kv_kernel(mem_ref, gmem_ref, wmk_ref, wmv_ref, mk_ref, mv_ref):
    mn = _rms(mem_ref[...], gmem_ref[...]).astype(BF16)
    mk_ref[...] = _dot(mn, wmk_ref[...]).astype(BF16)
    mv_ref[...] = _dot(mn, wmv_ref[...]).astype(BF16)


def _mem_kv(mem, gmem, wmk, wmv):
    m, _ = mem.shape
    tm = TM_MEM
    tok = pl.BlockSpec((tm, D_MODEL), lambda i: (i, 0))
    sq = _const_spec((D_MODEL, D_MODEL))
    out = jax.ShapeDtypeStruct((m, D_MODEL), BF16)
    return pl.pallas_call(
        _mem_kv_kernel,
        out_shape=(out, out),
        grid=(m // tm,),
        in_specs=[tok, _const_spec((1, D_MODEL)), sq, sq],
        out_specs=(tok, tok),
        compiler_params=pltpu.CompilerParams(dimension_semantics=("parallel",),
                                             vmem_limit_bytes=VMEM_LIMIT),
        name="mem_kv",
    )(mem, gmem, wmk, wmv)


def _mix_ffn_kernel(ylru_ref, ymla_ref, x_ref, mk_ref, mv_ref, wout_ref, gpost_ref, gpre_ref,
                    wmq_ref, wmo_ref, gpm_ref, gpf_ref, wg_hbm, wu_hbm, wd_hbm, gqf_ref,
                    h_ref, o_scr, wg_ref, wu_ref, wd_ref, sem):
    first = (pl.program_id(0) == 0) & (pl.program_id(1) == 0)
    copies = [pltpu.make_async_copy(src, dst, sem.at[n]) for n, (src, dst) in
              enumerate(((wg_hbm, wg_ref), (wu_hbm, wu_ref), (wd_hbm, wd_ref)))]
    for load_weights in (True, False):
        @pl.when(first == load_weights)
        def _(load_weights=load_weights):
            _mix_ffn_body(copies if load_weights else (), ylru_ref, ymla_ref, x_ref, mk_ref,
                          mv_ref, wout_ref, gpost_ref, gpre_ref, wmq_ref, wmo_ref, gpm_ref,
                          gpf_ref, wg_ref, wu_ref, wd_ref, gqf_ref, h_ref, o_scr)


def _mix_ffn_body(copies, ylru_ref, ymla_ref, x_ref, mk_ref, mv_ref, wout_ref, gpost_ref, gpre_ref,
                  wmq_ref, wmo_ref, gpm_ref, gpf_ref, wg_ref, wu_ref, wd_ref, gqf_ref, h_ref, o_scr):
    for c in copies:
        c.start()
    subs = [slice(r, r + ROW_SUB) for r in range(0, x_ref.shape[0], ROW_SUB)]
    yo = [_dot(ylru_ref[r, :], wout_ref[0:D_LRU, :])
          + _dot(ymla_ref[r, :], wout_ref[D_LRU:, :]) for r in subs]
    h1 = [x_ref[r, :] + _rms(y, gpost_ref[...]) for r, y in zip(subs, yo)]

    xn = [_rms(h, gpre_ref[...]).astype(BF16) for h in h1]
    mq = [(_dot(v, wmq_ref[...]) * (1.0 / math.sqrt(MEM_HEAD_DIM))).astype(BF16) for v in xn]
    for r, q in zip(subs, mq):
        for hd in range(MEM_HEADS):
            sl = slice(hd * MEM_HEAD_DIM, (hd + 1) * MEM_HEAD_DIM)
            s = _dot_nt(q[:, sl], mk_ref[:, sl])
            m = jnp.max(s, axis=-1, keepdims=True)
            p = jnp.exp(s - m)
            l = jnp.sum(p, axis=-1, keepdims=True)
            o_scr[r, sl] = (_dot(p.astype(BF16), mv_ref[:, sl]) / l).astype(BF16)
    mo = [_dot(o_scr[r, :], wmo_ref[...]) for r in subs]
    for r, h, o in zip(subs, h1, mo):
        h_ref[r, :] = h + _rms(o, gpm_ref[...])

    for c in copies:
        c.wait()
    xn = [_rms(h_ref[r, :], gpf_ref[...]).astype(BF16) for r in subs]
    f = [None] * len(subs)
    for lo, hi in FF_CHUNKS:
        for n, v in enumerate(xn):
            act = (jax.nn.silu(_dot(v, wg_ref[:, lo:hi])) * _dot(v, wu_ref[:, lo:hi])).astype(BF16)
            part = _dot(act, wd_ref[lo:hi, :])
            f[n] = part if f[n] is None else f[n] + part
    for r, fr in zip(subs, f):
        h_ref[r, :] = h_ref[r, :] + _rms(fr, gqf_ref[...])


def _mix_ffn(ylru, ymla, x, mk, mv, wout, gpost, gpre, wmq, wmo, gpm, gpf, wg, wu, wd, gqf):
    bsz, s, _ = x.shape
    tm = TM_MIX
    single = pl.Buffered(1)
    tok = lambda w: pl.BlockSpec((None, tm, w), lambda b, i: (b, i, 0))
    memspec = pl.BlockSpec((None, N_MEM, D_MODEL), lambda b, i: (b, 0, 0))
    sq = pl.BlockSpec((D_MODEL, D_MODEL), lambda b, i: (0, 0), pipeline_mode=single)
    hbm = pl.BlockSpec(memory_space=pl.ANY)
    vec = _const_spec((1, D_MODEL))
    return pl.pallas_call(
        _mix_ffn_kernel,
        out_shape=jax.ShapeDtypeStruct((bsz, s, D_MODEL), F32),
        grid=(bsz, s // tm),
        in_specs=[tok(D_LRU), tok(D_MLA_OUT), tok(D_MODEL), memspec, memspec, sq, vec, vec, sq,
                  sq, vec, vec, hbm, hbm, hbm, vec],
        out_specs=tok(D_MODEL),
        scratch_shapes=[pltpu.VMEM((tm, D_MODEL), BF16), pltpu.VMEM((D_MODEL, D_FF), BF16),
                        pltpu.VMEM((D_MODEL, D_FF), BF16), pltpu.VMEM((D_FF, D_MODEL), BF16),
                        pltpu.SemaphoreType.DMA((3,))],
        compiler_params=pltpu.CompilerParams(dimension_semantics=("arbitrary", "arbitrary"),
                                             vmem_limit_bytes=VMEM_LIMIT),
        name="mix_ffn",
    )(ylru, ymla, x, mk, mv, wout, gpost, gpre, wmq, wmo, gpm, gpf, wg, wu, wd, gqf)


def _block_diag(w):
    eye = jnp.eye(LRU_BLOCKS, dtype=w.dtype)
    return jnp.einsum('hij,hg->higj', w, eye).reshape(D_LRU, D_LRU)


def _gate_weights(wa, wx):
    da, dx = _block_diag(wa) * -math.log2(math.e), _block_diag(wx) * -math.log2(math.e)
    groups = []
    for j in range(D_LRU // LRU_GROUP):
        sl = slice(j * LRU_GROUP, (j + 1) * LRU_GROUP)
        groups.append(jnp.concatenate([da[sl, sl], dx[sl, sl]], axis=1))
    return jnp.stack(groups).astype(BF16)


def _pad_in_proj(w_in, gain):
    w = gain[:, None] * w_in
    pad = jnp.zeros((D_MODEL, Z_WIDTH - w.shape[1]), w.dtype)
    return jnp.concatenate([w, pad], axis=1).astype(BF16)


def _q_weights_t(w_uq, gain):
    dh = QK_NOPE_DIM + QK_ROPE_DIM
    scale = math.log2(math.e) / math.sqrt(dh)
    w = (scale * gain[:, None] * w_uq).reshape(Q_LORA_RANK, MLA_HEADS, dh)
    pad = jnp.zeros((Q_LORA_RANK, MLA_HEADS, HEAD_SLAB - dh), w.dtype)
    w = jnp.concatenate([w, pad], axis=-1).reshape(Q_LORA_RANK, MLA_HEADS * HEAD_SLAB)
    return w.T.astype(BF16)


def _kv_weights(w_ukv, gain):
    w = (gain[:, None] * w_ukv).reshape(KV_LORA_RANK, MLA_HEADS, QK_NOPE_DIM + V_HEAD_DIM)
    kn = w[:, :, :QK_NOPE_DIM].reshape(KV_LORA_RANK, MLA_HEADS * QK_NOPE_DIM)
    vv = w[:, :, QK_NOPE_DIM:].reshape(KV_LORA_RANK, MLA_HEADS * V_HEAD_DIM)
    return kn.astype(BF16), vv.T.astype(BF16)


def _rope_freq_column():
    inv_freq = ROPE_THETA ** (-jnp.arange(0, QK_ROPE_DIM, 2, dtype=F32) / QK_ROPE_DIM)
    return inv_freq.reshape(QK_ROPE_DIM // 2, 1)


def kernel(x, mem, positions, g_pre_mix, w_in, conv_w, conv_b, lru_wa, lru_ba, lru_wx, lru_bx, lru_lambda, g_q_lat, w_uq, g_kv_lat, w_ukv, g_lru_out, g_mla_out, w_out, g_post_mix, g_pre_mem, g_mem_kv, w_mq, w_mk, w_mv, w_mo, g_post_mem, g_pre_ffn, w_gate, w_up, w_down, g_post_ffn):
    bsz, s, d = x.shape
    depth = w_in.shape[0]
    assert d == D_MODEL and s % (TS_IN * max(IN_TILES_PER_STEP, Q_TILES_PER_STEP)) == 0
    assert s % TM_MIX == 0 and (bsz * N_MEM) % TM_MEM == 0 and mem.shape == (bsz, N_MEM, D_MODEL)
    pos_row = positions.reshape(bsz, 1, s)
    invf = _rope_freq_column()
    row = lambda a: a.reshape(1, -1)

    h = x
    for l in range(depth):
        wkn, wvt = _kv_weights(w_ukv[l], g_kv_lat[l])
        neg_log2e = -math.log2(math.e)
        qt, k, vt, ylru = _in_proj(
            h, pos_row, invf, _pad_in_proj(w_in[l], g_pre_mix[l]), conv_w[l], row(conv_b[l]),
            _gate_weights(lru_wa[l], lru_wx[l]), row(neg_log2e * lru_ba[l]),
            row(neg_log2e * lru_bx[l]), row(lru_lambda[l]),
            _q_weights_t(w_uq[l], g_q_lat[l]), wkn, wvt)
        ymla = _mla_attn(qt, k, vt)
        wout = (jnp.concatenate([g_lru_out[l], g_mla_out[l]])[:, None] * w_out[l]).astype(BF16)
        mk, mv = _mem_kv(mem.reshape(bsz * N_MEM, d), row(g_mem_kv[l]), w_mk[l].astype(BF16),
                         w_mv[l].astype(BF16))
        h = _mix_ffn(ylru, ymla, h, mk.reshape(bsz, N_MEM, d), mv.reshape(bsz, N_MEM, d), wout,
                     row(g_post_mix[l]), row(g_pre_mem[l]), w_mq[l].astype(BF16),
                     w_mo[l].astype(BF16), row(g_post_mem[l]), row(g_pre_ffn[l]),
                     w_gate[l].astype(BF16), w_up[l].astype(BF16), w_down[l].astype(BF16),
                     row(g_post_ffn[l]))
    return h
```

```python
import math

import jax
import jax.numpy as jnp
from jax import lax
from jax.experimental import pallas as pl
from jax.experimental.pallas import tpu as pltpu

F32 = jnp.float32
BF16 = jnp.bfloat16

D_MODEL = 1024
N_MEM = 256
MEM_HEADS = 4
MEM_HEAD_DIM = D_MODEL // MEM_HEADS
D_LRU = 512
LRU_BLOCKS = 8
LRU_BLOCK_DIM = D_LRU // LRU_BLOCKS
CONV_WIDTH = 4
LRU_C = 8.0
MLA_HEADS = 4
QK_NOPE_DIM = 128
QK_ROPE_DIM = 64
V_HEAD_DIM = 128
Q_LORA_RANK = 384
KV_LORA_RANK = 256
D_MLA_OUT = MLA_HEADS * V_HEAD_DIM
ROPE_THETA = 10000.0
D_FF = 2816
RMS_EPS = 1e-6
NEG_INF = -1e30

LANES = 128
SUBLANES = 8
HEAD_SLAB = 2 * LANES
BF16_ROWS = 16
VT_ROWS = V_HEAD_DIM + BF16_ROWS
Z_LRU_X = 0
Z_LRU_GATE = Z_LRU_X + D_LRU
Z_CQ = Z_LRU_GATE + D_LRU
Z_CKV = Z_CQ + Q_LORA_RANK
Z_KPE = Z_CKV + KV_LORA_RANK
Z_WIDTH = Z_KPE + LANES
MXU_TILE = 256
LRU_GROUP = MXU_TILE

TS_IN = 512
IN_TILES_PER_STEP = 4
Q_TILES_PER_STEP = 4
SCORE_LOOKAHEAD = 2
TM_MIX = 1024
TM_MEM = 1024
ROW_SUB = 256
FF_CHUNKS = ((0, 1536), (1536, D_FF))
VMEM_LIMIT = 60 * 1024 * 1024


def _rms(x, g):
    ms = jnp.mean(x * x, axis=-1, keepdims=True)
    return x * lax.rsqrt(ms + RMS_EPS) * g


def _rms_plain(x):
    ms = jnp.mean(x * x, axis=-1, keepdims=True)
    return x * lax.rsqrt(ms + RMS_EPS)


def _gelu_tanh(x):
    k1 = -2.0 * math.sqrt(2.0 / math.pi) * math.log2(math.e)
    k2 = k1 * 0.044715
    return x / (1.0 + jnp.exp2(x * (k1 + k2 * (x * x))))


def _dot(a, b):
    return jnp.dot(a, b, preferred_element_type=F32)


def _dot_nt(a, b):
    return lax.dot_general(a, b, (((1,), (1,)), ((), ())), preferred_element_type=F32)


def _const_spec(shape):
    return pl.BlockSpec(shape, lambda *_: (0,) * len(shape))


def _shift_rows(x, prev, d):
    r = pltpu.roll(x, d, 0)
    row = lax.broadcasted_iota(jnp.int32, prev.shape, 0)
    head = jnp.where(row < d, pltpu.roll(prev, d, 0), r[0:SUBLANES])
    return jnp.concatenate([head, r[SUBLANES:]], axis=0)


def _linear_scan(a, u, h0):
    t, c = a.shape
    groups = t // SUBLANES
    a = a.reshape(groups, SUBLANES, c)
    u = u.reshape(groups, SUBLANES, c)
    row = lax.broadcasted_iota(jnp.int32, a.shape, 1)
    for s in (1, 2, 4):
        a_s = pltpu.roll(a, s, 1)
        u_s = pltpu.roll(u, s, 1)
        m = row >= s
        u = jnp.where(m, a * u_s + u, u)
        a = jnp.where(m, a * a_s, a)
    hs = []
    h = h0
    for g in range(groups):
        hg = a[g] * h + u[g]
        hs.append(hg)
        h = hg[SUBLANES - 1:SUBLANES]
    return jnp.concatenate(hs, axis=0)


def _rope(t, cos, sin):
    half = QK_ROPE_DIM // 2
    return t * cos + pltpu.roll(t, half, 1) * sin - pltpu.roll(t, LANES - half, 1) * sin


def _in_proj_kernel(x_ref, pos_ref, invf_ref, win_ref, convw_ref, convb_ref, wg_ref,
                    ba_ref, bx_ref, lam_ref, wuqt_ref, wkn_ref, wvt_ref,
                    qt_ref, k_ref, vt_ref, ylru_ref, xprev_ref, hprev_ref):
    ts = TS_IN
    tiles = range(x_ref.shape[0] // ts)
    rows = [slice(n * ts, (n + 1) * ts) for n in tiles]
    half = QK_ROPE_DIM // 2

    @pl.when(pl.program_id(1) == 0)
    def _():
        xprev_ref[...] = jnp.zeros_like(xprev_ref)
        hprev_ref[...] = jnp.zeros_like(hprev_ref)

    xn = [_rms_plain(x_ref[r, :]).astype(BF16) for r in rows]
    z_tiles = [{} for _ in tiles]

    def z(n, lo, hi):
        parts = []
        for j in range(lo // MXU_TILE, -(-hi // MXU_TILE)):
            if j not in z_tiles[n]:
                z_tiles[n][j] = _dot(xn[n], win_ref[:, j * MXU_TILE:(j + 1) * MXU_TILE])
            a, b = max(lo, j * MXU_TILE), min(hi, (j + 1) * MXU_TILE)
            parts.append(z_tiles[n][j][:, a - j * MXU_TILE:b - j * MXU_TILE])
        return parts[0] if len(parts) == 1 else jnp.concatenate(parts, axis=-1)

    xl = [z(n, Z_LRU_X, Z_LRU_GATE) for n in tiles]
    u = []
    prev = xprev_ref[...]
    for n in tiles:
        un = convb_ref[...] + convw_ref[CONV_WIDTH - 1:CONV_WIDTH, :] * xl[n]
        for d in range(1, CONV_WIDTH):
            k = CONV_WIDTH - 1 - d
            un = un + convw_ref[k:k + 1, :] * _shift_rows(xl[n], prev, d)
        u.append(un)
        prev = xl[n][ts - SUBLANES:ts]
    xprev_ref[...] = prev
    pre = [[_dot(u[n][:, j * LRU_GROUP:(j + 1) * LRU_GROUP].astype(BF16), wg_ref[j])
            for j in range(D_LRU // LRU_GROUP)] for n in tiles]

    z_gate = [z(n, Z_LRU_GATE, Z_CQ) for n in tiles]
    z_cq = [z(n, Z_CQ, Z_CKV) for n in tiles]
    z_ckv = [z(n, Z_CKV, Z_KPE) for n in tiles]
    z_kpe = [z(n, Z_KPE, Z_WIDTH) for n in tiles]
    cqn = [_rms_plain(v) for v in z_cq]
    ckvn = [_rms_plain(v) for v in z_ckv]
    q_t = [_dot(wuqt_ref[...], v.T.astype(BF16)) for v in cqn]
    v_t = [_dot(wvt_ref[...], v.T.astype(BF16)) for v in ckvn]
    kn = [_dot(v.astype(BF16), wkn_ref[...]) for v in ckvn]

    nlam = -lam_ref[...]
    softplus = jnp.maximum(nlam, 0.0) + jnp.log1p(jnp.exp(-jnp.abs(nlam)))
    h_last = hprev_ref[SUBLANES - 1:SUBLANES, :]
    for n in tiles:
        r = 1.0 / (1.0 + jnp.exp2(
            jnp.concatenate([p[:, :LRU_GROUP] for p in pre[n]], axis=-1) + ba_ref[...]))
        gi = 1.0 / (1.0 + jnp.exp2(
            jnp.concatenate([p[:, LRU_GROUP:] for p in pre[n]], axis=-1) + bx_ref[...]))
        a = jnp.exp2(r * ((-LRU_C * math.log2(math.e)) * softplus))
        one_m_a2 = 1.0 - a * a
        root = jnp.where(one_m_a2 > 0.0, one_m_a2 * lax.rsqrt(one_m_a2), 0.0)
        h = _linear_scan(a, root * (gi * u[n]), h_last)
        h_last = h[ts - 1:ts]
        if n == tiles[-1]:
            hprev_ref[...] = h[ts - SUBLANES:ts]
        ylru_ref[rows[n], :] = _rms_plain(h * _gelu_tanh(z_gate[n])).astype(BF16)

    for n in tiles:
        ang_t = invf_ref[...] * pos_ref[:, rows[n]].astype(F32)
        cos_t = jnp.cos(ang_t)
        sin_t = jnp.sin(ang_t)
        for hd in range(MLA_HEADS):
            r0 = hd * HEAD_SLAB
            pe = r0 + QK_NOPE_DIM
            x1 = q_t[n][pe:pe + half]
            x2 = q_t[n][pe + half:pe + 2 * half]
            qt_ref[n, r0:pe, :] = q_t[n][r0:pe].astype(BF16)
            qt_ref[n, pe:pe + half, :] = (x1 * cos_t - x2 * sin_t).astype(BF16)
            qt_ref[n, pe + half:pe + 2 * half, :] = (x2 * cos_t + x1 * sin_t).astype(BF16)
            qt_ref[n, pe + 2 * half:r0 + HEAD_SLAB, :] = jnp.zeros(
                (HEAD_SLAB - QK_NOPE_DIM - 2 * half, ts), BF16)
            v0 = hd * VT_ROWS
            vt_ref[n, v0:v0 + V_HEAD_DIM, :] = v_t[n][hd * V_HEAD_DIM:(hd + 1) * V_HEAD_DIM].astype(BF16)
            vt_ref[n, v0 + V_HEAD_DIM:v0 + VT_ROWS, :] = jnp.ones((BF16_ROWS, ts), BF16)
        pad = jnp.zeros((LANES - 2 * half, ts), F32)
        cos = jnp.concatenate([cos_t, cos_t, pad], axis=0).T
        sin = jnp.concatenate([sin_t, sin_t, pad], axis=0).T
        kpe = _rope(z_kpe[n], cos, sin).astype(BF16)
        for hd in range(MLA_HEADS):
            c0 = hd * HEAD_SLAB
            k_ref[rows[n], c0:c0 + LANES] = kn[n][:, hd * LANES:(hd + 1) * LANES].astype(BF16)
            k_ref[rows[n], c0 + LANES:c0 + HEAD_SLAB] = kpe


def _in_proj(x, pos_row, invf, win, convw, convb, wg, ba, bx, lam, wuqt, wkn, wvt):
    bsz, s, _ = x.shape
    ts = TS_IN
    step = TS_IN * IN_TILES_PER_STEP
    tok = lambda w: pl.BlockSpec((None, step, w), lambda b, i: (b, i, 0))
    in_specs = [
        tok(D_MODEL), pl.BlockSpec((None, 1, step), lambda b, i: (b, 0, i)),
        _const_spec((QK_ROPE_DIM // 2, 1)),
        _const_spec((D_MODEL, Z_WIDTH)), _const_spec((CONV_WIDTH, D_LRU)), _const_spec((1, D_LRU)),
        _const_spec((D_LRU // LRU_GROUP, LRU_GROUP, 2 * LRU_GROUP)),
        _const_spec((1, D_LRU)), _const_spec((1, D_LRU)), _const_spec((1, D_LRU)),
        _const_spec((MLA_HEADS * HEAD_SLAB, Q_LORA_RANK)),
        _const_spec((KV_LORA_RANK, MLA_HEADS * QK_NOPE_DIM)),
        _const_spec((D_MLA_OUT, KV_LORA_RANK)),
    ]
    tile_t = lambda w: pl.BlockSpec((None, IN_TILES_PER_STEP, w, ts), lambda b, i: (b, i, 0, 0))
    out_shape = (
        jax.ShapeDtypeStruct((bsz, s // ts, MLA_HEADS * HEAD_SLAB, ts), BF16),
        jax.ShapeDtypeStruct((bsz, s, MLA_HEADS * HEAD_SLAB), BF16),
        jax.ShapeDtypeStruct((bsz, s // ts, MLA_HEADS * VT_ROWS, ts), BF16),
        jax.ShapeDtypeStruct((bsz, s, D_LRU), BF16),
    )
    out_specs = (tile_t(MLA_HEADS * HEAD_SLAB), tok(MLA_HEADS * HEAD_SLAB),
                 tile_t(MLA_HEADS * VT_ROWS), tok(D_LRU))
    return pl.pallas_call(
        _in_proj_kernel,
        out_shape=out_shape,
        grid=(bsz, s // step),
        in_specs=in_specs,
        out_specs=out_specs,
        scratch_shapes=[pltpu.VMEM((SUBLANES, D_LRU), F32), pltpu.VMEM((SUBLANES, D_LRU), F32)],
        compiler_params=pltpu.CompilerParams(dimension_semantics=("parallel", "arbitrary"),
                                             vmem_limit_bytes=VMEM_LIMIT),
        name="in_proj",
    )(x, pos_row, invf, win, convw, convb, wg, ba, bx, lam, wuqt, wkn, wvt)


def _mla_attn_kernel(qt_ref, k_ref, vt_ref, o_ref):
    n_local, _, tq = qt_ref.shape
    nq = vt_ref.shape[0]
    i = pl.program_id(1)
    hq = tq // 2

    def scores(ql, j, hd, diagonal):
        slab = slice(hd * HEAD_SLAB, (hd + 1) * HEAD_SLAB)
        if not diagonal:
            st = _dot(k_ref[j * tq:(j + 1) * tq, slab], qt_ref[ql, slab, :])
            return [(0, 0, False, st[:, :hq]), (1, 0, False, st[:, hq:])]
        return [(0, 0, True, _dot(k_ref[j * tq:j * tq + hq, slab], qt_ref[ql, slab, 0:hq])),
                (1, 0, True, _dot(k_ref[j * tq:(j + 1) * tq, slab], qt_ref[ql, slab, hq:tq]))]

    def finish(ql, acc):
        outs = []
        for hd in range(MLA_HEADS):
            a = jnp.concatenate(acc[hd], axis=1)
            outs.append(a[:V_HEAD_DIM] * (1.0 / a[V_HEAD_DIM:V_HEAD_DIM + 1]))
        out_t = jnp.concatenate(outs, axis=0)
        ms = jnp.mean(out_t * out_t, axis=0, keepdims=True)
        o_ref[ql * tq:(ql + 1) * tq, :] = (out_t * lax.rsqrt(ms + RMS_EPS)).astype(BF16).T

    def attend(first_tile):
        units = [(ql, j, hd, j == first_tile + ql)
                 for ql in range(n_local) for j in range(first_tile + ql + 1)
                 for hd in range(MLA_HEADS)]
        ahead = [scores(*u) for u in units[:SCORE_LOOKAHEAD]]
        m = acc = None
        for n, (ql, j, hd, diagonal) in enumerate(units):
            if j == 0 and hd == 0:
                m = [[jnp.full((1, hq), NEG_INF, F32)] * 2 for _ in range(MLA_HEADS)]
                acc = [[None, None] for _ in range(MLA_HEADS)]
            pieces = ahead.pop(0)
            if n + SCORE_LOOKAHEAD < len(units):
                ahead.append(scores(*units[n + SCORE_LOOKAHEAD]))
            vrows = slice(hd * VT_ROWS, (hd + 1) * VT_ROWS)
            for half, k0, masked, st in pieces:
                nk = st.shape[0]
                if masked:
                    kid = lax.broadcasted_iota(jnp.int32, (nk, hq), 0) + k0
                    qid = lax.broadcasted_iota(jnp.int32, (nk, hq), 1) + half * hq
                    st = jnp.where(qid >= kid, st, NEG_INF)
                m_old = m[hd][half]
                m_new = jnp.maximum(m_old, jnp.max(st, axis=0, keepdims=True))
                p = jnp.exp2(st - m_new).astype(BF16)
                pv = _dot(vt_ref[j, vrows, k0:k0 + nk], p)
                old = acc[hd][half]
                acc[hd][half] = pv if old is None else jnp.exp2(m_old - m_new) * old + pv
                m[hd][half] = m_new
            if diagonal and hd == MLA_HEADS - 1:
                finish(ql, acc)

    for n in range(nq // n_local):
        @pl.when(i == n)
        def _(n=n):
            attend(n * n_local)


def _mla_attn(qt, k, vt):
    bsz, nq, _, tq = qt.shape
    s = k.shape[1]
    per = Q_TILES_PER_STEP
    return pl.pallas_call(
        _mla_attn_kernel,
        out_shape=jax.ShapeDtypeStruct((bsz, s, D_MLA_OUT), BF16),
        grid=(bsz, nq // per),
        in_specs=[
            pl.BlockSpec((None, per, MLA_HEADS * HEAD_SLAB, tq), lambda b, i: (b, i, 0, 0)),
            pl.BlockSpec((None, s, MLA_HEADS * HEAD_SLAB), lambda b, i: (b, 0, 0)),
            pl.BlockSpec((None, nq, MLA_HEADS * VT_ROWS, tq), lambda b, i: (b, 0, 0, 0)),
        ],
        out_specs=pl.BlockSpec((None, per * tq, D_MLA_OUT), lambda b, i: (b, i, 0)),
        compiler_params=pltpu.CompilerParams(dimension_semantics=("parallel", "arbitrary"),
                                             vmem_limit_bytes=VMEM_LIMIT),
        name="mla_attn",
    )(qt, k, vt)


def _mem_kv_kernel(mem_ref, gmem_ref, wmk_ref, wmv_ref, mk_ref, mv_ref):
    mn = _rms(mem_ref[...], gmem_ref[...]).astype(BF16)
    mk_ref[...] = _dot(mn, wmk_ref[...]).astype(BF16)
    mv_ref[...] = _dot(mn, wmv_ref[...]).astype(BF16)


def _mem_kv(mem, gmem, wmk, wmv):
    m, _ = mem.shape
    tm = TM_MEM
    tok = pl.BlockSpec((tm, D_MODEL), lambda i: (i, 0))
    sq = _const_spec((D_MODEL, D_MODEL))
    out = jax.ShapeDtypeStruct((m, D_MODEL), BF16)
    return pl.pallas_call(
        _mem_kv_kernel,
        out_shape=(out, out),
        grid=(m // tm,),
        in_specs=[tok, _const_spec((1, D_MODEL)), sq, sq],
        out_specs=(tok, tok),
        compiler_params=pltpu.CompilerParams(dimension_semantics=("parallel",),
                                             vmem_limit_bytes=VMEM_LIMIT),
        name="mem_kv",
    )(mem, gmem, wmk, wmv)


def _mix_ffn_kernel(ylru_ref, ymla_ref, x_ref, mk_ref, mv_ref, wout_ref, gpost_ref, gpre_ref,
                    wmq_ref, wmo_ref, gpm_ref, gpf_ref, wg_hbm, wu_hbm, wd_hbm, gqf_ref,
                    h_ref, o_scr, wg_ref, wu_ref, wd_ref, sem):
    first = (pl.program_id(0) == 0) & (pl.program_id(1) == 0)
    copies = [pltpu.make_async_copy(src, dst, sem.at[n]) for n, (src, dst) in
              enumerate(((wg_hbm, wg_ref), (wu_hbm, wu_ref), (wd_hbm, wd_ref)))]

    @pl.when(first)
    def _():
        for c in copies:
            c.start()

    subs = [slice(r, r + ROW_SUB) for r in range(0, x_ref.shape[0], ROW_SUB)]
    yo = [_dot(ylru_ref[r, :], wout_ref[0:D_LRU, :])
          + _dot(ymla_ref[r, :], wout_ref[D_LRU:, :]) for r in subs]
    h1 = [x_ref[r, :] + _rms(y, gpost_ref[...]) for r, y in zip(subs, yo)]

    xn = [_rms(h, gpre_ref[...]).astype(BF16) for h in h1]
    mq = [(_dot(v, wmq_ref[...]) * (1.0 / math.sqrt(MEM_HEAD_DIM))).astype(BF16) for v in xn]
    for r, q in zip(subs, mq):
        for hd in range(MEM_HEADS):
            sl = slice(hd * MEM_HEAD_DIM, (hd + 1) * MEM_HEAD_DIM)
            s = _dot_nt(q[:, sl], mk_ref[:, sl])
            m = jnp.max(s, axis=-1, keepdims=True)
            p = jnp.exp(s - m)
            l = jnp.sum(p, axis=-1, keepdims=True)
            o_scr[r, sl] = (_dot(p.astype(BF16), mv_ref[:, sl]) / l).astype(BF16)
    mo = [_dot(o_scr[r, :], wmo_ref[...]) for r in subs]
    for r, h, o in zip(subs, h1, mo):
        h_ref[r, :] = h + _rms(o, gpm_ref[...])

    @pl.when(first)
    def _():
        for c in copies:
            c.wait()

    xn = [_rms(h_ref[r, :], gpf_ref[...]).astype(BF16) for r in subs]
    f = [None] * len(subs)
    for lo, hi in FF_CHUNKS:
        for n, v in enumerate(xn):
            act = (jax.nn.silu(_dot(v, wg_ref[:, lo:hi])) * _dot(v, wu_ref[:, lo:hi])).astype(BF16)
            part = _dot(act, wd_ref[lo:hi, :])
            f[n] = part if f[n] is None else f[n] + part
    for r, fr in zip(subs, f):
        h_ref[r, :] = h_ref[r, :] + _rms(fr, gqf_ref[...])


def _mix_ffn(ylru, ymla, x, mk, mv, wout, gpost, gpre, wmq, wmo, gpm, gpf, wg, wu, wd, gqf):
    bsz, s, _ = x.shape
    tm = TM_MIX
    single = pl.Buffered(1)
    tok = lambda w: pl.BlockSpec((None, tm, w), lambda b, i: (b, i, 0))
    memspec = pl.BlockSpec((None, N_MEM, D_MODEL), lambda b, i: (b, 0, 0))
    sq = pl.BlockSpec((D_MODEL, D_MODEL), lambda b, i: (0, 0), pipeline_mode=single)
    hbm = pl.BlockSpec(memory_space=pl.ANY)
    vec = _const_spec((1, D_MODEL))
    return pl.pallas_call(
        _mix_ffn_kernel,
        out_shape=jax.ShapeDtypeStruct((bsz, s, D_MODEL), F32),
        grid=(bsz, s // tm),
        in_specs=[tok(D_LRU), tok(D_MLA_OUT), tok(D_MODEL), memspec, memspec, sq, vec, vec, sq,
                  sq, vec, vec, hbm, hbm, hbm, vec],
        out_specs=tok(D_MODEL),
        scratch_shapes=[pltpu.VMEM((tm, D_MODEL), BF16), pltpu.VMEM((D_MODEL, D_FF), BF16),
                        pltpu.VMEM((D_MODEL, D_FF), BF16), pltpu.VMEM((D_FF, D_MODEL), BF16),
                        pltpu.SemaphoreType.DMA((3,))],
        compiler_params=pltpu.CompilerParams(dimension_semantics=("arbitrary", "arbitrary"),
                                             vmem_limit_bytes=VMEM_LIMIT),
        name="mix_ffn",
    )(ylru, ymla, x, mk, mv, wout, gpost, gpre, wmq, wmo, gpm, gpf, wg, wu, wd, gqf)


def _block_diag(w):
    eye = jnp.eye(LRU_BLOCKS, dtype=w.dtype)
    return jnp.einsum('hij,hg->higj', w, eye).reshape(D_LRU, D_LRU)


def _gate_weights(wa, wx):
    da, dx = _block_diag(wa) * -math.log2(math.e), _block_diag(wx) * -math.log2(math.e)
    groups = []
    for j in range(D_LRU // LRU_GROUP):
        sl = slice(j * LRU_GROUP, (j + 1) * LRU_GROUP)
        groups.append(jnp.concatenate([da[sl, sl], dx[sl, sl]], axis=1))
    return jnp.stack(groups).astype(BF16)


def _pad_in_proj(w_in, gain):
    w = gain[:, None] * w_in
    pad = jnp.zeros((D_MODEL, Z_WIDTH - w.shape[1]), w.dtype)
    return jnp.concatenate([w, pad], axis=1).astype(BF16)


def _q_weights_t(w_uq, gain):
    dh = QK_NOPE_DIM + QK_ROPE_DIM
    scale = math.log2(math.e) / math.sqrt(dh)
    w = (scale * gain[:, None] * w_uq).reshape(Q_LORA_RANK, MLA_HEADS, dh)
    pad = jnp.zeros((Q_LORA_RANK, MLA_HEADS, HEAD_SLAB - dh), w.dtype)
    w = jnp.concatenate([w, pad], axis=-1).reshape(Q_LORA_RANK, MLA_HEADS * HEAD_SLAB)
    return w.T.astype(BF16)


def _kv_weights(w_ukv, gain):
    w = (gain[:, None] * w_ukv).reshape(KV_LORA_RANK, MLA_HEADS, QK_NOPE_DIM + V_HEAD_DIM)
    kn = w[:, :, :QK_NOPE_DIM].reshape(KV_LORA_RANK, MLA_HEADS * QK_NOPE_DIM)
    vv = w[:, :, QK_NOPE_DIM:].reshape(KV_LORA_RANK, MLA_HEADS * V_HEAD_DIM)
    return kn.astype(BF16), vv.T.astype(BF16)


def _rope_freq_column():
    inv_freq = ROPE_THETA ** (-jnp.arange(0, QK_ROPE_DIM, 2, dtype=F32) / QK_ROPE_DIM)
    return inv_freq.reshape(QK_ROPE_DIM // 2, 1)


def kernel(x, mem, positions, g_pre_mix, w_in, conv_w, conv_b, lru_wa, lru_ba, lru_wx, lru_bx, lru_lambda, g_q_lat, w_uq, g_kv_lat, w_ukv, g_lru_out, g_mla_out, w_out, g_post_mix, g_pre_mem, g_mem_kv, w_mq, w_mk, w_mv, w_mo, g_post_mem, g_pre_ffn, w_gate, w_up, w_down, g_post_ffn):
    bsz, s, d = x.shape
    depth = w_in.shape[0]
    assert d == D_MODEL and s % (TS_IN * max(IN_TILES_PER_STEP, Q_TILES_PER_STEP)) == 0
    assert s % TM_MIX == 0 and (bsz * N_MEM) % TM_MEM == 0 and mem.shape == (bsz, N_MEM, D_MODEL)
    pos_row = positions.reshape(bsz, 1, s)
    invf = _rope_freq_column()
    row = lambda a: a.reshape(1, -1)

    h = x
    for l in range(depth):
        wkn, wvt = _kv_weights(w_ukv[l], g_kv_lat[l])
        neg_log2e = -math.log2(math.e)
        qt, k, vt, ylru = _in_proj(
            h, pos_row, invf, _pad_in_proj(w_in[l], g_pre_mix[l]), conv_w[l], row(conv_b[l]),
            _gate_weights(lru_wa[l], lru_wx[l]), row(neg_log2e * lru_ba[l]),
            row(neg_log2e * lru_bx[l]), row(lru_lambda[l]),
            _q_weights_t(w_uq[l], g_q_lat[l]), wkn, wvt)
        ymla = _mla_attn(qt, k, vt)
        wout = (jnp.concatenate([g_lru_out[l], g_mla_out[l]])[:, None] * w_out[l]).astype(BF16)
        mk, mv = _mem_kv(mem.reshape(bsz * N_MEM, d), row(g_mem_kv[l]), w_mk[l].astype(BF16),
                         w_mv[l].astype(BF16))
        h = _mix_ffn(ylru, ymla, h, mk.reshape(bsz, N_MEM, d), mv.reshape(bsz, N_MEM, d), wout,
                     row(g_post_mix[l]), row(g_pre_mem[l]), w_mq[l].astype(BF16),
                     w_mo[l].astype(BF16), row(g_post_mem[l]), row(g_pre_ffn[l]),
                     w_gate[l].astype(BF16), w_up[l].astype(BF16), w_down[l].astype(BF16),
                     row(g_post_ffn[l]))
    return h
```

```python
import math

import jax
import jax.numpy as jnp
from jax import lax
from jax.experimental import pallas as pl
from jax.experimental.pallas import tpu as pltpu

F32 = jnp.float32
BF16 = jnp.bfloat16

D_MODEL = 1024
N_MEM = 256
MEM_HEADS = 4
MEM_HEAD_DIM = D_MODEL // MEM_HEADS
D_LRU = 512
LRU_BLOCKS = 8
LRU_BLOCK_DIM = D_LRU // LRU_BLOCKS
CONV_WIDTH = 4
LRU_C = 8.0
MLA_HEADS = 4
QK_NOPE_DIM = 128
QK_ROPE_DIM = 64
V_HEAD_DIM = 128
Q_LORA_RANK = 384
KV_LORA_RANK = 256
D_MLA_OUT = MLA_HEADS * V_HEAD_DIM
ROPE_THETA = 10000.0
D_FF = 2816
RMS_EPS = 1e-6
NEG_INF = -1e30

LANES = 128
SUBLANES = 8
HEAD_SLAB = 2 * LANES
BF16_ROWS = 16
VT_ROWS = V_HEAD_DIM + BF16_ROWS
Z_LRU_X = 0
Z_LRU_GATE = Z_LRU_X + D_LRU
Z_CQ = Z_LRU_GATE + D_LRU
Z_CKV = Z_CQ + Q_LORA_RANK
Z_KPE = Z_CKV + KV_LORA_RANK
Z_WIDTH = Z_KPE + LANES
MXU_TILE = 256
LRU_GROUP = MXU_TILE

TS_IN = 512
IN_TILES_PER_STEP = 4
Q_TILES_PER_STEP = 4
SCORE_LOOKAHEAD = 2
TM_MIX = 1024
TM_MEM = 1024
ROW_SUB = 256
FF_CHUNKS = ((0, 1536), (1536, D_FF))
VMEM_LIMIT = 60 * 1024 * 1024


def _rms(x, g):
    ms = jnp.mean(x * x, axis=-1, keepdims=True)
    return x * lax.rsqrt(ms + RMS_EPS) * g


def _rms_plain(x):
    ms = jnp.mean(x * x, axis=-1, keepdims=True)
    return x * lax.rsqrt(ms + RMS_EPS)


def _gelu_tanh(x):
    k1 = -2.0 * math.sqrt(2.0 / math.pi) * math.log2(math.e)
    k2 = k1 * 0.044715
    return x / (1.0 + jnp.exp2(x * (k1 + k2 * (x * x))))


def _dot(a, b):
    return jnp.dot(a, b, preferred_element_type=F32)


def _dot_nt(a, b):
    return lax.dot_general(a, b, (((1,), (1,)), ((), ())), preferred_element_type=F32)


def _const_spec(shape):
    return pl.BlockSpec(shape, lambda *_: (0,) * len(shape))


def _shift_rows(x, prev, d):
    r = pltpu.roll(x, d, 0)
    row = lax.broadcasted_iota(jnp.int32, prev.shape, 0)
    head = jnp.where(row < d, pltpu.roll(prev, d, 0), r[0:SUBLANES])
    return jnp.concatenate([head, r[SUBLANES:]], axis=0)


def _linear_scan(a, u, h0):
    t, c = a.shape
    groups = t // SUBLANES
    a = a.reshape(groups, SUBLANES, c)
    u = u.reshape(groups, SUBLANES, c)
    row = lax.broadcasted_iota(jnp.int32, a.shape, 1)
    for s in (1, 2, 4):
        a_s = pltpu.roll(a, s, 1)
        u_s = pltpu.roll(u, s, 1)
        m = row >= s
        u = jnp.where(m, a * u_s + u, u)
        a = jnp.where(m, a * a_s, a)
    hs = []
    h = h0
    for g in range(groups):
        hg = a[g] * h + u[g]
        hs.append(hg)
        h = hg[SUBLANES - 1:SUBLANES]
    return jnp.concatenate(hs, axis=0)


def _rope(t, cos, sin):
    half = QK_ROPE_DIM // 2
    return t * cos + pltpu.roll(t, half, 1) * sin - pltpu.roll(t, LANES - half, 1) * sin


def _in_proj_kernel(x_ref, pos_ref, invf_ref, win_ref, convw_ref, convb_ref, wg_ref,
                    ba_ref, bx_ref, lam_ref, wuqt_ref, wkn_ref, wvt_ref,
                    qt_ref, k_ref, vt_ref, ylru_ref, xprev_ref, hprev_ref):
    ts = TS_IN
    tiles = range(x_ref.shape[0] // ts)
    rows = [slice(n * ts, (n + 1) * ts) for n in tiles]
    half = QK_ROPE_DIM // 2

    @pl.when(pl.program_id(1) == 0)
    def _():
        xprev_ref[...] = jnp.zeros_like(xprev_ref)
        hprev_ref[...] = jnp.zeros_like(hprev_ref)

    xn = [_rms_plain(x_ref[r, :]).astype(BF16) for r in rows]
    z_tiles = [{} for _ in tiles]

    def z(n, lo, hi):
        parts = []
        for j in range(lo // MXU_TILE, -(-hi // MXU_TILE)):
            if j not in z_tiles[n]:
                z_tiles[n][j] = _dot(xn[n], win_ref[:, j * MXU_TILE:(j + 1) * MXU_TILE])
            a, b = max(lo, j * MXU_TILE), min(hi, (j + 1) * MXU_TILE)
            parts.append(z_tiles[n][j][:, a - j * MXU_TILE:b - j * MXU_TILE])
        return parts[0] if len(parts) == 1 else jnp.concatenate(parts, axis=-1)

    xl = [z(n, Z_LRU_X, Z_LRU_GATE) for n in tiles]
    u = []
    prev = xprev_ref[...]
    for n in tiles:
        un = convb_ref[...] + convw_ref[CONV_WIDTH - 1:CONV_WIDTH, :] * xl[n]
        for d in range(1, CONV_WIDTH):
            k = CONV_WIDTH - 1 - d
            un = un + convw_ref[k:k + 1, :] * _shift_rows(xl[n], prev, d)
        u.append(un)
        prev = xl[n][ts - SUBLANES:ts]
    xprev_ref[...] = prev
    pre = [[_dot(u[n][:, j * LRU_GROUP:(j + 1) * LRU_GROUP].astype(BF16), wg_ref[j])
            for j in range(D_LRU // LRU_GROUP)] for n in tiles]

    z_gate = [z(n, Z_LRU_GATE, Z_CQ) for n in tiles]
    z_cq = [z(n, Z_CQ, Z_CKV) for n in tiles]
    z_ckv = [z(n, Z_CKV, Z_KPE) for n in tiles]
    z_kpe = [z(n, Z_KPE, Z_WIDTH) for n in tiles]
    cqn = [_rms_plain(v) for v in z_cq]
    ckvn = [_rms_plain(v) for v in z_ckv]
    q_t = [_dot(wuqt_ref[...], v.T.astype(BF16)) for v in cqn]
    v_t = [_dot(wvt_ref[...], v.T.astype(BF16)) for v in ckvn]
    kn = [_dot(v.astype(BF16), wkn_ref[...]) for v in ckvn]

    nlam = -lam_ref[...]
    softplus = jnp.maximum(nlam, 0.0) + jnp.log1p(jnp.exp(-jnp.abs(nlam)))
    h_last = hprev_ref[SUBLANES - 1:SUBLANES, :]
    for n in tiles:
        r = 1.0 / (1.0 + jnp.exp2(
            jnp.concatenate([p[:, :LRU_GROUP] for p in pre[n]], axis=-1) + ba_ref[...]))
        gi = 1.0 / (1.0 + jnp.exp2(
            jnp.concatenate([p[:, LRU_GROUP:] for p in pre[n]], axis=-1) + bx_ref[...]))
        a = jnp.exp2(r * ((-LRU_C * math.log2(math.e)) * softplus))
        one_m_a2 = 1.0 - a * a
        root = jnp.where(one_m_a2 > 0.0, one_m_a2 * lax.rsqrt(one_m_a2), 0.0)
        h = _linear_scan(a, root * (gi * u[n]), h_last)
        h_last = h[ts - 1:ts]
        if n == tiles[-1]:
            hprev_ref[...] = h[ts - SUBLANES:ts]
        ylru_ref[rows[n], :] = _rms_plain(h * _gelu_tanh(z_gate[n])).astype(BF16)

    for n in tiles:
        ang_t = invf_ref[...] * pos_ref[:, rows[n]].astype(F32)
        cos_t = jnp.cos(ang_t)
        sin_t = jnp.sin(ang_t)
        for hd in range(MLA_HEADS):
            r0 = hd * HEAD_SLAB
            pe = r0 + QK_NOPE_DIM
            x1 = q_t[n][pe:pe + half]
            x2 = q_t[n][pe + half:pe + 2 * half]
            qt_ref[n, r0:pe, :] = q_t[n][r0:pe].astype(BF16)
            qt_ref[n, pe:pe + half, :] = (x1 * cos_t - x2 * sin_t).astype(BF16)
            qt_ref[n, pe + half:pe + 2 * half, :] = (x2 * cos_t + x1 * sin_t).astype(BF16)
            qt_ref[n, pe + 2 * half:r0 + HEAD_SLAB, :] = jnp.zeros(
                (HEAD_SLAB - QK_NOPE_DIM - 2 * half, ts), BF16)
            v0 = hd * VT_ROWS
            vt_ref[n, v0:v0 + V_HEAD_DIM, :] = v_t[n][hd * V_HEAD_DIM:(hd + 1) * V_HEAD_DIM].astype(BF16)
            vt_ref[n, v0 + V_HEAD_DIM:v0 + VT_ROWS, :] = jnp.ones((BF16_ROWS, ts), BF16)
        pad = jnp.zeros((LANES - 2 * half, ts), F32)
        cos = jnp.concatenate([cos_t, cos_t, pad], axis=0).T
        sin = jnp.concatenate([sin_t, sin_t, pad], axis=0).T
        kpe = _rope(z_kpe[n], cos, sin).astype(BF16)
        for hd in range(MLA_HEADS):
            c0 = hd * HEAD_SLAB
            k_ref[rows[n], c0:c0 + LANES] = kn[n][:, hd * LANES:(hd + 1) * LANES].astype(BF16)
            k_ref[rows[n], c0 + LANES:c0 + HEAD_SLAB] = kpe


def _in_proj(x, pos_row, invf, win, convw, convb, wg, ba, bx, lam, wuqt, wkn, wvt):
    bsz, s, _ = x.shape
    ts = TS_IN
    step = TS_IN * IN_TILES_PER_STEP
    tok = lambda w: pl.BlockSpec((None, step, w), lambda b, i: (b, i, 0))
    in_specs = [
        tok(D_MODEL), pl.BlockSpec((None, 1, step), lambda b, i: (b, 0, i)),
        _const_spec((QK_ROPE_DIM // 2, 1)),
        _const_spec((D_MODEL, Z_WIDTH)), _const_spec((CONV_WIDTH, D_LRU)), _const_spec((1, D_LRU)),
        _const_spec((D_LRU // LRU_GROUP, LRU_GROUP, 2 * LRU_GROUP)),
        _const_spec((1, D_LRU)), _const_spec((1, D_LRU)), _const_spec((1, D_LRU)),
        _const_spec((MLA_HEADS * HEAD_SLAB, Q_LORA_RANK)),
        _const_spec((KV_LORA_RANK, MLA_HEADS * QK_NOPE_DIM)),
        _const_spec((D_MLA_OUT, KV_LORA_RANK)),
    ]
    tile_t = lambda w: pl.BlockSpec((None, IN_TILES_PER_STEP, w, ts), lambda b, i: (b, i, 0, 0))
    out_shape = (
        jax.ShapeDtypeStruct((bsz, s // ts, MLA_HEADS * HEAD_SLAB, ts), BF16),
        jax.ShapeDtypeStruct((bsz, s, MLA_HEADS * HEAD_SLAB), BF16),
        jax.ShapeDtypeStruct((bsz, s // ts, MLA_HEADS * VT_ROWS, ts), BF16),
        jax.ShapeDtypeStruct((bsz, s, D_LRU), BF16),
    )
    out_specs = (tile_t(MLA_HEADS * HEAD_SLAB), tok(MLA_HEADS * HEAD_SLAB),
                 tile_t(MLA_HEADS * VT_ROWS), tok(D_LRU))
    return pl.pallas_call(
        _in_proj_kernel,
        out_shape=out_shape,
        grid=(bsz, s // step),
        in_specs=in_specs,
        out_specs=out_specs,
        scratch_shapes=[pltpu.VMEM((SUBLANES, D_LRU), F32), pltpu.VMEM((SUBLANES, D_LRU), F32)],
        compiler_params=pltpu.CompilerParams(dimension_semantics=("parallel", "arbitrary"),
                                             vmem_limit_bytes=VMEM_LIMIT),
        name="in_proj",
    )(x, pos_row, invf, win, convw, convb, wg, ba, bx, lam, wuqt, wkn, wvt)


def _mla_attn_kernel(qt_ref, k_ref, vt_ref, o_ref):
    n_local, _, tq = qt_ref.shape
    nq = vt_ref.shape[0]
    i = pl.program_id(1)
    hq = tq // 2

    def scores(ql, j, hd, diagonal):
        slab = slice(hd * HEAD_SLAB, (hd + 1) * HEAD_SLAB)
        if not diagonal:
            st = _dot(k_ref[j * tq:(j + 1) * tq, slab], qt_ref[ql, slab, :])
            return [(0, 0, False, st[:, :hq]), (1, 0, False, st[:, hq:])]
        return [(0, 0, True, _dot(k_ref[j * tq:j * tq + hq, slab], qt_ref[ql, slab, 0:hq])),
                (1, 0, True, _dot(k_ref[j * tq:(j + 1) * tq, slab], qt_ref[ql, slab, hq:tq]))]

    def finish(ql, acc):
        outs = []
        for hd in range(MLA_HEADS):
            a = jnp.concatenate(acc[hd], axis=1)
            outs.append(a[:V_HEAD_DIM] * (1.0 / a[V_HEAD_DIM:V_HEAD_DIM + 1]))
        out_t = jnp.concatenate(outs, axis=0)
        ms = jnp.mean(out_t * out_t, axis=0, keepdims=True)
        o_ref[ql * tq:(ql + 1) * tq, :] = (out_t * lax.rsqrt(ms + RMS_EPS)).astype(BF16).T

    def attend(first_tile):
        units = [(ql, j, hd, j == first_tile + ql)
                 for ql in range(n_local) for j in range(first_tile + ql + 1)
                 for hd in range(MLA_HEADS)]
        ahead = [scores(*u) for u in units[:SCORE_LOOKAHEAD]]
        m = acc = None
        for n, (ql, j, hd, diagonal) in enumerate(units):
            if j == 0 and hd == 0:
                m = [[jnp.full((1, hq), NEG_INF, F32)] * 2 for _ in range(MLA_HEADS)]
                acc = [[None, None] for _ in range(MLA_HEADS)]
            pieces = ahead.pop(0)
            if n + SCORE_LOOKAHEAD < len(units):
                ahead.append(scores(*units[n + SCORE_LOOKAHEAD]))
            vrows = slice(hd * VT_ROWS, (hd + 1) * VT_ROWS)
            for half, k0, masked, st in pieces:
                nk = st.shape[0]
                if masked:
                    kid = lax.broadcasted_iota(jnp.int32, (nk, hq), 0) + k0
                    qid = lax.broadcasted_iota(jnp.int32, (nk, hq), 1) + half * hq
                    st = jnp.where(qid >= kid, st, NEG_INF)
                m_old = m[hd][half]
                m_new = jnp.maximum(m_old, jnp.max(st, axis=0, keepdims=True))
                p = jnp.exp2(st - m_new).astype(BF16)
                pv = _dot(vt_ref[j, vrows, k0:k0 + nk], p)
                old = acc[hd][half]
                acc[hd][half] = pv if old is None else jnp.exp2(m_old - m_new) * old + pv
                m[hd][half] = m_new
            if diagonal and hd == MLA_HEADS - 1:
                finish(ql, acc)

    for n in range(nq // n_local):
        @pl.when(i == n)
        def _(n=n):
            attend(n * n_local)


def _mla_attn(qt, k, vt):
    bsz, nq, _, tq = qt.shape
    s = k.shape[1]
    per = Q_TILES_PER_STEP
    return pl.pallas_call(
        _mla_attn_kernel,
        out_shape=jax.ShapeDtypeStruct((bsz, s, D_MLA_OUT), BF16),
        grid=(bsz, nq // per),
        in_specs=[
            pl.BlockSpec((None, per, MLA_HEADS * HEAD_SLAB, tq), lambda b, i: (b, i, 0, 0)),
            pl.BlockSpec((None, s, MLA_HEADS * HEAD_SLAB), lambda b, i: (b, 0, 0)),
            pl.BlockSpec((None, nq, MLA_HEADS * VT_ROWS, tq), lambda b, i: (b, 0, 0, 0)),
        ],
        out_specs=pl.BlockSpec((None, per * tq, D_MLA_OUT), lambda b, i: (b, i, 0)),
        compiler_params=pltpu.CompilerParams(dimension_semantics=("parallel", "arbitrary"),
                                             vmem_limit_bytes=VMEM_LIMIT),
        name="mla_attn",
    )(qt, k, vt)


def _mem_kv_kernel(mem_ref, gmem_ref, wmk_ref, wmv_ref, mk_ref, mv_ref):
    mn = _rms(mem_ref[...], gmem_ref[...]).astype(BF16)
    mk_ref[...] = _dot(mn, wmk_ref[...]).astype(BF16)
    mv_ref[...] = _dot(mn, wmv_ref[...]).astype(BF16)


def _mem_kv(mem, gmem, wmk, wmv):
    m, _ = mem.shape
    tm = TM_MEM
    tok = pl.BlockSpec((tm, D_MODEL), lambda i: (i, 0))
    sq = _const_spec((D_MODEL, D_MODEL))
    out = jax.ShapeDtypeStruct((m, D_MODEL), BF16)
    return pl.pallas_call(
        _mem_kv_kernel,
        out_shape=(out, out),
        grid=(m // tm,),
        in_specs=[tok, _const_spec((1, D_MODEL)), sq, sq],
        out_specs=(tok, tok),
        compiler_params=pltpu.CompilerParams(dimension_semantics=("parallel",),
                                             vmem_limit_bytes=VMEM_LIMIT),
        name="mem_kv",
    )(mem, gmem, wmk, wmv)


def _mix_ffn_kernel(ylru_ref, ymla_ref, x_ref, mk_ref, mv_ref, wout_ref, gpost_ref, gpre_ref,
                    wmq_ref, wmo_ref, gpm_ref, gpf_ref, wg_ref, wu_ref, wd_ref, gqf_ref,
                    h_ref, o_scr):
    subs = [slice(r, r + ROW_SUB) for r in range(0, x_ref.shape[0], ROW_SUB)]
    yo = [_dot(ylru_ref[r, :], wout_ref[0:D_LRU, :])
          + _dot(ymla_ref[r, :], wout_ref[D_LRU:, :]) for r in subs]
    h1 = [x_ref[r, :] + _rms(y, gpost_ref[...]) for r, y in zip(subs, yo)]

    xn = [_rms(h, gpre_ref[...]).astype(BF16) for h in h1]
    mq = [(_dot(v, wmq_ref[...]) * (1.0 / math.sqrt(MEM_HEAD_DIM))).astype(BF16) for v in xn]
    for r, q in zip(subs, mq):
        for hd in range(MEM_HEADS):
            sl = slice(hd * MEM_HEAD_DIM, (hd + 1) * MEM_HEAD_DIM)
            s = _dot_nt(q[:, sl], mk_ref[:, sl])
            m = jnp.max(s, axis=-1, keepdims=True)
            p = jnp.exp(s - m)
            l = jnp.sum(p, axis=-1, keepdims=True)
            o_scr[r, sl] = (_dot(p.astype(BF16), mv_ref[:, sl]) / l).astype(BF16)
    mo = [_dot(o_scr[r, :], wmo_ref[...]) for r in subs]
    for r, h, o in zip(subs, h1, mo):
        h_ref[r, :] = h + _rms(o, gpm_ref[...])

    xn = [None] * len(subs)
    f = [None] * len(subs)
    for lo, hi in FF_CHUNKS:
        for n, r in enumerate(subs):
            if xn[n] is None:
                xn[n] = _rms(h_ref[r, :], gpf_ref[...]).astype(BF16)
            v = xn[n]
            act = (jax.nn.silu(_dot(v, wg_ref[:, lo:hi])) * _dot(v, wu_ref[:, lo:hi])).astype(BF16)
            part = _dot(act, wd_ref[lo:hi, :])
            f[n] = part if f[n] is None else f[n] + part
    for r, fr in zip(subs, f):
        h_ref[r, :] = h_ref[r, :] + _rms(fr, gqf_ref[...])


def _mix_ffn(ylru, ymla, x, mk, mv, wout, gpost, gpre, wmq, wmo, gpm, gpf, wg, wu, wd, gqf):
    bsz, s, _ = x.shape
    tm = TM_MIX
    single = pl.Buffered(1)
    tok = lambda w: pl.BlockSpec((None, tm, w), lambda b, i: (b, i, 0))
    memspec = pl.BlockSpec((None, N_MEM, D_MODEL), lambda b, i: (b, 0, 0))
    sq = pl.BlockSpec((D_MODEL, D_MODEL), lambda b, i: (0, 0), pipeline_mode=single)
    wide = pl.BlockSpec((D_MODEL, D_FF), lambda b, i: (0, 0), pipeline_mode=single)
    tall = pl.BlockSpec((D_FF, D_MODEL), lambda b, i: (0, 0), pipeline_mode=single)
    vec = _const_spec((1, D_MODEL))
    return pl.pallas_call(
        _mix_ffn_kernel,
        out_shape=jax.ShapeDtypeStruct((bsz, s, D_MODEL), F32),
        grid=(bsz, s // tm),
        in_specs=[tok(D_LRU), tok(D_MLA_OUT), tok(D_MODEL), memspec, memspec, sq, vec, vec, sq,
                  sq, vec, vec, wide, wide, tall, vec],
        out_specs=tok(D_MODEL),
        scratch_shapes=[pltpu.VMEM((tm, D_MODEL), BF16)],
        compiler_params=pltpu.CompilerParams(dimension_semantics=("parallel", "arbitrary"),
                                             vmem_limit_bytes=VMEM_LIMIT),
        name="mix_ffn",
    )(ylru, ymla, x, mk, mv, wout, gpost, gpre, wmq, wmo, gpm, gpf, wg, wu, wd, gqf)


def _block_diag(w):
    eye = jnp.eye(LRU_BLOCKS, dtype=w.dtype)
    return jnp.einsum('hij,hg->higj', w, eye).reshape(D_LRU, D_LRU)


def _gate_weights(wa, wx):
    da, dx = _block_diag(wa) * -math.log2(math.e), _block_diag(wx) * -math.log2(math.e)
    groups = []
    for j in range(D_LRU // LRU_GROUP):
        sl = slice(j * LRU_GROUP, (j + 1) * LRU_GROUP)
        groups.append(jnp.concatenate([da[sl, sl], dx[sl, sl]], axis=1))
    return jnp.stack(groups).astype(BF16)


def _pad_in_proj(w_in, gain):
    w = gain[:, None] * w_in
    pad = jnp.zeros((D_MODEL, Z_WIDTH - w.shape[1]), w.dtype)
    return jnp.concatenate([w, pad], axis=1).astype(BF16)


def _q_weights_t(w_uq, gain):
    dh = QK_NOPE_DIM + QK_ROPE_DIM
    scale = math.log2(math.e) / math.sqrt(dh)
    w = (scale * gain[:, None] * w_uq).reshape(Q_LORA_RANK, MLA_HEADS, dh)
    pad = jnp.zeros((Q_LORA_RANK, MLA_HEADS, HEAD_SLAB - dh), w.dtype)
    w = jnp.concatenate([w, pad], axis=-1).reshape(Q_LORA_RANK, MLA_HEADS * HEAD_SLAB)
    return w.T.astype(BF16)


def _kv_weights(w_ukv, gain):
    w = (gain[:, None] * w_ukv).reshape(KV_LORA_RANK, MLA_HEADS, QK_NOPE_DIM + V_HEAD_DIM)
    kn = w[:, :, :QK_NOPE_DIM].reshape(KV_LORA_RANK, MLA_HEADS * QK_NOPE_DIM)
    vv = w[:, :, QK_NOPE_DIM:].reshape(KV_LORA_RANK, MLA_HEADS * V_HEAD_DIM)
    return kn.astype(BF16), vv.T.astype(BF16)


def _rope_freq_column():
    inv_freq = ROPE_THETA ** (-jnp.arange(0, QK_ROPE_DIM, 2, dtype=F32) / QK_ROPE_DIM)
    return inv_freq.reshape(QK_ROPE_DIM // 2, 1)


def kernel(x, mem, positions, g_pre_mix, w_in, conv_w, conv_b, lru_wa, lru_ba, lru_wx, lru_bx, lru_lambda, g_q_lat, w_uq, g_kv_lat, w_ukv, g_lru_out, g_mla_out, w_out, g_post_mix, g_pre_mem, g_mem_kv, w_mq, w_mk, w_mv, w_mo, g_post_mem, g_pre_ffn, w_gate, w_up, w_down, g_post_ffn):
    bsz, s, d = x.shape
    depth = w_in.shape[0]
    assert d == D_MODEL and s % (TS_IN * max(IN_TILES_PER_STEP, Q_TILES_PER_STEP)) == 0
    assert s % TM_MIX == 0 and (bsz * N_MEM) % TM_MEM == 0 and mem.shape == (bsz, N_MEM, D_MODEL)
    pos_row = positions.reshape(bsz, 1, s)
    invf = _rope_freq_column()
    row = lambda a: a.reshape(1, -1)

    h = x
    for l in range(depth):
        wkn, wvt = _kv_weights(w_ukv[l], g_kv_lat[l])
        neg_log2e = -math.log2(math.e)
        qt, k, vt, ylru = _in_proj(
            h, pos_row, invf, _pad_in_proj(w_in[l], g_pre_mix[l]), conv_w[l], row(conv_b[l]),
            _gate_weights(lru_wa[l], lru_wx[l]), row(neg_log2e * lru_ba[l]),
            row(neg_log2e * lru_bx[l]), row(lru_lambda[l]),
            _q_weights_t(w_uq[l], g_q_lat[l]), wkn, wvt)
        ymla = _mla_attn(qt, k, vt)
        wout = (jnp.concatenate([g_lru_out[l], g_mla_out[l]])[:, None] * w_out[l]).astype(BF16)
        mk, mv = _mem_kv(mem.reshape(bsz * N_MEM, d), row(g_mem_kv[l]), w_mk[l].astype(BF16),
                         w_mv[l].astype(BF16))
        h = _mix_ffn(ylru, ymla, h, mk.reshape(bsz, N_MEM, d), mv.reshape(bsz, N_MEM, d), wout,
                     row(g_post_mix[l]), row(g_pre_mem[l]), w_mq[l].astype(BF16),
                     w_mo[l].astype(BF16), row(g_post_mem[l]), row(g_pre_ffn[l]),
                     w_gate[l].astype(BF16), w_up[l].astype(BF16), w_down[l].astype(BF16),
                     row(g_post_ffn[l]))
    return h
```
